```python
import math
import jax
import jax.numpy as jnp
from jax import lax
import numpy as np

D_MODEL = 2048
BATCH = 8
SEQ = 4096
DEPTH = 4

GRID_W = 64
CTX_LEN = 256
N_MIXERS = 3
EXPAND = 2
D_INNER = EXPAND * D_MODEL
EPS = 1e-6
CONV_WIDTH = 31
MLSTM_HEADS = 8
MLSTM_HEAD_DIM = D_INNER // MLSTM_HEADS
QKV_BLOCK = 4
MLSTM_CONV_WIDTH = 3
MLSTM_CHUNK = 64
HYENA_SHORT_WIDTH = 3
HYENA_EMB_DIM = 33
HYENA_FILTER_WIDTH = 64
HYENA_FAST_DECAY = 0.3
HYENA_SLOW_DECAY = 1.5
HYENA_DECAY_TARGET = 1e-2
N_A = (DEPTH + 2) // 3
N_B = (DEPTH + 1) // 3
N_C = DEPTH // 3

kernel_name = 'hybrid_conformer_mlstm_hyena_dit'


def _rmsnorm(x, g):
    xf = x.astype(jnp.float32)
    y = xf * lax.rsqrt(jnp.mean(xf * xf, axis=-1, keepdims=True) + EPS)
    return (y * g.astype(jnp.float32)).astype(x.dtype)


def _ada_rmsnorm(x, g, shift, scale):
    return _rmsnorm(x, g) * (1 + scale) + shift


def _layernorm(x, g, b):
    xf = x.astype(jnp.float32)
    mu = jnp.mean(xf, axis=-1, keepdims=True)
    var = jnp.mean(jnp.square(xf - mu), axis=-1, keepdims=True)
    y = (xf - mu) * lax.rsqrt(var + EPS) * g.astype(jnp.float32) + b.astype(jnp.float32)
    return y.astype(x.dtype)


def _dwconv(x, w, b):
    width = w.shape[0]
    pad = width // 2
    y = lax.conv_general_dilated(
        x, w[:, None, :].astype(x.dtype), window_strides=(1,), padding=[(pad, pad)],
        dimension_numbers=('NWC', 'WIO', 'NWC'), feature_group_count=x.shape[-1])
    return y + b.astype(x.dtype)


def _conv_module(u, w_in, dw_w, dw_b, ln_g, ln_b, w_out, rows):
    a, g, z = jnp.split(u @ w_in, 3, axis=-1)
    y = a * jax.nn.sigmoid(g)
    if rows is None:
        y = _dwconv(y, dw_w, dw_b)
    else:
        bsz, n, e = y.shape
        y = _dwconv(y.reshape(bsz * rows, GRID_W, e), dw_w, dw_b).reshape(bsz, n, e)
    y = jax.nn.silu(_layernorm(y, ln_g, ln_b))
    return (y * jax.nn.silu(z)) @ w_out


def _headwise(x, w):
    g, bi, bo = w.shape
    xs = x.reshape(x.shape[:-1] + (g, bi))
    return jnp.einsum('btgi,gio->btgo', xs, w).reshape(x.shape[:-1] + (g * bo,))


def _reverse_segments(a, n_first):
    return jnp.concatenate([jnp.flip(a[:, :, :n_first], 2), jnp.flip(a[:, :, n_first:], 2)], axis=2)


def _mlstm_scan(q, k, v, li, lf):
    bsz, nh, t, dk = q.shape
    dv = v.shape[-1]
    L = MLSTM_CHUNK
    nc = t // L
    q = q * (dk ** -0.5)
    tril = jnp.tril(jnp.ones((L, L), dtype=bool))

    def chunks(a):
        return jnp.moveaxis(a.reshape((bsz, nh, nc, L) + a.shape[3:]), 2, 0)

    def step(carry, inp):
        C, n, m = carry
        qc, kc, vc, lic, lfc = inp
        b = jnp.cumsum(lfc, axis=-1)
        log_d = b[..., :, None] - b[..., None, :] + lic[..., None, :]
        log_d = jnp.where(tril, log_d, -jnp.inf)
        log_inter = b + m[..., None]
        m_t = jnp.maximum(log_inter, jnp.max(log_d, axis=-1))
        d = jnp.exp(log_d - m_t[..., None])
        a = jnp.exp(log_inter - m_t)
        s = jnp.einsum('bhtd,bhsd->bhts', qc, kc) * d
        num = a[..., None] * jnp.einsum('bhtd,bhde->bhte', qc, C) + jnp.einsum('bhts,bhse->bhte', s, vc)
        den = a * jnp.einsum('bhtd,bhd->bht', qc, n) + jnp.sum(s, axis=-1)
        h = num / jnp.maximum(jnp.abs(den), jnp.exp(-m_t))[..., None]
        b_last = b[..., -1]
        log_w = b_last[..., None] - b + lic
        m_new = jnp.maximum(b_last + m, jnp.max(log_w, axis=-1))
        w = jnp.exp(log_w - m_new[..., None])
        decay = jnp.exp(b_last + m - m_new)
        kw = kc * w[..., None]
        C_new = decay[..., None, None] * C + jnp.einsum('bhsd,bhse->bhde', kw, vc)
        n_new = decay[..., None] * n + jnp.sum(kw, axis=2)
        return (C_new, n_new, m_new), h

    init = (jnp.zeros((bsz, nh, dk, dv), jnp.float32), jnp.zeros((bsz, nh, dk), jnp.float32),
            jnp.full((bsz, nh), -1e30, jnp.float32))
    _, h = lax.scan(step, init, (chunks(q), chunks(k), chunks(v), chunks(li), chunks(lf)))
    return jnp.moveaxis(h, 0, 2).reshape(bsz, nh, t, dv)


def _head_layernorm(h, g):
    mu = jnp.mean(h, axis=-1, keepdims=True)
    var = jnp.mean(jnp.square(h - mu), axis=-1, keepdims=True)
    return (h - mu) * lax.rsqrt(var + EPS) * g.astype(jnp.float32).reshape(h.shape[1], 1, h.shape[3])


def _mlstm_module(u_ctx, u_lat, w_in, conv_w, conv_b, w_q, w_k, w_v, w_o, b_o,
                  w_gates, b_gates, mh_g, skip, w_out, ctx_out):
    n_ctx = u_ctx.shape[1]

    def streams(u):
        xm, z = jnp.split(u @ w_in, 2, axis=-1)
        xc = jax.nn.silu(_dwconv(xm, conv_w, conv_b))
        return xm, xc, z

    xm_c, xc_c, z_c = streams(u_ctx)
    xm_l, xc_l, z_l = streams(u_lat)
    xm = jnp.concatenate([xm_c, xm_l], axis=1)
    xc = jnp.concatenate([xc_c, xc_l], axis=1)
    z = jnp.concatenate([z_c, z_l], axis=1)
    q = _headwise(xc, w_q)
    k = _headwise(xc, w_k)
    v = _headwise(xm, w_v)
    o = jax.nn.sigmoid(_headwise(xc, w_o) + b_o)
    bsz, t, e = q.shape
    wg = w_gates.reshape(3, e, -1)
    gates = (q @ wg[0] + k @ wg[1] + v @ wg[2] + b_gates).astype(jnp.float32)
    gates = gates.reshape(bsz, t, 2, 2, MLSTM_HEADS).transpose(2, 3, 0, 4, 1)

    def heads(a):
        return a.reshape(bsz, t, MLSTM_HEADS, -1).transpose(0, 2, 1, 3).astype(jnp.float32)

    qh, kh, vh = heads(q), heads(k), heads(v)

    def rev(a):
        return _reverse_segments(a, n_ctx)

    h_fwd = _mlstm_scan(qh, kh, vh, gates[0, 0], jax.nn.log_sigmoid(gates[0, 1]))
    h_bwd = rev(_mlstm_scan(rev(qh), rev(kh), rev(vh), rev(gates[1, 0]),
                            rev(jax.nn.log_sigmoid(gates[1, 1]))))
    h = _head_layernorm(h_fwd + h_bwd, mh_g).transpose(0, 2, 1, 3).reshape(bsz, t, e).astype(u_lat.dtype)
    y = (o * h + skip * xc) * jax.nn.silu(z)
    if ctx_out:
        y = y @ w_out
        return y[:, :n_ctx], y[:, n_ctx:]
    return None, y[:, n_ctx:] @ w_out


def _hyena_filters(L, f_w1, f_b1, f_freq1, f_w2, f_b2, f_freq2, f_w3, f_b3):
    f32 = jnp.float32
    t = jnp.linspace(0.0, 1.0, L, dtype=f32)[:, None]
    bands = (HYENA_EMB_DIM - 1) // 2
    ang = 2.0 * math.pi * jnp.arange(L, dtype=f32)[:, None] / L
    fr = jnp.linspace(1e-4, bands - 1, bands, dtype=f32)[None, :]
    feat = jnp.concatenate([t, jnp.cos(fr * ang), -jnp.sin(fr * ang)], axis=-1)
    h = jnp.sin(f_freq1.astype(f32) * (feat @ f_w1.astype(f32) + f_b1.astype(f32)))
    h = jnp.sin(f_freq2.astype(f32) * (h @ f_w2.astype(f32) + f_b2.astype(f32)))
    h = h @ f_w3.astype(f32) + f_b3.astype(f32)
    e = h.shape[-1] // 2
    lo = math.log(HYENA_DECAY_TARGET) / HYENA_FAST_DECAY
    hi = math.log(HYENA_DECAY_TARGET) / HYENA_SLOW_DECAY
    deltas = jnp.abs(jnp.linspace(lo, hi, e, dtype=f32))
    h = h * jnp.exp(-t * jnp.tile(deltas, 2)[None, :])
    return h[:, :e], h[:, e:]


def _bidir_long_conv(u, hf, hb, bias):
    n = u.shape[1]
    uf = u.astype(jnp.float32)
    k = jnp.concatenate([hf, jnp.zeros_like(hf[:1]), jnp.flip(hb[1:], axis=0)], axis=0)
    spec = jnp.fft.rfft(uf, n=2 * n, axis=1) * jnp.fft.rfft(k, axis=0)[None]
    y = jnp.fft.irfft(spec, n=2 * n, axis=1)[:, :n]
    return (y + uf * bias.astype(jnp.float32)).astype(u.dtype)


def _hyena_module(u, w_in, conv_w, conv_b, f_w1, f_b1, f_freq1, f_w2, f_b2, f_freq2,
                  f_w3, f_b3, h_bias, w_out):
    e = w_out.shape[0]
    p = u @ w_in
    g = _dwconv(p[..., :3 * e], conv_w, conv_b)
    x0, x1, v = jnp.split(g, 3, axis=-1)
    z = p[..., 3 * e:]
    hf, hb = _hyena_filters(u.shape[1], f_w1, f_b1, f_freq1, f_w2, f_b2, f_freq2, f_w3, f_b3)
    y = x0 * _bidir_long_conv(x1 * v, hf, hb, h_bias)
    return (y * jax.nn.silu(z)) @ w_out


def setup_inputs(seed: int = 0) -> dict:
    key = jax.random.key(seed)
    ks = iter(jax.random.split(key, 64))
    D, E, H = D_MODEL, D_INNER, MLSTM_HEADS
    F = HYENA_FILTER_WIDTH

    def nrm(shape, std):
        return std * jax.random.normal(next(ks), shape, jnp.float32)

    def gain(shape):
        return 1.0 + nrm(shape, 0.02)

    gate_bias = jnp.concatenate([jnp.zeros((H,), jnp.float32), jnp.linspace(3.0, 6.0, H, dtype=jnp.float32),
                                 jnp.zeros((H,), jnp.float32), jnp.linspace(3.0, 6.0, H, dtype=jnp.float32)])
    return {
        'x': nrm((BATCH, SEQ, D), 1.0),
        'c': nrm((BATCH, D), 1.0),
        'ctx': nrm((BATCH, CTX_LEN, D), 1.0),
        'c_ctx': nrm((D,), 1.0),
        'norm_g': gain((DEPTH, D)),
        'ada_w': nrm((DEPTH, D, 3 * D), 0.5 * D ** -0.5),
        'ada_b': nrm((DEPTH, 3 * D), 0.02),
        'final_g': gain((D,)),
        'cv_w_in': nrm((N_A, D, 3 * E), D ** -0.5),
        'cv_dw_w': nrm((N_A, CONV_WIDTH, E), CONV_WIDTH ** -0.5),
        'cv_dw_b': nrm((N_A, E), 0.02),
        'cv_ln_g': gain((N_A, E)),
        'cv_ln_b': nrm((N_A, E), 0.02),
        'cv_w_out': nrm((N_A, E, D), E ** -0.5),
        'ml_w_in': nrm((N_B, D, 2 * E), D ** -0.5),
        'ml_conv_w': nrm((N_B, MLSTM_CONV_WIDTH, E), MLSTM_CONV_WIDTH ** -0.5),
        'ml_conv_b': nrm((N_B, E), 0.02),
        'ml_w_q': nrm((N_B, E // QKV_BLOCK, QKV_BLOCK, QKV_BLOCK), QKV_BLOCK ** -0.5),
        'ml_w_k': nrm((N_B, E // QKV_BLOCK, QKV_BLOCK, QKV_BLOCK), QKV_BLOCK ** -0.5),
        'ml_w_v': nrm((N_B, E // QKV_BLOCK, QKV_BLOCK, QKV_BLOCK), QKV_BLOCK ** -0.5),
        'ml_w_o': nrm((N_B, E // QKV_BLOCK, QKV_BLOCK, QKV_BLOCK), QKV_BLOCK ** -0.5),
        'ml_b_o': nrm((N_B, E), 0.02),
        'ml_w_gates': nrm((N_B, 3 * E, 4 * H), 0.5 * (3 * E) ** -0.5),
        'ml_b_gates': gate_bias[None, :] + nrm((N_B, 4 * H), 0.1),
        'ml_mh_g': gain((N_B, E)),
        'ml_skip': gain((N_B, E)),
        'ml_w_out': nrm((N_B, E, D), E ** -0.5),
        'hy_w_in': nrm((N_C, D, 4 * E), D ** -0.5),
        'hy_conv_w': nrm((N_C, HYENA_SHORT_WIDTH, 3 * E), HYENA_SHORT_WIDTH ** -0.5),
        'hy_conv_b': nrm((N_C, 3 * E), 0.02),
        'hy_f_w1': nrm((N_C, HYENA_EMB_DIM, F), HYENA_EMB_DIM ** -0.5),
        'hy_f_b1': nrm((N_C, F), 0.1),
        'hy_f_freq1': 1.0 + nrm((N_C, F), 0.1),
        'hy_f_w2': nrm((N_C, F, F), F ** -0.5),
        'hy_f_b2': nrm((N_C, F), 0.1),
        'hy_f_freq2': 1.0 + nrm((N_C, F), 0.1),
        'hy_f_w3': nrm((N_C, F, 2 * E), 0.1 * F ** -0.5),
        'hy_f_b3': nrm((N_C, 2 * E), 0.01),
        'hy_h_bias': nrm((N_C, E), 0.5),
        'hy_w_out': nrm((N_C, E, D), E ** -0.5),
    }


def reference(x, c, ctx, c_ctx, norm_g, ada_w, ada_b, final_g,
              cv_w_in, cv_dw_w, cv_dw_b, cv_ln_g, cv_ln_b, cv_w_out,
              ml_w_in, ml_conv_w, ml_conv_b, ml_w_q, ml_w_k, ml_w_v, ml_w_o, ml_b_o,
              ml_w_gates, ml_b_gates, ml_mh_g, ml_skip, ml_w_out,
              hy_w_in, hy_conv_w, hy_conv_b, hy_f_w1, hy_f_b1, hy_f_freq1, hy_f_w2, hy_f_b2,
              hy_f_freq2, hy_f_w3, hy_f_b3, hy_h_bias, hy_w_out):
    rows = x.shape[1] // GRID_W
    readers = [i for i in range(DEPTH) if i % N_MIXERS == 1]
    last_reader = readers[-1] if readers else -1
    silu_c = jax.nn.silu(c)
    silu_cc = jax.nn.silu(c_ctx)
    h_lat, h_ctx = x, ctx
    for i in range(DEPTH):
        kind, j = i % N_MIXERS, i // N_MIXERS
        ctx_in = i <= last_reader
        ctx_out = i < last_reader
        sh, sc, gt = jnp.split(silu_c @ ada_w[i] + ada_b[i], 3, axis=-1)
        u_lat = _ada_rmsnorm(h_lat, norm_g[i], sh[:, None], sc[:, None])
        if ctx_in:
            sh_c, sc_c, gt_c = jnp.split(silu_cc @ ada_w[i] + ada_b[i], 3, axis=-1)
            u_ctx = _ada_rmsnorm(h_ctx, norm_g[i], sh_c, sc_c)
        if kind == 0:
            y_lat = _conv_module(u_lat, cv_w_in[j], cv_dw_w[j], cv_dw_b[j], cv_ln_g[j], cv_ln_b[j],
                                 cv_w_out[j], rows)
            if ctx_out:
                y_ctx = _conv_module(u_ctx, cv_w_in[j], cv_dw_w[j], cv_dw_b[j], cv_ln_g[j], cv_ln_b[j],
                                     cv_w_out[j], None)
        elif kind == 1:
            y_ctx, y_lat = _mlstm_module(u_ctx, u_lat, ml_w_in[j], ml_conv_w[j], ml_conv_b[j], ml_w_q[j],
                                         ml_w_k[j], ml_w_v[j], ml_w_o[j], ml_b_o[j], ml_w_gates[j],
                                         ml_b_gates[j], ml_mh_g[j], ml_skip[j], ml_w_out[j], ctx_out)
        else:
            hy = (hy_w_in[j], hy_conv_w[j], hy_conv_b[j], hy_f_w1[j], hy_f_b1[j], hy_f_freq1[j],
                  hy_f_w2[j], hy_f_b2[j], hy_f_freq2[j], hy_f_w3[j], hy_f_b3[j], hy_h_bias[j], hy_w_out[j])
            y_lat = _hyena_module(u_lat, *hy)
            if ctx_out:
                y_ctx = _hyena_module(u_ctx, *hy)
        h_lat = h_lat + gt[:, None] * y_lat
        if ctx_out:
            h_ctx = h_ctx + gt_c * y_ctx
    return _rmsnorm(h_lat, final_g)
```

```python
import functools
import math

import jax
import jax.numpy as jnp
from jax import lax
from jax.experimental import pallas as pl
from jax.experimental.pallas import tpu as pltpu

GRID_W = 64
N_MIXERS = 3
EPS = 1e-6
MLSTM_HEADS = 8
QKV_BLOCK = 4
MLSTM_CHUNK = 64
HYENA_EMB_DIM = 33
HYENA_FAST_DECAY = 0.3
HYENA_SLOW_DECAY = 1.5
HYENA_DECAY_TARGET = 1e-2

V7X_VMEM_LIMIT_BYTES = 56 * 1024 * 1024

F32 = jnp.float32
BF16 = jnp.bfloat16


def _tiles(m, n, tm, tn):
    tm, tn = min(tm, m), min(tn, n)
    assert m % tm == 0 and n % tn == 0, (m, n, tm, tn)
    return tm, tn


def _norm_matmul_kernel(h_ref, g_ref, sh_ref, sc_ref, w_ref, o_ref, u_ref):
    @pl.when(pl.program_id(1) == 0)
    def _():
        x = h_ref[...]
        ms = jnp.mean(x * x, axis=-1, keepdims=True)
        y = x * lax.rsqrt(ms + EPS) * g_ref[...]
        u_ref[...] = (y * (1.0 + sc_ref[0]) + sh_ref[0]).astype(u_ref.dtype)

    o_ref[...] = jnp.dot(u_ref[...], w_ref[...],
                         preferred_element_type=F32).astype(o_ref.dtype)


def _norm_matmul(h, g, sh, sc, w, *, tm=512, tn=1024, out_dtype=F32):
    bn, t, d = h.shape
    n = w.shape[1]
    tm, tn = _tiles(t, n, tm, tn)
    tiles_per_seq = t // tm
    out = pl.pallas_call(
        _norm_matmul_kernel,
        grid=(bn * tiles_per_seq, n // tn),
        in_specs=[
            pl.BlockSpec((tm, d), lambda i, j: (i, 0)),
            pl.BlockSpec((1, d), lambda i, j: (0, 0)),
            pl.BlockSpec((1, 1, d), lambda i, j: (i // tiles_per_seq, 0, 0)),
            pl.BlockSpec((1, 1, d), lambda i, j: (i // tiles_per_seq, 0, 0)),
            pl.BlockSpec((d, tn), lambda i, j: (0, j)),
        ],
        out_specs=pl.BlockSpec((tm, tn), lambda i, j: (i, j)),
        out_shape=jax.ShapeDtypeStruct((bn * t, n), out_dtype),
        scratch_shapes=[pltpu.VMEM((tm, d), BF16)],
        compiler_params=pltpu.CompilerParams(
            dimension_semantics=("arbitrary", "arbitrary"),
            vmem_limit_bytes=V7X_VMEM_LIMIT_BYTES),
        name="norm_matmul",
    )(h.reshape(bn * t, d), g.reshape(1, d), sh.reshape(bn, 1, d),
      sc.reshape(bn, 1, d), w)
    return out.reshape(bn, t, n)


def _out_matmul_kernel(y_ref, w_ref, h_ref, gt_ref, o_ref):
    acc = jnp.dot(y_ref[...], w_ref[...], preferred_element_type=F32)
    o_ref[...] = h_ref[...] + gt_ref[0] * acc


def _out_matmul(y, w, h, gt, *, tm=512, tn=1024):
    bn, t, e = y.shape
    d = w.shape[1]
    tm, tn = _tiles(t, d, tm, tn)
    tiles_per_seq = t // tm
    out = pl.pallas_call(
        _out_matmul_kernel,
        grid=(bn * tiles_per_seq, d // tn),
        in_specs=[
            pl.BlockSpec((tm, e), lambda i, j: (i, 0)),
            pl.BlockSpec((e, tn), lambda i, j: (0, j)),
            pl.BlockSpec((tm, tn), lambda i, j: (i, j)),
            pl.BlockSpec((1, 1, tn), lambda i, j: (i // tiles_per_seq, 0, j)),
        ],
        out_specs=pl.BlockSpec((tm, tn), lambda i, j: (i, j)),
        out_shape=jax.ShapeDtypeStruct((bn * t, d), F32),
        compiler_params=pltpu.CompilerParams(
            dimension_semantics=("arbitrary", "arbitrary"),
            vmem_limit_bytes=V7X_VMEM_LIMIT_BYTES),
        name="out_matmul",
    )(y.reshape(bn * t, e), w, h.reshape(bn * t, d), gt.reshape(bn, 1, d))
    return out.reshape(bn, t, d)


def _layernorm(x, g, b):
    mu = jnp.mean(x, axis=-1, keepdims=True)
    var = jnp.mean(jnp.square(x - mu), axis=-1, keepdims=True)
    return (x - mu) * lax.rsqrt(var + EPS) * g + b


def _dwconv(x, w, b):
    width = w.shape[0]
    pad = width // 2
    y = lax.conv_general_dilated(
        x, w[:, None, :], window_strides=(1,), padding=[(pad, pad)],
        dimension_numbers=('NWC', 'WIO', 'NWC'), feature_group_count=x.shape[-1])
    return y + b


def _conv_mid(p, dw_w, dw_b, ln_g, ln_b, rows):
    a, g, z = jnp.split(p, 3, axis=-1)
    y = a * jax.nn.sigmoid(g)
    if rows is None:
        y = _dwconv(y, dw_w, dw_b)
    else:
        bsz, n, e = y.shape
        y = _dwconv(y.reshape(bsz * rows, GRID_W, e), dw_w, dw_b).reshape(bsz, n, e)
    y = jax.nn.silu(_layernorm(y, ln_g, ln_b))
    return y * jax.nn.silu(z)


def _headwise(x, w):
    g, bi, bo = w.shape
    xs = x.reshape(x.shape[:-1] + (g, bi))
    return jnp.einsum('btgi,gio->btgo', xs, w).reshape(x.shape[:-1] + (g * bo,))


def _reverse_segments(a, n_first):
    return jnp.concatenate([jnp.flip(a[:, :, :n_first], 2), jnp.flip(a[:, :, n_first:], 2)], axis=2)


def _mlstm_scan(q, k, v, li, lf):
    bsz, nh, t, dk = q.shape
    dv = v.shape[-1]
    L = MLSTM_CHUNK
    nc = t // L
    q = q * (dk ** -0.5)
    tril = jnp.tril(jnp.ones((L, L), dtype=bool))

    def chunks(a):
        return jnp.moveaxis(a.reshape((bsz, nh, nc, L) + a.shape[3:]), 2, 0)

    def step(carry, inp):
        C, n, m = carry
        qc, kc, vc, lic, lfc = inp
        b = jnp.cumsum(lfc, axis=-1)
        log_d = b[..., :, None] - b[..., None, :] + lic[..., None, :]
        log_d = jnp.where(tril, log_d, -jnp.inf)
        log_inter = b + m[..., None]
        m_t = jnp.maximum(log_inter, jnp.max(log_d, axis=-1))
        d = jnp.exp(log_d - m_t[..., None])
        a = jnp.exp(log_inter - m_t)
        s = jnp.einsum('bhtd,bhsd->bhts', qc, kc) * d
        num = a[..., None] * jnp.einsum('bhtd,bhde->bhte', qc, C) + jnp.einsum('bhts,bhse->bhte', s, vc)
        den = a * jnp.einsum('bhtd,bhd->bht', qc, n) + jnp.sum(s, axis=-1)
        h = num / jnp.maximum(jnp.abs(den), jnp.exp(-m_t))[..., None]
        b_last = b[..., -1]
        log_w = b_last[..., None] - b + lic
        m_new = jnp.maximum(b_last + m, jnp.max(log_w, axis=-1))
        w = jnp.exp(log_w - m_new[..., None])
        decay = jnp.exp(b_last + m - m_new)
        kw = kc * w[..., None]
        C_new = decay[..., None, None] * C + jnp.einsum('bhsd,bhse->bhde', kw, vc)
        n_new = decay[..., None] * n + jnp.sum(kw, axis=2)
        return (C_new, n_new, m_new), h

    init = (jnp.zeros((bsz, nh, dk, dv), F32), jnp.zeros((bsz, nh, dk), F32),
            jnp.full((bsz, nh), -1e30, F32))
    _, h = lax.scan(step, init, (chunks(q), chunks(k), chunks(v), chunks(li), chunks(lf)))
    return jnp.moveaxis(h, 0, 2).reshape(bsz, nh, t, dv)


def _head_layernorm(h, g):
    mu = jnp.mean(h, axis=-1, keepdims=True)
    var = jnp.mean(jnp.square(h - mu), axis=-1, keepdims=True)
    return (h - mu) * lax.rsqrt(var + EPS) * g.reshape(h.shape[1], 1, h.shape[3])


def _mlstm_mid(p_ctx, p_lat, conv_w, conv_b, w_q, w_k, w_v, w_o, b_o,
               w_gates, b_gates, mh_g, skip):
    n_ctx = p_ctx.shape[1]

    def streams(p):
        xm, z = jnp.split(p, 2, axis=-1)
        xc = jax.nn.silu(_dwconv(xm, conv_w, conv_b))
        return xm, xc, z

    xm_c, xc_c, z_c = streams(p_ctx)
    xm_l, xc_l, z_l = streams(p_lat)
    xm = jnp.concatenate([xm_c, xm_l], axis=1)
    xc = jnp.concatenate([xc_c, xc_l], axis=1)
    z = jnp.concatenate([z_c, z_l], axis=1)
    q = _headwise(xc, w_q)
    k = _headwise(xc, w_k)
    v = _headwise(xm, w_v)
    o = jax.nn.sigmoid(_headwise(xc, w_o) + b_o)
    bsz, t, e = q.shape
    wg = w_gates.reshape(3, e, -1)
    gates = q @ wg[0] + k @ wg[1] + v @ wg[2] + b_gates
    gates = gates.reshape(bsz, t, 2, 2, MLSTM_HEADS).transpose(2, 3, 0, 4, 1)

    def heads(a):
        return a.reshape(bsz, t, MLSTM_HEADS, -1).transpose(0, 2, 1, 3)

    qh, kh, vh = heads(q), heads(k), heads(v)

    def rev(a):
        return _reverse_segments(a, n_ctx)

    h_fwd = _mlstm_scan(qh, kh, vh, gates[0, 0], jax.nn.log_sigmoid(gates[0, 1]))
    h_bwd = rev(_mlstm_scan(rev(qh), rev(kh), rev(vh), rev(gates[1, 0]),
                            rev(jax.nn.log_sigmoid(gates[1, 1]))))
    h = _head_layernorm(h_fwd + h_bwd, mh_g).transpose(0, 2, 1, 3).reshape(bsz, t, e)
    y = (o * h + skip * xc) * jax.nn.silu(z)
    return y[:, n_ctx:]


def _hyena_filters(L, f_w1, f_b1, f_freq1, f_w2, f_b2, f_freq2, f_w3, f_b3):
    t = jnp.linspace(0.0, 1.0, L, dtype=F32)[:, None]
    bands = (HYENA_EMB_DIM - 1) // 2
    ang = 2.0 * math.pi * jnp.arange(L, dtype=F32)[:, None] / L
    fr = jnp.linspace(1e-4, bands - 1, bands, dtype=F32)[None, :]
    feat = jnp.concatenate([t, jnp.cos(fr * ang), -jnp.sin(fr * ang)], axis=-1)
    h = jnp.sin(f_freq1 * (feat @ f_w1 + f_b1))
    h = jnp.sin(f_freq2 * (h @ f_w2 + f_b2))
    h = h @ f_w3 + f_b3
    e = h.shape[-1] // 2
    lo = math.log(HYENA_DECAY_TARGET) / HYENA_FAST_DECAY
    hi = math.log(HYENA_DECAY_TARGET) / HYENA_SLOW_DECAY
    deltas = jnp.abs(jnp.linspace(lo, hi, e, dtype=F32))
    h = h * jnp.exp(-t * jnp.tile(deltas, 2)[None, :])
    return h[:, :e], h[:, e:]


def _bidir_long_conv(u, hf, hb, bias):
    n = u.shape[1]
    k = jnp.concatenate([hf, jnp.zeros_like(hf[:1]), jnp.flip(hb[1:], axis=0)], axis=0)
    spec = jnp.fft.rfft(u, n=2 * n, axis=1) * jnp.fft.rfft(k, axis=0)[None]
    y = jnp.fft.irfft(spec, n=2 * n, axis=1)[:, :n]
    return y + u * bias


def _hyena_mid(p, conv_w, conv_b, f_w1, f_b1, f_freq1, f_w2, f_b2, f_freq2,
               f_w3, f_b3, h_bias):
    e = h_bias.shape[0]
    g = _dwconv(p[..., :3 * e], conv_w, conv_b)
    x0, x1, v = jnp.split(g, 3, axis=-1)
    z = p[..., 3 * e:]
    hf, hb = _hyena_filters(p.shape[1], f_w1, f_b1, f_freq1, f_w2, f_b2, f_freq2, f_w3, f_b3)
    y = x0 * _bidir_long_conv(x1 * v, hf, hb, h_bias)
    return y * jax.nn.silu(z)


def _final_norm_kernel(h_ref, g_ref, o_ref):
    x = h_ref[...]
    ms = jnp.mean(x * x, axis=-1, keepdims=True)
    o_ref[...] = x * lax.rsqrt(ms + EPS) * g_ref[...]


def _final_norm(h, g, *, tm=1024):
    bn, t, d = h.shape
    m = bn * t
    tm = min(tm, m)
    out = pl.pallas_call(
        _final_norm_kernel,
        grid=(m // tm,),
        in_specs=[pl.BlockSpec((tm, d), lambda i: (i, 0)),
                  pl.BlockSpec((1, d), lambda i: (0, 0))],
        out_specs=pl.BlockSpec((tm, d), lambda i: (i, 0)),
        out_shape=jax.ShapeDtypeStruct((m, d), F32),
        compiler_params=pltpu.CompilerParams(
            dimension_semantics=("arbitrary",),
            vmem_limit_bytes=V7X_VMEM_LIMIT_BYTES),
        name="final_norm",
    )(h.reshape(m, d), g.reshape(1, d))
    return out.reshape(bn, t, d)


def kernel(x, c, ctx, c_ctx, norm_g, ada_w, ada_b, final_g, cv_w_in, cv_dw_w, cv_dw_b, cv_ln_g, cv_ln_b, cv_w_out, ml_w_in, ml_conv_w, ml_conv_b, ml_w_q, ml_w_k, ml_w_v, ml_w_o, ml_b_o, ml_w_gates, ml_b_gates, ml_mh_g, ml_skip, ml_w_out, hy_w_in, hy_conv_w, hy_conv_b, hy_f_w1, hy_f_b1, hy_f_freq1, hy_f_w2, hy_f_b2, hy_f_freq2, hy_f_w3, hy_f_b3, hy_h_bias, hy_w_out):
    depth = norm_g.shape[0]
    bsz, seq, d = x.shape
    rows = seq // GRID_W
    readers = [i for i in range(depth) if i % N_MIXERS == 1]
    last_reader = readers[-1] if readers else -1
    silu_c = jax.nn.silu(c)
    silu_cc = jax.nn.silu(c_ctx)
    h_lat, h_ctx = x, ctx
    for i in range(depth):
        kind, j = i % N_MIXERS, i // N_MIXERS
        ctx_in = i <= last_reader
        ctx_out = i < last_reader
        sh, sc, gt = jnp.split(silu_c @ ada_w[i] + ada_b[i], 3, axis=-1)
        if ctx_in:
            sh_c, sc_c, gt_c = jnp.split(silu_cc @ ada_w[i] + ada_b[i], 3, axis=-1)
            bc = lambda a: jnp.broadcast_to(a[None], (bsz, d))
            sh_c, sc_c, gt_c = bc(sh_c), bc(sc_c), bc(gt_c)
        if kind == 0:
            w_in = cv_w_in[j].astype(BF16)
            w_out = cv_w_out[j].astype(BF16)
            p = _norm_matmul(h_lat, norm_g[i], sh, sc, w_in)
            y_lat = _conv_mid(p, cv_dw_w[j], cv_dw_b[j], cv_ln_g[j], cv_ln_b[j], rows)
            if ctx_out:
                p_c = _norm_matmul(h_ctx, norm_g[i], sh_c, sc_c, w_in)
                y_ctx = _conv_mid(p_c, cv_dw_w[j], cv_dw_b[j], cv_ln_g[j], cv_ln_b[j], None)
        elif kind == 1:
            w_in = ml_w_in[j].astype(BF16)
            w_out = ml_w_out[j].astype(BF16)
            p = _norm_matmul(h_lat, norm_g[i], sh, sc, w_in)
            p_c = _norm_matmul(h_ctx, norm_g[i], sh_c, sc_c, w_in)
            assert not ctx_out
            y_lat = _mlstm_mid(p_c, p, ml_conv_w[j], ml_conv_b[j], ml_w_q[j], ml_w_k[j],
                               ml_w_v[j], ml_w_o[j], ml_b_o[j], ml_w_gates[j],
                               ml_b_gates[j], ml_mh_g[j], ml_skip[j])
        else:
            w_in = hy_w_in[j].astype(BF16)
            w_out = hy_w_out[j].astype(BF16)
            p = _norm_matmul(h_lat, norm_g[i], sh, sc, w_in)
            assert not ctx_out
            y_lat = _hyena_mid(p, hy_conv_w[j], hy_conv_b[j], hy_f_w1[j], hy_f_b1[j],
                               hy_f_freq1[j], hy_f_w2[j], hy_f_b2[j], hy_f_freq2[j],
                               hy_f_w3[j], hy_f_b3[j], hy_h_bias[j])
        h_lat = _out_matmul(y_lat.astype(BF16), w_out, h_lat, gt)
        if ctx_out:
            h_ctx = _out_matmul(y_ctx.astype(BF16), w_out, h_ctx, gt_c)
    return _final_norm(h_lat, final_g)
```

```python
import functools
import math

import jax
import jax.numpy as jnp
import numpy as np
from jax import lax
from jax.experimental import pallas as pl
from jax.experimental.pallas import tpu as pltpu

GRID_W = 64
N_MIXERS = 3
EPS = 1e-6
MLSTM_HEADS = 8
HYENA_EMB_DIM = 33
HYENA_FAST_DECAY = 0.3
HYENA_SLOW_DECAY = 1.5
HYENA_DECAY_TARGET = 1e-2

V7X_VMEM_LIMIT_BYTES = 56 * 1024 * 1024
SUBLANES = 8
BF16_ROWS = 16
LANES = 128
MXU_TILE = 256

F32 = jnp.float32
BF16 = jnp.bfloat16


def _tiles(m, n, tm, tn):
    tm, tn = min(tm, m), min(tn, n)
    assert m % tm == 0 and n % tn == 0, (m, n, tm, tn)
    return tm, tn


def _sigmoid(x):
    return 1.0 / (1.0 + jnp.exp(-x))


def _silu(x):
    return x * _sigmoid(x)


def _dot_bf16(a, b):
    return jnp.dot(a.astype(BF16), b.astype(BF16), preferred_element_type=F32)


def _params(semantics):
    return pltpu.CompilerParams(dimension_semantics=semantics,
                                vmem_limit_bytes=V7X_VMEM_LIMIT_BYTES)


def _resident(shape, n_grid):
    zeros = (0,) * len(shape)
    return pl.BlockSpec(shape, lambda *_: zeros, pipeline_mode=pl.Buffered(1))


def _ada_kernel(c_ref, w_ref, b_ref, o_ref):
    o_ref[0] = _dot_bf16(_silu(c_ref[...]), w_ref[0]) + b_ref[0]


def _ada_params(cond, ada_w, ada_b, *, tn=512):
    depth, d, n = ada_w.shape
    r = cond.shape[0]
    return pl.pallas_call(
        _ada_kernel,
        grid=(depth, n // tn),
        in_specs=[pl.BlockSpec((r, d), lambda i, j: (0, 0)),
                  pl.BlockSpec((1, d, tn), lambda i, j: (i, 0, j)),
                  pl.BlockSpec((1, 1, tn), lambda i, j: (i, 0, j))],
        out_specs=pl.BlockSpec((1, r, tn), lambda i, j: (i, 0, j)),
        out_shape=jax.ShapeDtypeStruct((depth, r, n), F32),
        compiler_params=_params(("arbitrary", "arbitrary")),
        name="ada_params",
    )(cond, ada_w, ada_b.reshape(depth, 1, n))


def _epilogue_conv(a, g, z):
    return a * _sigmoid(g), _silu(z)


def _epilogue_gate_last(*accs):
    return accs[:-1] + (_silu(accs[-1]),)


def _in_proj_kernel(*refs, n_groups, epilogue):
    h_ref, g_ref, sh_ref, sc_ref = refs[:4]
    w_refs = refs[4:4 + n_groups]
    o_refs = refs[4 + n_groups:-1]
    u_ref = refs[-1]

    @pl.when(pl.program_id(1) == 0)
    def _():
        x = h_ref[...]
        ms = jnp.mean(x * x, axis=-1, keepdims=True)
        y = x * lax.rsqrt(ms + EPS) * g_ref[...]
        u_ref[...] = (y * (1.0 + sc_ref[0]) + sh_ref[0]).astype(u_ref.dtype)

    u = u_ref[...]
    outs = epilogue(*[jnp.dot(u, w[...], preferred_element_type=F32) for w in w_refs])
    for o_ref, val in zip(o_refs, outs, strict=True):
        o_ref[...] = val.astype(o_ref.dtype)


def _in_proj(h, g, sh, sc, w, *, n_groups, n_out, epilogue, n_ctx=0, tm=512, tn=512):
    bn, t, d = h.shape
    e = w.shape[1] // n_groups
    tm, tn = _tiles(t, e, tm, tn)
    assert n_ctx % tm == 0
    tiles_per_seq, ctx_tiles = t // tm, n_ctx // tm
    col_tiles = e // tn
    n_mod = sh.shape[0]

    def mod_row(i, j):
        return (jnp.where(i % tiles_per_seq < ctx_tiles, n_mod - 1, i // tiles_per_seq), 0, 0)

    w_specs = [pl.BlockSpec((d, tn), functools.partial(lambda i, j, k: (0, k * col_tiles + j), k=k))
               for k in range(n_groups)]
    outs = pl.pallas_call(
        functools.partial(_in_proj_kernel, n_groups=n_groups, epilogue=epilogue),
        grid=(bn * tiles_per_seq, col_tiles),
        in_specs=[
            pl.BlockSpec((tm, d), lambda i, j: (i, 0)),
            pl.BlockSpec((1, d), lambda i, j: (0, 0)),
            pl.BlockSpec((1, 1, d), mod_row),
            pl.BlockSpec((1, 1, d), mod_row),
        ] + w_specs,
        out_specs=[pl.BlockSpec((tm, tn), lambda i, j: (i, j))] * n_out,
        out_shape=[jax.ShapeDtypeStruct((bn * t, e), BF16)] * n_out,
        scratch_shapes=[pltpu.VMEM((tm, d), BF16)],
        compiler_params=_params(("arbitrary", "arbitrary")),
        name="in_proj",
    )(h.reshape(bn * t, d), g.reshape(1, d), sh.reshape(n_mod, 1, d),
      sc.reshape(n_mod, 1, d), *([w] * n_groups))
    return [o.reshape(bn, t, e) for o in outs]


def _out_matmul_kernel(y_ref, w_ref, h_ref, gt_ref, o_ref):
    acc = jnp.dot(y_ref[...], w_ref[...], preferred_element_type=F32)
    o_ref[...] = h_ref[...] + gt_ref[0] * acc


def _out_matmul(y, w, h, gt, *, tm=512, tn=1024):
    bn, t, e = y.shape
    d = w.shape[1]
    tm, tn = _tiles(t, d, tm, tn)
    tiles_per_seq = t // tm
    out = pl.pallas_call(
        _out_matmul_kernel,
        grid=(bn * tiles_per_seq, d // tn),
        in_specs=[
            pl.BlockSpec((tm, e), lambda i, j: (i, 0)),
            pl.BlockSpec((e, tn), lambda i, j: (0, j)),
            pl.BlockSpec((tm, tn), lambda i, j: (i, j)),
            pl.BlockSpec((1, 1, tn), lambda i, j: (i // tiles_per_seq, 0, j)),
        ],
        out_specs=pl.BlockSpec((tm, tn), lambda i, j: (i, j)),
        out_shape=jax.ShapeDtypeStruct((bn * t, d), F32),
        compiler_params=_params(("arbitrary", "arbitrary")),
        name="out_matmul",
    )(y.reshape(bn * t, e), w, h.reshape(bn * t, d), gt.reshape(bn, 1, d))
    return out.reshape(bn, t, d)


CONV_PAD = 16
LN_UNROLL = 8


def _conv_mid_kernel(y_ref, zs_ref, w_ref, b_ref, lg_ref, lb_ref, o_ref, xp_ref, xs_ref, cv_ref, *, seg, width):
    tb, e = y_ref.shape
    n_seg = tb // seg
    half = width // 2
    rows_s = xs_ref.shape[2]
    rb = SUBLANES
    zeros = jnp.zeros((CONV_PAD, e), F32)
    for s in range(n_seg):
        xp_ref[s, 0:CONV_PAD, :] = zeros
        xp_ref[s, CONV_PAD:CONV_PAD + seg, :] = y_ref[s * seg:(s + 1) * seg, :].astype(F32)
        xp_ref[s, CONV_PAD + seg:2 * CONV_PAD + seg, :] = zeros

    def lane_block(cb, carry):
        lanes = pl.ds(pl.multiple_of(cb * LANES, LANES), LANES)
        for sh in range(rb):
            for s in range(n_seg):
                xs_ref[sh, s] = xp_ref[s, sh:sh + rows_s, lanes]
        taps = [w_ref[k:k + 1, lanes] for k in range(width)]
        bias = b_ref[:, lanes]
        for s in range(n_seg):
            for r in range(seg // rb):
                acc = bias
                for k in range(width):
                    off = CONV_PAD - half + k
                    base = (off // rb) * rb + r * rb
                    acc = acc + taps[k] * xs_ref[off % rb, s, base:base + rb, :]
                cv_ref[s * seg + r * rb:s * seg + (r + 1) * rb, lanes] = acc
        return carry

    lax.fori_loop(0, e // LANES, lane_block, 0)

    def row_block(i, carry):
        rows = pl.ds(pl.multiple_of(i * rb, rb), rb)
        cv = cv_ref[rows, :]
        mu = jnp.mean(cv, axis=-1, keepdims=True)
        xc = cv - mu
        var = jnp.mean(xc * xc, axis=-1, keepdims=True)
        yn = xc * lax.rsqrt(var + EPS) * lg_ref[...] + lb_ref[...]
        o_ref[rows, :] = (_silu(yn) * zs_ref[rows, :].astype(F32)).astype(o_ref.dtype)
        return carry

    lax.fori_loop(0, tb // rb, row_block, 0, unroll=LN_UNROLL)


def _conv_mid(y, zs, dw_w, dw_b, ln_g, ln_b, *, seg, tb=256):
    bn, t, e = y.shape
    width = dw_w.shape[0]
    m = bn * t
    tb = min(tb, m)
    assert m % tb == 0 and tb % seg == 0 and t % seg == 0 and width // 2 <= CONV_PAD
    row = lambda a: a.reshape(1, e)
    tok = pl.BlockSpec((tb, e), lambda i: (i, 0))
    vec = pl.BlockSpec((1, e), lambda i: (0, 0))
    out = pl.pallas_call(
        functools.partial(_conv_mid_kernel, seg=seg, width=width),
        grid=(m // tb,),
        in_specs=[tok, tok, pl.BlockSpec((width, e), lambda i: (0, 0)), vec, vec, vec],
        out_specs=tok,
        out_shape=jax.ShapeDtypeStruct((m, e), BF16),
        scratch_shapes=[pltpu.VMEM((tb // seg, seg + 2 * CONV_PAD, e), F32),
                        pltpu.VMEM((SUBLANES, tb // seg, seg + 2 * CONV_PAD - SUBLANES, LANES), F32),
                        pltpu.VMEM((tb, e), F32)],
        compiler_params=_params(("arbitrary",)),
        name="conv_mid",
    )(y.reshape(m, e), zs.reshape(m, e), dw_w, row(dw_b), row(ln_g), row(ln_b))
    return out.reshape(bn, t, e)


STAGE_PAD = 8


def _expand_block_diag(w):
    g, bi, bo = w.shape
    per = MXU_TILE // bi
    wt = w.reshape(g // per, per, bi, bo)
    eye = jnp.eye(per, dtype=w.dtype)
    return jnp.einsum('tgio,gh->tgiho', wt, eye).reshape(g // per, per * bi, per * bo)


def _mlstm_pre_kernel(xm_ref, prev_ref, next_ref, cw_ref, cb_ref, wq_ref, wk_ref, wkt_ref, wv_ref, wo_ref,
                      bo_ref, wg_ref, wgt_ref, bg_ref, bgt_ref,
                      q_ref, k_ref, kt_ref, v_ref, o_ref, xc_ref, gc_ref, gr_ref, stage_ref,
                      *, tiles_per_seq, ctx_tiles, q_scale):
    tb, e = xm_ref.shape
    t = pl.program_id(0) % tiles_per_seq
    has_prev = jnp.logical_and(t != 0, t != ctx_tiles).astype(F32)
    has_next = jnp.logical_and(t != ctx_tiles - 1, t != tiles_per_seq - 1).astype(F32)
    gc = jnp.zeros(gc_ref.shape, F32) + bg_ref[...]
    gr = jnp.zeros(gr_ref.shape, F32) + bgt_ref[...]
    nt = (((1,), (1,)), ((), ()))
    for j in range(e // MXU_TILE):
        lanes = slice(j * MXU_TILE, (j + 1) * MXU_TILE)
        xm = xm_ref[:, lanes]
        stage_ref[STAGE_PAD - 1:STAGE_PAD, :] = \
            prev_ref[:, lanes].astype(F32)[BF16_ROWS - 1:BF16_ROWS, :] * has_prev
        stage_ref[STAGE_PAD:STAGE_PAD + tb, :] = xm.astype(F32)
        stage_ref[STAGE_PAD + tb:STAGE_PAD + tb + 1, :] = next_ref[:, lanes].astype(F32)[0:1, :] * has_next
        cw = cw_ref[:, lanes]
        pre = (cb_ref[:, lanes] + cw[0:1] * stage_ref[STAGE_PAD - 1:STAGE_PAD - 1 + tb, :]
               + cw[1:2] * stage_ref[STAGE_PAD:STAGE_PAD + tb, :]
               + cw[2:3] * stage_ref[STAGE_PAD + 1:STAGE_PAD + 1 + tb, :])
        xcb = _silu(pre).astype(BF16)
        xc_ref[:, lanes] = xcb
        q = jnp.dot(xcb, wq_ref[j], preferred_element_type=F32)
        k = jnp.dot(xcb, wk_ref[j], preferred_element_type=F32)
        v = jnp.dot(xm, wv_ref[j], preferred_element_type=F32)
        o = jnp.dot(xcb, wo_ref[j], preferred_element_type=F32) + bo_ref[:, lanes]
        kt_ref[lanes, :] = lax.dot_general(wkt_ref[j], xcb, nt, preferred_element_type=F32).astype(BF16)
        qb, kb, vb = q.astype(BF16), k.astype(BF16), v.astype(BF16)
        q_ref[:, lanes] = (q * q_scale).astype(BF16)
        k_ref[:, lanes] = kb
        v_ref[:, lanes] = vb
        o_ref[:, lanes] = _sigmoid(o).astype(BF16)
        for i, a in enumerate((qb, kb, vb)):
            gc = gc + jnp.dot(a, wg_ref[i, lanes, :], preferred_element_type=F32)
            gr = gr + lax.dot_general(wgt_ref[i, :, lanes], a, nt, preferred_element_type=F32)
    gc_ref[...] = gc
    gr_ref[...] = gr


def _mlstm_pre(xm, conv_w, conv_b, w_q, w_k, w_v, w_o, b_o, w_gates, b_gates, *, n_ctx, n_heads, tb=256):
    bsz, t, e = xm.shape
    assert t % tb == 0 and n_ctx % tb == 0
    tiles_per_seq, ctx_tiles = t // tb, n_ctx // tb
    m = bsz * t
    n_g = w_gates.shape[1]
    hpt = tb // BF16_ROWS
    n_halo = m // BF16_ROWS
    bd = lambda w: _expand_block_diag(w).astype(BF16)
    wq, wk, wv, wo = bd(w_q), bd(w_k), bd(w_v), bd(w_o)
    wkt = jnp.swapaxes(wk, 1, 2)
    wg = w_gates.reshape(3, e, n_g).astype(BF16)
    wgt = jnp.swapaxes(wg, 1, 2)
    tok = pl.BlockSpec((tb, e), lambda i: (i, 0))
    row = lambda a: a.reshape(1, -1)
    consts = [conv_w, row(conv_b), wq, wk, wkt, wv, wo, row(b_o), wg, wgt, row(b_gates), b_gates.reshape(-1, 1)]
    sds = jax.ShapeDtypeStruct
    xm2 = xm.reshape(m, e)
    outs = pl.pallas_call(
        functools.partial(_mlstm_pre_kernel, tiles_per_seq=tiles_per_seq, ctx_tiles=ctx_tiles,
                          q_scale=(e // n_heads) ** -0.5),
        grid=(m // tb,),
        in_specs=[tok,
                  pl.BlockSpec((BF16_ROWS, e), lambda i: (jnp.maximum(i * hpt - 1, 0), 0)),
                  pl.BlockSpec((BF16_ROWS, e), lambda i: (jnp.minimum((i + 1) * hpt, n_halo - 1), 0))]
                 + [_resident(a.shape, 1) for a in consts],
        out_specs=[tok, tok,
                   pl.BlockSpec((None, e, tb), lambda i: (i // tiles_per_seq, 0, i % tiles_per_seq)),
                   tok, tok, tok,
                   pl.BlockSpec((tb, n_g), lambda i: (i, 0)),
                   pl.BlockSpec((None, n_g, tb), lambda i: (i // tiles_per_seq, 0, i % tiles_per_seq))],
        out_shape=[sds((m, e), BF16), sds((m, e), BF16), sds((bsz, e, t), BF16), sds((m, e), BF16),
                   sds((m, e), BF16), sds((m, e), BF16), sds((m, n_g), F32), sds((bsz, n_g, t), F32)],
        scratch_shapes=[pltpu.VMEM((tb + 2 * STAGE_PAD, MXU_TILE), F32)],
        compiler_params=_params(("arbitrary",)),
        name="mlstm_pre",
    )(xm2, xm2, xm2, *consts)
    q, k, kt, v, o, xc, gc, gr = outs
    r3 = lambda a: a.reshape(bsz, t, -1)
    return r3(q), r3(k), kt, r3(v), r3(o), r3(xc), r3(gc), gr


SCAN_CHUNK = 256


def _log_sigmoid(x):
    return jnp.minimum(x, 0.0) - jnp.log1p(jnp.exp(-jnp.abs(x)))


def _mlstm_scan_kernel(q_ref, k_ref, kt_ref, v_ref, gc_ref, gr_ref, o_ref,
                       c_ref, cb_ref, n_ref, m_ref, *, n_heads):
    direction = pl.program_id(0)
    head = pl.program_id(2)

    @pl.when(pl.program_id(3) == 0)
    def _():
        c_ref[...] = jnp.zeros_like(c_ref)
        cb_ref[...] = jnp.zeros_like(cb_ref)
        n_ref[...] = jnp.zeros_like(n_ref)
        m_ref[...] = jnp.full_like(m_ref, -1e30)

    L = q_ref.shape[1]
    n_g = gc_ref.shape[2]
    gc = gc_ref[0]
    gr = gr_ref[0]
    col_i = direction * (2 * n_heads) + head
    col_f = col_i + n_heads
    lane = lax.broadcasted_iota(jnp.int32, (L, n_g), 1)
    sub = lax.broadcasted_iota(jnp.int32, (n_g, L), 0)
    li_col = jnp.sum(jnp.where(lane == col_i, gc, 0.0), axis=1, keepdims=True)
    lf_col = _log_sigmoid(jnp.sum(jnp.where(lane == col_f, gc, 0.0), axis=1, keepdims=True))
    li_row = jnp.sum(jnp.where(sub == col_i, gr, 0.0), axis=0, keepdims=True)
    lf_row = _log_sigmoid(jnp.sum(jnp.where(sub == col_f, gr, 0.0), axis=0, keepdims=True))

    sign = 1 - 2 * direction
    diff = (lax.broadcasted_iota(jnp.int32, (L, L), 0)
            - lax.broadcasted_iota(jnp.int32, (L, L), 1)) * sign
    causal = diff >= 0
    b_col = jnp.sum(jnp.where(causal, lf_row, 0.0), axis=1, keepdims=True)
    b_row = jnp.sum(jnp.where(diff <= 0, lf_col, 0.0), axis=0, keepdims=True)
    total = jnp.sum(lf_row, axis=1, keepdims=True)

    m = m_ref[...]
    log_d = jnp.where(causal, b_col - b_row + li_row, -jnp.inf)
    log_inter = b_col + m
    m_t = jnp.maximum(log_inter, jnp.max(log_d, axis=1, keepdims=True))
    dmat = jnp.exp(log_d - m_t)
    a = jnp.exp(log_inter - m_t)

    q = q_ref[0]
    kt = kt_ref[0]
    v = v_ref[0]
    s = jnp.dot(q, kt, preferred_element_type=F32) * dmat
    num = a * jnp.dot(q, cb_ref[...], preferred_element_type=F32) \
        + jnp.dot(s.astype(BF16), v, preferred_element_type=F32)
    qn = jnp.sum(q.astype(F32) * n_ref[...], axis=1, keepdims=True)
    den = a * qn + jnp.sum(s, axis=1, keepdims=True)
    o_ref[0, 0] = (num / jnp.maximum(jnp.abs(den), jnp.exp(-m_t))).astype(o_ref.dtype)

    log_w_row = total - b_row + li_row
    log_w_col = total - b_col + li_col
    m_new = jnp.maximum(total + m, jnp.max(log_w_row, axis=1, keepdims=True))
    decay = jnp.exp(total + m - m_new)
    kwt = (kt.astype(F32) * jnp.exp(log_w_row - m_new)).astype(BF16)
    c_new = decay * c_ref[...] + jnp.dot(kwt, v, preferred_element_type=F32)
    c_ref[...] = c_new
    cb_ref[...] = c_new.astype(BF16)
    kw = k_ref[0].astype(F32) * jnp.exp(log_w_col - m_new)
    n_ref[...] = decay * n_ref[...] + jnp.sum(kw, axis=0, keepdims=True)
    m_ref[...] = m_new


def _mlstm_scan(q, k, kt, v, gc, gr, *, n_heads, n_ctx, chunk=SCAN_CHUNK):
    bsz, t, e = q.shape
    dh = e // n_heads
    L = chunk
    assert t % L == 0 and n_ctx % L == 0
    nc, nc_ctx = t // L, n_ctx // L

    def cidx(d, c):
        rev = jnp.where(c < nc_ctx, nc_ctx - 1 - c, nc - 1 - c + nc_ctx)
        return jnp.where(d == 0, c, rev)

    tok = pl.BlockSpec((1, L, dh), lambda d, b, h, c: (b, cidx(d, c), h))
    return pl.pallas_call(
        functools.partial(_mlstm_scan_kernel, n_heads=n_heads),
        grid=(2, bsz, n_heads, nc),
        in_specs=[
            tok, tok,
            pl.BlockSpec((1, dh, L), lambda d, b, h, c: (b, h, cidx(d, c))),
            tok,
            pl.BlockSpec((1, L, 4 * n_heads), lambda d, b, h, c: (b, cidx(d, c), 0)),
            pl.BlockSpec((1, 4 * n_heads, L), lambda d, b, h, c: (b, 0, cidx(d, c))),
        ],
        out_specs=pl.BlockSpec((1, 1, L, dh), lambda d, b, h, c: (d, b, cidx(d, c), h)),
        out_shape=jax.ShapeDtypeStruct((2, bsz, t, e), BF16),
        scratch_shapes=[pltpu.VMEM((dh, dh), F32), pltpu.VMEM((dh, dh), BF16),
                        pltpu.VMEM((1, dh), F32), pltpu.VMEM((1, 1), F32)],
        compiler_params=_params(("arbitrary",) * 4),
        name="mlstm_scan",
    )(q, k, kt, v, gc, gr)


def _mlstm_out_kernel(hf_ref, hb_ref, o_ref, xc_ref, zs_ref, mh_ref, skip_ref, w_ref, res_ref, gt_ref,
                      out_ref, y_ref, *, n_heads):
    tm, e = o_ref.shape
    dh = e // n_heads
    for hd in range(n_heads):
        lanes = slice(hd * dh, (hd + 1) * dh)
        h = hf_ref[:, lanes].astype(F32) + hb_ref[:, lanes].astype(F32)
        mu = jnp.mean(h, axis=-1, keepdims=True)
        hc = h - mu
        var = jnp.mean(hc * hc, axis=-1, keepdims=True)
        hn = hc * lax.rsqrt(var + EPS) * mh_ref[:, lanes]
        y = (o_ref[:, lanes].astype(F32) * hn + skip_ref[:, lanes] * xc_ref[:, lanes].astype(F32)) \
            * zs_ref[:, lanes].astype(F32)
        y_ref[:, lanes] = y.astype(BF16)
    acc = jnp.dot(y_ref[...], w_ref[...], preferred_element_type=F32)
    out_ref[...] = res_ref[...] + gt_ref[0] * acc


def _mlstm_out(hs, o, xc, zs, mh_g, skip, w_out, h_lat, gt, *, n_ctx, n_heads, tm=256):
    _, bsz, t, e = hs.shape
    n = t - n_ctx
    d = w_out.shape[1]
    assert n % tm == 0 and n_ctx % tm == 0
    tiles, off = n // tm, n_ctx // tm
    tok = pl.BlockSpec((None, tm, e), lambda i: (i // tiles, off + i % tiles, 0))
    row = lambda a: a.reshape(1, -1)
    out = pl.pallas_call(
        functools.partial(_mlstm_out_kernel, n_heads=n_heads),
        grid=(bsz * tiles,),
        in_specs=[pl.BlockSpec((None, None, tm, e), lambda i: (0, i // tiles, off + i % tiles, 0)),
                  pl.BlockSpec((None, None, tm, e), lambda i: (1, i // tiles, off + i % tiles, 0)),
                  tok, tok, tok, _resident((1, e), 1), _resident((1, e), 1), _resident((e, d), 1),
                  pl.BlockSpec((tm, d), lambda i: (i, 0)),
                  pl.BlockSpec((1, 1, d), lambda i: (i // tiles, 0, 0))],
        out_specs=pl.BlockSpec((tm, d), lambda i: (i, 0)),
        out_shape=jax.ShapeDtypeStruct((bsz * n, d), F32),
        scratch_shapes=[pltpu.VMEM((tm, e), BF16)],
        compiler_params=_params(("arbitrary",)),
        name="mlstm_out",
    )(hs, hs, o, xc, zs, row(mh_g), row(skip), w_out, h_lat.reshape(bsz * n, d), gt.reshape(bsz, 1, d))
    return out.reshape(bsz, n, d)


def _filter_features(seq):
    r = np.arange(2 * seq)
    lag = np.where(r < seq, r, 2 * seq - r).clip(0, seq - 1).astype(np.float64)
    t = lag / (seq - 1)
    bands = (HYENA_EMB_DIM - 1) // 2
    ang = 2.0 * math.pi * lag / seq
    fr = np.linspace(1e-4, bands - 1, bands)
    feat = np.concatenate([t[:, None], np.cos(fr[None] * ang[:, None]), -np.sin(fr[None] * ang[:, None])], axis=-1)
    return jnp.asarray(feat, F32), jnp.asarray(t[:, None], F32)


def _filter_kernel(feat_ref, t_ref, w1_ref, b1_ref, f1_ref, w2_ref, b2_ref, f2_ref, w3_ref, b3_ref, dl_ref,
                   o_ref, *, seq):
    tr = o_ref.shape[0]
    h = jnp.sin(f1_ref[...] * (_dot_bf16(feat_ref[...], w1_ref[...]) + b1_ref[...]))
    h = jnp.sin(f2_ref[...] * (_dot_bf16(h, w2_ref[...]) + b2_ref[...]))
    h = _dot_bf16(h, w3_ref[...]) + b3_ref[...]
    out = h * jnp.exp(-t_ref[...] * dl_ref[...])
    row = pl.program_id(0) * tr + lax.broadcasted_iota(jnp.int32, (tr, 1), 0)
    o_ref[...] = jnp.where(row == seq, 0.0, out)


def _hyena_filter(seq, f_w1, f_b1, f_freq1, f_w2, f_b2, f_freq2, f_w3, f_b3, *, tr=512, tn=1024):
    e = f_w3.shape[1] // 2
    feat, t = _filter_features(seq)
    lo = math.log(HYENA_DECAY_TARGET) / HYENA_FAST_DECAY
    hi = math.log(HYENA_DECAY_TARGET) / HYENA_SLOW_DECAY
    deltas = jnp.asarray(np.abs(np.linspace(lo, hi, e)), F32).reshape(1, e)
    fdim = f_w1.shape[1]
    row_tiles, col_tiles = 2 * seq // tr, e // tn
    half = lambda i: i // (row_tiles // 2)
    row = lambda a: a.reshape(1, -1)
    small = lambda shape: pl.BlockSpec(shape, lambda i, j: (0, 0))
    return pl.pallas_call(
        functools.partial(_filter_kernel, seq=seq),
        grid=(row_tiles, col_tiles),
        in_specs=[pl.BlockSpec((tr, HYENA_EMB_DIM), lambda i, j: (i, 0)),
                  pl.BlockSpec((tr, 1), lambda i, j: (i, 0)),
                  small((HYENA_EMB_DIM, fdim)), small((1, fdim)), small((1, fdim)),
                  small((fdim, fdim)), small((1, fdim)), small((1, fdim)),
                  pl.BlockSpec((fdim, tn), lambda i, j: (0, half(i) * col_tiles + j)),
                  pl.BlockSpec((1, tn), lambda i, j: (0, half(i) * col_tiles + j)),
                  pl.BlockSpec((1, tn), lambda i, j: (0, j))],
        out_specs=pl.BlockSpec((tr, tn), lambda i, j: (i, j)),
        out_shape=jax.ShapeDtypeStruct((2 * seq, e), F32),
        compiler_params=_params(("arbitrary", "arbitrary")),
        name="hyena_filter",
    )(feat, t, f_w1, row(f_b1), row(f_freq1), f_w2, row(f_b2), row(f_freq2), f_w3, row(f_b3), deltas)


FFT_N1 = 64
FFT_N2 = 128
FFT_N = FFT_N1 * FFT_N2
FFT_K1 = FFT_N1 // 2 + 1
SHORT_CHUNK = 256
K1_UNROLL = 3


def _dft_tables():
    n1h = FFT_N1 // 2
    k1 = np.arange(FFT_K1)
    eye = np.eye(SUBLANES)

    def outer_fwd(n_rows):
        th = 2 * np.pi * np.outer(k1, np.arange(n_rows)) / FFT_N1
        f = np.stack([np.cos(th), -np.sin(th)], axis=1).reshape(2 * FFT_K1, n_rows)
        return np.kron(f, eye)

    n2 = np.arange(FFT_N2)
    k2 = np.arange(FFT_N2)
    ph = 2 * np.pi * n2[None, None, :] * (k1[:, None, None] + FFT_N1 * k2[None, :, None]) / FFT_N
    gr, gi = np.cos(ph), -np.sin(ph)
    g = np.concatenate([np.concatenate([gr, -gi], axis=2), np.concatenate([gi, gr], axis=2)], axis=1)
    ginv = np.swapaxes(g, 1, 2)
    th2 = 2 * np.pi * np.outer(np.arange(n1h), k1) / FFT_N1
    c = np.where((k1 == 0) | (k1 == n1h), 1.0, 2.0)[None, :] / FFT_N
    finv = np.stack([c * np.cos(th2), -c * np.sin(th2)], axis=2).reshape(n1h, 2 * FFT_K1)
    bf = lambda a: jnp.asarray(a, F32).astype(BF16)
    return dict(s1_half=bf(outer_fwd(n1h)), s1_full=bf(outer_fwd(FFT_N1)), g=bf(g), ginv=bf(ginv),
                i2=bf(np.kron(finv, eye)))


def _outer_forward(src_ref, s1_ref, dst_ref, n1_rows):
    def body(n2h, carry):
        r0 = pl.multiple_of(n2h * SUBLANES, SUBLANES)
        tiles = [src_ref[pl.ds(n1 * FFT_N2 + r0, SUBLANES), :] for n1 in range(n1_rows)]
        rhs = jnp.concatenate(tiles, axis=0).astype(BF16)
        out = jnp.dot(s1_ref[...], rhs, preferred_element_type=F32)
        for k1 in range(FFT_K1):
            for ri in range(2):
                row = (2 * k1 + ri) * SUBLANES
                dst_ref[k1, pl.ds(ri * FFT_N2 + r0, SUBLANES), :] = out[row:row + SUBLANES, :]
        return carry
    lax.fori_loop(0, FFT_N2 // SUBLANES, body, 0)


def _filter_spectrum_kernel(k_ref, s1_ref, g_ref, kf_ref, a_ref):
    _outer_forward(k_ref, s1_ref, a_ref, FFT_N1)

    def body(k1, carry):
        kf_ref[k1] = jnp.dot(g_ref[k1], a_ref[k1].astype(BF16), preferred_element_type=F32)
        return carry
    lax.fori_loop(0, FFT_K1, body, 0)


def _filter_spectrum(kfull, tables, *, cb=MXU_TILE):
    e = kfull.shape[1]
    cb = min(cb, e)
    return pl.pallas_call(
        _filter_spectrum_kernel,
        grid=(e // cb,),
        in_specs=[pl.BlockSpec((FFT_N, cb), lambda c: (0, c)),
                  _resident(tables["s1_full"].shape, 1), _resident(tables["g"].shape, 1)],
        out_specs=pl.BlockSpec((FFT_K1, 2 * FFT_N2, cb), lambda c: (0, 0, c)),
        out_shape=jax.ShapeDtypeStruct((FFT_K1, 2 * FFT_N2, e), F32),
        scratch_shapes=[pltpu.VMEM((FFT_K1, 2 * FFT_N2, cb), F32)],
        compiler_params=_params(("arbitrary",)),
        name="filter_spectrum",
    )(kfull, tables["s1_full"], tables["g"])


def _short_conv(x_ref, cw_ref, cb_ref, group, c, stage_ref, n_rows):
    r0 = pl.multiple_of(c * SHORT_CHUNK, SHORT_CHUNK)
    prev0 = pl.multiple_of(jnp.maximum(r0 - BF16_ROWS, 0), BF16_ROWS)
    next0 = pl.multiple_of(jnp.minimum(r0 + SHORT_CHUNK, n_rows - BF16_ROWS), BF16_ROWS)
    prev = x_ref[pl.ds(prev0, BF16_ROWS), :].astype(F32)[BF16_ROWS - 1:BF16_ROWS, :] * (c > 0).astype(F32)
    nxt = x_ref[pl.ds(next0, BF16_ROWS), :].astype(F32)[0:1, :] * (r0 + SHORT_CHUNK < n_rows).astype(F32)
    stage_ref[STAGE_PAD - 1:STAGE_PAD, :] = prev
    stage_ref[STAGE_PAD:STAGE_PAD + SHORT_CHUNK, :] = x_ref[pl.ds(r0, SHORT_CHUNK), :].astype(F32)
    stage_ref[STAGE_PAD + SHORT_CHUNK:STAGE_PAD + SHORT_CHUNK + 1, :] = nxt
    w = cw_ref[group]
    return (cb_ref[group]
            + w[0:1] * stage_ref[STAGE_PAD - 1:STAGE_PAD - 1 + SHORT_CHUNK, :]
            + w[1:2] * stage_ref[STAGE_PAD:STAGE_PAD + SHORT_CHUNK, :]
            + w[2:3] * stage_ref[STAGE_PAD + 1:STAGE_PAD + 1 + SHORT_CHUNK, :])


def _long_conv_kernel(x0_ref, x1_ref, v_ref, zs_ref, cw_ref, cb_ref, hb_ref, kf_ref,
                      s1_ref, g_ref, ginv_ref, i2_ref, o_ref, w_ref, ab_ref, stage_ref):
    n_rows = x1_ref.shape[0]
    n_chunks = n_rows // SHORT_CHUNK

    def make_w(c, carry):
        rows = pl.ds(pl.multiple_of(c * SHORT_CHUNK, SHORT_CHUNK), SHORT_CHUNK)
        x1c = _short_conv(x1_ref, cw_ref, cb_ref, 1, c, stage_ref, n_rows)
        vc = _short_conv(v_ref, cw_ref, cb_ref, 2, c, stage_ref, n_rows)
        w_ref[rows, :] = x1c * vc
        return carry
    lax.fori_loop(0, n_chunks, make_w, 0)

    _outer_forward(w_ref, s1_ref, ab_ref, FFT_N1 // 2)

    def per_k1(k1, carry):
        x = jnp.dot(g_ref[k1], ab_ref[k1].astype(BF16), preferred_element_type=F32)
        kf = kf_ref[k1]
        xr, xi = x[:FFT_N2], x[FFT_N2:]
        kr, ki = kf[:FFT_N2], kf[FFT_N2:]
        y = jnp.concatenate([xr * kr - xi * ki, xr * ki + xi * kr], axis=0).astype(BF16)
        ab_ref[k1] = jnp.dot(ginv_ref[k1], y, preferred_element_type=F32)
        return carry
    lax.fori_loop(0, FFT_K1, per_k1, 0, unroll=K1_UNROLL)

    def outer_inverse(n2h, carry):
        r0 = pl.multiple_of(n2h * SUBLANES, SUBLANES)
        tiles = [ab_ref[k1, pl.ds(ri * FFT_N2 + r0, SUBLANES), :] for k1 in range(FFT_K1) for ri in range(2)]
        rhs = jnp.concatenate(tiles, axis=0).astype(BF16)
        out = jnp.dot(i2_ref[...], rhs, preferred_element_type=F32)
        for n1 in range(FFT_N1 // 2):
            rows = pl.ds(n1 * FFT_N2 + r0, SUBLANES)
            w_ref[rows, :] = out[n1 * SUBLANES:(n1 + 1) * SUBLANES, :] + hb_ref[...] * w_ref[rows, :]
        return carry
    lax.fori_loop(0, FFT_N2 // SUBLANES, outer_inverse, 0)

    def finish(c, carry):
        rows = pl.ds(pl.multiple_of(c * SHORT_CHUNK, SHORT_CHUNK), SHORT_CHUNK)
        x0c = _short_conv(x0_ref, cw_ref, cb_ref, 0, c, stage_ref, n_rows)
        o_ref[rows, :] = (x0c * w_ref[rows, :] * zs_ref[rows, :].astype(F32)).astype(o_ref.dtype)
        return carry
    lax.fori_loop(0, n_chunks, finish, 0)


def _long_conv(x0, x1, v, zs, conv_w, conv_b, h_bias, kf, tables, *, cb=MXU_TILE):
    bsz, seq, e = x0.shape
    assert seq == FFT_N // 2
    cb = min(cb, e)
    cw = conv_w.reshape(3, 3, e).transpose(1, 0, 2)
    cbias = conv_b.reshape(3, 1, e)
    tok = pl.BlockSpec((None, seq, cb), lambda c, b: (b, 0, c))
    consts = [tables["s1_half"], tables["g"], tables["ginv"], tables["i2"]]
    return pl.pallas_call(
        _long_conv_kernel,
        grid=(e // cb, bsz),
        in_specs=[tok, tok, tok, tok,
                  pl.BlockSpec((3, 3, cb), lambda c, b: (0, 0, c)),
                  pl.BlockSpec((3, 1, cb), lambda c, b: (0, 0, c)),
                  pl.BlockSpec((1, cb), lambda c, b: (0, c)),
                  pl.BlockSpec((FFT_K1, 2 * FFT_N2, cb), lambda c, b: (0, 0, c), pipeline_mode=pl.Buffered(1))]
                 + [_resident(a.shape, 2) for a in consts],
        out_specs=tok,
        out_shape=jax.ShapeDtypeStruct((bsz, seq, e), BF16),
        scratch_shapes=[pltpu.VMEM((seq, cb), F32),
                        pltpu.VMEM((FFT_K1, 2 * FFT_N2, cb), F32),
                        pltpu.VMEM((SHORT_CHUNK + 2 * STAGE_PAD, cb), F32)],
        compiler_params=_params(("arbitrary", "arbitrary")),
        name="long_conv",
    )(x0, x1, v, zs, cw, cbias, h_bias.reshape(1, e), kf, *consts)


def _final_norm_kernel(h_ref, g_ref, o_ref):
    x = h_ref[...]
    ms = jnp.mean(x * x, axis=-1, keepdims=True)
    o_ref[...] = x * lax.rsqrt(ms + EPS) * g_ref[...]


def _final_norm(h, g, *, tm=1024):
    bn, t, d = h.shape
    m = bn * t
    tm = min(tm, m)
    out = pl.pallas_call(
        _final_norm_kernel,
        grid=(m // tm,),
        in_specs=[pl.BlockSpec((tm, d), lambda i: (i, 0)),
                  pl.BlockSpec((1, d), lambda i: (0, 0))],
        out_specs=pl.BlockSpec((tm, d), lambda i: (i, 0)),
        out_shape=jax.ShapeDtypeStruct((m, d), F32),
        compiler_params=_params(("arbitrary",)),
        name="final_norm",
    )(h.reshape(m, d), g.reshape(1, d))
    return out.reshape(bn, t, d)


def kernel(x, c, ctx, c_ctx, norm_g, ada_w, ada_b, final_g, cv_w_in, cv_dw_w, cv_dw_b, cv_ln_g, cv_ln_b, cv_w_out, ml_w_in, ml_conv_w, ml_conv_b, ml_w_q, ml_w_k, ml_w_v, ml_w_o, ml_b_o, ml_w_gates, ml_b_gates, ml_mh_g, ml_skip, ml_w_out, hy_w_in, hy_conv_w, hy_conv_b, hy_f_w1, hy_f_b1, hy_f_freq1, hy_f_w2, hy_f_b2, hy_f_freq2, hy_f_w3, hy_f_b3, hy_h_bias, hy_w_out):
    depth = norm_g.shape[0]
    bsz, seq, d = x.shape
    n_ctx = ctx.shape[1]
    readers = [i for i in range(depth) if i % N_MIXERS == 1]
    last_reader = readers[-1] if readers else -1

    cond_rows = -(-(bsz + 1) // SUBLANES) * SUBLANES
    cond = jnp.concatenate([c, c_ctx[None], jnp.zeros((cond_rows - bsz - 1, d), F32)], axis=0)
    ada = _ada_params(cond, ada_w, ada_b)

    h_lat, h_ctx = x, ctx
    for i in range(depth):
        kind, j = i % N_MIXERS, i // N_MIXERS
        ctx_in = i <= last_reader
        ctx_out = i < last_reader
        sh, sc, gt = jnp.split(ada[i, :bsz], 3, axis=-1)
        sh_c, sc_c, gt_c = (jnp.broadcast_to(a, (bsz, d)) for a in jnp.split(ada[i, bsz:bsz + 1], 3, axis=-1))
        if kind == 0:
            w_in = cv_w_in[j].astype(BF16)
            w_out = cv_w_out[j].astype(BF16)
            streams = [(h_lat, sh, sc, gt, GRID_W)]
            if ctx_out:
                streams.append((h_ctx, sh_c, sc_c, gt_c, n_ctx))
            new = []
            for h, s_h, s_c, g_t, seg in streams:
                y, zs = _in_proj(h, norm_g[i], s_h, s_c, w_in, n_groups=3, n_out=2, epilogue=_epilogue_conv)
                yb = _conv_mid(y, zs, cv_dw_w[j], cv_dw_b[j], cv_ln_g[j], cv_ln_b[j], seg=seg)
                new.append(_out_matmul(yb, w_out, h, g_t))
            h_lat = new[0]
            if ctx_out:
                h_ctx = new[1]
        elif kind == 1:
            assert ctx_in and not ctx_out
            w_in = ml_w_in[j].astype(BF16)
            w_out = ml_w_out[j].astype(BF16)
            h_all = jnp.concatenate([h_ctx, h_lat], axis=1)
            sh_all = jnp.concatenate([sh, sh_c[:1]], axis=0)
            sc_all = jnp.concatenate([sc, sc_c[:1]], axis=0)
            xm, zs = _in_proj(h_all, norm_g[i], sh_all, sc_all, w_in, n_groups=2, n_out=2,
                              epilogue=_epilogue_gate_last, n_ctx=n_ctx, tm=256)
            q, k, kt, v, o, xc, gc, gr = _mlstm_pre(
                xm, ml_conv_w[j], ml_conv_b[j], ml_w_q[j], ml_w_k[j], ml_w_v[j], ml_w_o[j], ml_b_o[j],
                ml_w_gates[j], ml_b_gates[j], n_ctx=n_ctx, n_heads=MLSTM_HEADS)
            hs = _mlstm_scan(q, k, kt, v, gc, gr, n_heads=MLSTM_HEADS, n_ctx=n_ctx)
            h_lat = _mlstm_out(hs, o, xc, zs, ml_mh_g[j], ml_skip[j], w_out, h_lat, gt,
                               n_ctx=n_ctx, n_heads=MLSTM_HEADS)
        else:
            assert not ctx_out
            w_in = hy_w_in[j].astype(BF16)
            w_out = hy_w_out[j].astype(BF16)
            x0, x1, v, zs = _in_proj(h_lat, norm_g[i], sh, sc, w_in, n_groups=4, n_out=4,
                                     epilogue=_epilogue_gate_last)
            tables = _dft_tables()
            kfull = _hyena_filter(seq, hy_f_w1[j], hy_f_b1[j], hy_f_freq1[j], hy_f_w2[j], hy_f_b2[j],
                                  hy_f_freq2[j], hy_f_w3[j], hy_f_b3[j])
            kf = _filter_spectrum(kfull, tables)
            yb = _long_conv(x0, x1, v, zs, hy_conv_w[j], hy_conv_b[j], hy_h_bias[j], kf, tables)
            h_lat = _out_matmul(yb, w_out, h_lat, gt)
    return _final_norm(h_lat, final_g)
```

```python
import functools
import math

import jax
import jax.numpy as jnp
import numpy as np
from jax import lax
from jax.experimental import pallas as pl
from jax.experimental.pallas import tpu as pltpu

GRID_W = 64
N_MIXERS = 3
EPS = 1e-6
MLSTM_HEADS = 8
HYENA_EMB_DIM = 33
HYENA_FAST_DECAY = 0.3
HYENA_SLOW_DECAY = 1.5
HYENA_DECAY_TARGET = 1e-2

V7X_VMEM_LIMIT_BYTES = 56 * 1024 * 1024
SUBLANES = 8
BF16_ROWS = 16
LANES = 128
MXU_TILE = 256

F32 = jnp.float32
BF16 = jnp.bfloat16


def _tiles(m, n, tm, tn):
    tm, tn = min(tm, m), min(tn, n)
    assert m % tm == 0 and n % tn == 0, (m, n, tm, tn)
    return tm, tn


def _sigmoid(x):
    return 1.0 / (1.0 + jnp.exp(-x))


def _silu(x):
    return x * _sigmoid(x)


def _dot_bf16(a, b):
    return jnp.dot(a.astype(BF16), b.astype(BF16), preferred_element_type=F32)


def _params(semantics):
    return pltpu.CompilerParams(dimension_semantics=semantics,
                                vmem_limit_bytes=V7X_VMEM_LIMIT_BYTES)


def _resident(shape, n_grid):
    zeros = (0,) * len(shape)
    return pl.BlockSpec(shape, lambda *_: zeros, pipeline_mode=pl.Buffered(1))


def _ada_kernel(c_ref, w_ref, b_ref, o_ref):
    o_ref[0] = _dot_bf16(_silu(c_ref[...]), w_ref[0]) + b_ref[0]


def _ada_params(cond, ada_w, ada_b, *, tn=512):
    depth, d, n = ada_w.shape
    r = cond.shape[0]
    return pl.pallas_call(
        _ada_kernel,
        grid=(depth, n // tn),
        in_specs=[pl.BlockSpec((r, d), lambda i, j: (0, 0)),
                  pl.BlockSpec((1, d, tn), lambda i, j: (i, 0, j)),
                  pl.BlockSpec((1, 1, tn), lambda i, j: (i, 0, j))],
        out_specs=pl.BlockSpec((1, r, tn), lambda i, j: (i, 0, j)),
        out_shape=jax.ShapeDtypeStruct((depth, r, n), F32),
        compiler_params=_params(("arbitrary", "arbitrary")),
        name="ada_params",
    )(cond, ada_w, ada_b.reshape(depth, 1, n))


def _epilogue_conv(a, g, z):
    return a * _sigmoid(g), _silu(z)


def _epilogue_gate_last(*accs):
    return accs[:-1] + (_silu(accs[-1]),)


def _in_proj_kernel(*refs, n_groups, epilogue):
    h_ref, g_ref, sh_ref, sc_ref = refs[:4]
    w_refs = refs[4:4 + n_groups]
    o_refs = refs[4 + n_groups:-1]
    u_ref = refs[-1]

    @pl.when(pl.program_id(1) == 0)
    def _():
        x = h_ref[...]
        ms = jnp.mean(x * x, axis=-1, keepdims=True)
        y = x * lax.rsqrt(ms + EPS) * g_ref[...]
        u_ref[...] = (y * (1.0 + sc_ref[0]) + sh_ref[0]).astype(u_ref.dtype)

    u = u_ref[...]
    outs = epilogue(*[jnp.dot(u, w[...], preferred_element_type=F32) for w in w_refs])
    for o_ref, val in zip(o_refs, outs, strict=True):
        o_ref[...] = val.astype(o_ref.dtype)


IN_PROJ_ROWS = 512


def _in_proj(h, g, sh, sc, w, *, n_groups, n_out, epilogue, n_ctx=0, tm=IN_PROJ_ROWS, tn=512):
    bn, t, d = h.shape
    e = w.shape[1] // n_groups
    tm, tn = _tiles(t, e, tm, tn)
    assert n_ctx % tm == 0
    tiles_per_seq, ctx_tiles = t // tm, n_ctx // tm
    col_tiles = e // tn
    n_mod = sh.shape[0]

    def mod_row(i, j):
        return (jnp.where(i % tiles_per_seq < ctx_tiles, n_mod - 1, i // tiles_per_seq), 0, 0)

    w_specs = [pl.BlockSpec((d, tn), functools.partial(lambda i, j, k: (0, k * col_tiles + j), k=k))
               for k in range(n_groups)]
    outs = pl.pallas_call(
        functools.partial(_in_proj_kernel, n_groups=n_groups, epilogue=epilogue),
        grid=(bn * tiles_per_seq, col_tiles),
        in_specs=[
            pl.BlockSpec((tm, d), lambda i, j: (i, 0)),
            pl.BlockSpec((1, d), lambda i, j: (0, 0)),
            pl.BlockSpec((1, 1, d), mod_row),
            pl.BlockSpec((1, 1, d), mod_row),
        ] + w_specs,
        out_specs=[pl.BlockSpec((tm, tn), lambda i, j: (i, j))] * n_out,
        out_shape=[jax.ShapeDtypeStruct((bn * t, e), BF16)] * n_out,
        scratch_shapes=[pltpu.VMEM((tm, d), BF16)],
        compiler_params=_params(("arbitrary", "arbitrary")),
        name="in_proj",
    )(h.reshape(bn * t, d), g.reshape(1, d), sh.reshape(n_mod, 1, d),
      sc.reshape(n_mod, 1, d), *([w] * n_groups))
    return [o.reshape(bn, t, e) for o in outs]


def _out_matmul_kernel(y_ref, w_ref, h_ref, gt_ref, o_ref):
    acc = jnp.dot(y_ref[...], w_ref[...], preferred_element_type=F32)
    o_ref[...] = h_ref[...] + gt_ref[0] * acc


def _out_matmul(y, w, h, gt, *, tm=512, tn=1024):
    bn, t, e = y.shape
    d = w.shape[1]
    tm, tn = _tiles(t, d, tm, tn)
    tiles_per_seq = t // tm
    out = pl.pallas_call(
        _out_matmul_kernel,
        grid=(bn * tiles_per_seq, d // tn),
        in_specs=[
            pl.BlockSpec((tm, e), lambda i, j: (i, 0)),
            pl.BlockSpec((e, tn), lambda i, j: (0, j)),
            pl.BlockSpec((tm, tn), lambda i, j: (i, j)),
            pl.BlockSpec((1, 1, tn), lambda i, j: (i // tiles_per_seq, 0, j)),
        ],
        out_specs=pl.BlockSpec((tm, tn), lambda i, j: (i, j)),
        out_shape=jax.ShapeDtypeStruct((bn * t, d), F32),
        compiler_params=_params(("arbitrary", "arbitrary")),
        name="out_matmul",
    )(y.reshape(bn * t, e), w, h.reshape(bn * t, d), gt.reshape(bn, 1, d))
    return out.reshape(bn, t, d)


CONV_PAD = 16
LN_UNROLL = 8


def _conv_mid_kernel(y_ref, zs_ref, w_ref, b_ref, lg_ref, lb_ref, o_ref, xp_ref, xs_ref, cv_ref, *, seg, width):
    tb, e = y_ref.shape
    n_seg = tb // seg
    half = width // 2
    rows_s = xs_ref.shape[2]
    rb = SUBLANES
    zeros = jnp.zeros((CONV_PAD, e), F32)
    for s in range(n_seg):
        xp_ref[s, 0:CONV_PAD, :] = zeros
        xp_ref[s, CONV_PAD:CONV_PAD + seg, :] = y_ref[s * seg:(s + 1) * seg, :].astype(F32)
        xp_ref[s, CONV_PAD + seg:2 * CONV_PAD + seg, :] = zeros

    def lane_block(cb, carry):
        lanes = pl.ds(pl.multiple_of(cb * LANES, LANES), LANES)
        for sh in range(rb):
            for s in range(n_seg):
                xs_ref[sh, s] = xp_ref[s, sh:sh + rows_s, lanes]
        taps = [w_ref[k:k + 1, lanes] for k in range(width)]
        bias = b_ref[:, lanes]
        for s in range(n_seg):
            for r in range(seg // rb):
                acc = bias
                for k in range(width):
                    off = CONV_PAD - half + k
                    base = (off // rb) * rb + r * rb
                    acc = acc + taps[k] * xs_ref[off % rb, s, base:base + rb, :]
                cv_ref[s * seg + r * rb:s * seg + (r + 1) * rb, lanes] = acc
        return carry

    lax.fori_loop(0, e // LANES, lane_block, 0)

    def row_block(i, carry):
        rows = pl.ds(pl.multiple_of(i * rb, rb), rb)
        cv = cv_ref[rows, :]
        mu = jnp.mean(cv, axis=-1, keepdims=True)
        xc = cv - mu
        var = jnp.mean(xc * xc, axis=-1, keepdims=True)
        yn = xc * lax.rsqrt(var + EPS) * lg_ref[...] + lb_ref[...]
        o_ref[rows, :] = (_silu(yn) * zs_ref[rows, :].astype(F32)).astype(o_ref.dtype)
        return carry

    lax.fori_loop(0, tb // rb, row_block, 0, unroll=LN_UNROLL)


def _conv_mid(y, zs, dw_w, dw_b, ln_g, ln_b, *, seg, tb=256):
    bn, t, e = y.shape
    width = dw_w.shape[0]
    m = bn * t
    tb = min(tb, m)
    assert m % tb == 0 and tb % seg == 0 and t % seg == 0 and width // 2 <= CONV_PAD
    row = lambda a: a.reshape(1, e)
    tok = pl.BlockSpec((tb, e), lambda i: (i, 0))
    vec = pl.BlockSpec((1, e), lambda i: (0, 0))
    out = pl.pallas_call(
        functools.partial(_conv_mid_kernel, seg=seg, width=width),
        grid=(m // tb,),
        in_specs=[tok, tok, pl.BlockSpec((width, e), lambda i: (0, 0)), vec, vec, vec],
        out_specs=tok,
        out_shape=jax.ShapeDtypeStruct((m, e), BF16),
        scratch_shapes=[pltpu.VMEM((tb // seg, seg + 2 * CONV_PAD, e), F32),
                        pltpu.VMEM((SUBLANES, tb // seg, seg + 2 * CONV_PAD - SUBLANES, LANES), F32),
                        pltpu.VMEM((tb, e), F32)],
        compiler_params=_params(("arbitrary",)),
        name="conv_mid",
    )(y.reshape(m, e), zs.reshape(m, e), dw_w, row(dw_b), row(ln_g), row(ln_b))
    return out.reshape(bn, t, e)


CONV_DFT = 96
CONV_FREQS = CONV_DFT // 2 + 1
CONV_HALF = -(-CONV_FREQS // SUBLANES) * SUBLANES
CONV_LANE_UNROLL = 4


def _conv_dft_tables(seg, width, n_seg):
    f = np.arange(CONV_FREQS)
    th = 2 * np.pi * np.outer(f, np.arange(seg)) / CONV_DFT
    fwd = np.zeros((2 * CONV_HALF, seg))
    fwd[:CONV_FREQS] = np.cos(th)
    fwd[CONV_HALF:CONV_HALF + CONV_FREQS] = -np.sin(th)
    c = np.where((f == 0) | (f == CONV_DFT // 2), 1.0, 2.0) / CONV_DFT
    inv = np.zeros((seg, 2 * CONV_HALF))
    inv[:, :CONV_FREQS] = (c[:, None] * np.cos(th)).T
    inv[:, CONV_HALF:CONV_HALF + CONV_FREQS] = (-c[:, None] * np.sin(th)).T
    eye = np.eye(n_seg)
    thw = 2 * np.pi * np.outer(f, np.arange(width) - width // 2) / CONV_DFT
    wtab = np.zeros((2 * CONV_HALF, width))
    wtab[:CONV_FREQS] = np.cos(thw)
    wtab[CONV_HALF:CONV_HALF + CONV_FREQS] = np.sin(thw)
    bf = lambda a: jnp.asarray(a, F32).astype(BF16)
    return bf(np.kron(eye, fwd)), bf(np.kron(eye, inv)), jnp.asarray(wtab, F32)


def _conv_dft_kernel(y_ref, zs_ref, w_ref, b_ref, lg_ref, lb_ref, fwd_ref, inv_ref, wtab_ref, o_ref,
                     h_ref, cv_ref, *, n_seg):
    tb, e = y_ref.shape
    rb = SUBLANES

    @pl.when(pl.program_id(0) == 0)
    def _():
        h_ref[...] = jnp.dot(wtab_ref[...], w_ref[...], preferred_element_type=F32,
                             precision=lax.Precision.HIGHEST)

    def lane_block(cb, carry):
        lanes = pl.ds(pl.multiple_of(cb * MXU_TILE, MXU_TILE), MXU_TILE)
        x = jnp.dot(fwd_ref[...], y_ref[:, lanes], preferred_element_type=F32)
        hr = h_ref[0:CONV_HALF, lanes]
        hi = h_ref[CONV_HALF:2 * CONV_HALF, lanes]
        parts = []
        for s in range(n_seg):
            xr = x[s * 2 * CONV_HALF:s * 2 * CONV_HALF + CONV_HALF]
            xi = x[s * 2 * CONV_HALF + CONV_HALF:(s + 1) * 2 * CONV_HALF]
            parts += [xr * hr - xi * hi, xr * hi + xi * hr]
        yf = jnp.concatenate(parts, axis=0).astype(BF16)
        cv_ref[:, lanes] = jnp.dot(inv_ref[...], yf, preferred_element_type=F32) + b_ref[:, lanes]
        return carry

    lax.fori_loop(0, e // MXU_TILE, lane_block, 0, unroll=CONV_LANE_UNROLL)

    def row_block(i, carry):
        rows = pl.ds(pl.multiple_of(i * rb, rb), rb)
        cv = cv_ref[rows, :]
        mu = jnp.mean(cv, axis=-1, keepdims=True)
        xc = cv - mu
        var = jnp.mean(xc * xc, axis=-1, keepdims=True)
        yn = xc * lax.rsqrt(var + EPS) * lg_ref[...] + lb_ref[...]
        o_ref[rows, :] = (_silu(yn) * zs_ref[rows, :].astype(F32)).astype(o_ref.dtype)
        return carry

    lax.fori_loop(0, tb // rb, row_block, 0, unroll=LN_UNROLL)


def _conv_mid_dft(y, zs, dw_w, dw_b, ln_g, ln_b, *, seg, tb=256):
    bn, t, e = y.shape
    width = dw_w.shape[0]
    m = bn * t
    tb = min(tb, m)
    assert m % tb == 0 and tb % seg == 0 and t % seg == 0 and seg + width - 1 <= CONV_DFT
    n_seg = tb // seg
    fwd, inv, wtab = _conv_dft_tables(seg, width, n_seg)
    row = lambda a: a.reshape(1, e)
    tok = pl.BlockSpec((tb, e), lambda i: (i, 0))
    out = pl.pallas_call(
        functools.partial(_conv_dft_kernel, n_seg=n_seg),
        grid=(m // tb,),
        in_specs=[tok, tok, _resident((width, e), 1), _resident((1, e), 1), _resident((1, e), 1),
                  _resident((1, e), 1), _resident(fwd.shape, 1), _resident(inv.shape, 1),
                  _resident(wtab.shape, 1)],
        out_specs=tok,
        out_shape=jax.ShapeDtypeStruct((m, e), BF16),
        scratch_shapes=[pltpu.VMEM((2 * CONV_HALF, e), F32), pltpu.VMEM((tb, e), F32)],
        compiler_params=_params(("arbitrary",)),
        name="conv_mid_dft",
    )(y.reshape(m, e), zs.reshape(m, e), dw_w, row(dw_b), row(ln_g), row(ln_b), fwd, inv, wtab)
    return out.reshape(bn, t, e)


STAGE_PAD = 8


def _expand_block_diag(w):
    g, bi, bo = w.shape
    per = MXU_TILE // bi
    wt = w.reshape(g // per, per, bi, bo)
    eye = jnp.eye(per, dtype=w.dtype)
    return jnp.einsum('tgio,gh->tgiho', wt, eye).reshape(g // per, per * bi, per * bo)


def _mlstm_pre_kernel(xm_ref, prev_ref, next_ref, cw_ref, cb_ref, wq_ref, wk_ref, wkt_ref, wv_ref, wo_ref,
                      bo_ref, wg_ref, wgt_ref, bg_ref, bgt_ref,
                      q_ref, k_ref, kt_ref, v_ref, o_ref, xc_ref, gc_ref, gr_ref, stage_ref,
                      *, tiles_per_seq, ctx_tiles, q_scale):
    tb, e = xm_ref.shape
    t = pl.program_id(0) % tiles_per_seq
    has_prev = jnp.logical_and(t != 0, t != ctx_tiles).astype(F32)
    has_next = jnp.logical_and(t != ctx_tiles - 1, t != tiles_per_seq - 1).astype(F32)
    gc = jnp.zeros(gc_ref.shape, F32) + bg_ref[...]
    gr = jnp.zeros(gr_ref.shape, F32) + bgt_ref[...]
    nt = (((1,), (1,)), ((), ()))
    for j in range(e // MXU_TILE):
        lanes = slice(j * MXU_TILE, (j + 1) * MXU_TILE)
        xm = xm_ref[:, lanes]
        stage_ref[STAGE_PAD - 1:STAGE_PAD, :] = \
            prev_ref[:, lanes].astype(F32)[BF16_ROWS - 1:BF16_ROWS, :] * has_prev
        stage_ref[STAGE_PAD:STAGE_PAD + tb, :] = xm.astype(F32)
        stage_ref[STAGE_PAD + tb:STAGE_PAD + tb + 1, :] = next_ref[:, lanes].astype(F32)[0:1, :] * has_next
        cw = cw_ref[:, lanes]
        pre = (cb_ref[:, lanes] + cw[0:1] * stage_ref[STAGE_PAD - 1:STAGE_PAD - 1 + tb, :]
               + cw[1:2] * stage_ref[STAGE_PAD:STAGE_PAD + tb, :]
               + cw[2:3] * stage_ref[STAGE_PAD + 1:STAGE_PAD + 1 + tb, :])
        xcb = _silu(pre).astype(BF16)
        xc_ref[:, lanes] = xcb
        q = jnp.dot(xcb, wq_ref[j], preferred_element_type=F32)
        k = jnp.dot(xcb, wk_ref[j], preferred_element_type=F32)
        v = jnp.dot(xm, wv_ref[j], preferred_element_type=F32)
        o = jnp.dot(xcb, wo_ref[j], preferred_element_type=F32) + bo_ref[:, lanes]
        kt_ref[lanes, :] = lax.dot_general(wkt_ref[j], xcb, nt, preferred_element_type=F32).astype(BF16)
        qb, kb, vb = q.astype(BF16), k.astype(BF16), v.astype(BF16)
        q_ref[:, lanes] = (q * q_scale).astype(BF16)
        k_ref[:, lanes] = kb
        v_ref[:, lanes] = vb
        o_ref[:, lanes] = _sigmoid(o).astype(BF16)
        for i, a in enumerate((qb, kb, vb)):
            gc = gc + jnp.dot(a, wg_ref[i, lanes, :], preferred_element_type=F32)
            gr = gr + lax.dot_general(wgt_ref[i, :, lanes], a, nt, preferred_element_type=F32)
    gc_ref[...] = gc
    gr_ref[...] = gr


def _mlstm_pre(xm, conv_w, conv_b, w_q, w_k, w_v, w_o, b_o, w_gates, b_gates, *, lead, n_ctx, n_heads, tb=256):
    bsz, t_all, e = xm.shape
    t = t_all - lead
    assert t % tb == 0 and n_ctx % tb == 0 and lead % tb == 0
    tiles_per_seq, ctx_tiles = t // tb, n_ctx // tb
    tiles_all, lead_tiles = t_all // tb, lead // tb
    m = bsz * t
    n_g = w_gates.shape[1]
    hpt = tb // BF16_ROWS
    n_halo = bsz * t_all // BF16_ROWS

    def src_tile(i):
        return (i // tiles_per_seq) * tiles_all + lead_tiles + i % tiles_per_seq
    bd = lambda w: _expand_block_diag(w).astype(BF16)
    wq, wk, wv, wo = bd(w_q), bd(w_k), bd(w_v), bd(w_o)
    wkt = jnp.swapaxes(wk, 1, 2)
    wg = w_gates.reshape(3, e, n_g).astype(BF16)
    wgt = jnp.swapaxes(wg, 1, 2)
    tok = pl.BlockSpec((tb, e), lambda i: (i, 0))
    row = lambda a: a.reshape(1, -1)
    consts = [conv_w, row(conv_b), wq, wk, wkt, wv, wo, row(b_o), wg, wgt, row(b_gates), b_gates.reshape(-1, 1)]
    sds = jax.ShapeDtypeStruct
    xm2 = xm.reshape(bsz * t_all, e)
    outs = pl.pallas_call(
        functools.partial(_mlstm_pre_kernel, tiles_per_seq=tiles_per_seq, ctx_tiles=ctx_tiles,
                          q_scale=(e // n_heads) ** -0.5),
        grid=(m // tb,),
        in_specs=[pl.BlockSpec((tb, e), lambda i: (src_tile(i), 0)),
                  pl.BlockSpec((BF16_ROWS, e), lambda i: (jnp.maximum(src_tile(i) * hpt - 1, 0), 0)),
                  pl.BlockSpec((BF16_ROWS, e), lambda i: (jnp.minimum((src_tile(i) + 1) * hpt, n_halo - 1), 0))]
                 + [_resident(a.shape, 1) for a in consts],
        out_specs=[tok, tok,
                   pl.BlockSpec((None, e, tb), lambda i: (i // tiles_per_seq, 0, i % tiles_per_seq)),
                   tok, tok, tok,
                   pl.BlockSpec((tb, n_g), lambda i: (i, 0)),
                   pl.BlockSpec((None, n_g, tb), lambda i: (i // tiles_per_seq, 0, i % tiles_per_seq))],
        out_shape=[sds((m, e), BF16), sds((m, e), BF16), sds((bsz, e, t), BF16), sds((m, e), BF16),
                   sds((m, e), BF16), sds((m, e), BF16), sds((m, n_g), F32), sds((bsz, n_g, t), F32)],
        scratch_shapes=[pltpu.VMEM((tb + 2 * STAGE_PAD, MXU_TILE), F32)],
        compiler_params=_params(("arbitrary",)),
        name="mlstm_pre",
    )(xm2, xm2, xm2, *consts)
    q, k, kt, v, o, xc, gc, gr = outs
    r3 = lambda a: a.reshape(bsz, t, -1)
    return r3(q), r3(k), kt, r3(v), r3(o), r3(xc), r3(gc), gr


SCAN_CHUNK = 256


def _log_sigmoid(x):
    return jnp.minimum(x, 0.0) - jnp.log1p(jnp.exp(-jnp.abs(x)))


SCAN_HEADS_PER_STEP = 2


def _mlstm_scan_kernel(q_ref, k_ref, kt_ref, v_ref, gc_ref, gr_ref, o_ref,
                       c_ref, cb_ref, n_ref, m_ref, *, n_heads, heads_per_step):
    direction = pl.program_id(0)

    @pl.when(pl.program_id(3) == 0)
    def _():
        c_ref[...] = jnp.zeros_like(c_ref)
        cb_ref[...] = jnp.zeros_like(cb_ref)
        n_ref[...] = jnp.zeros_like(n_ref)
        m_ref[...] = jnp.full_like(m_ref, -1e30)

    L = q_ref.shape[1]
    dh = q_ref.shape[2] // heads_per_step
    n_g = gc_ref.shape[2]
    gc = gc_ref[0]
    gr = gr_ref[0]
    lane = lax.broadcasted_iota(jnp.int32, (L, n_g), 1)
    sub = lax.broadcasted_iota(jnp.int32, (n_g, L), 0)
    sign = 1 - 2 * direction
    diff = (lax.broadcasted_iota(jnp.int32, (L, L), 0)
            - lax.broadcasted_iota(jnp.int32, (L, L), 1)) * sign
    causal = diff >= 0
    anti = diff <= 0

    for hh in range(heads_per_step):
        lanes = slice(hh * dh, (hh + 1) * dh)
        head = pl.program_id(2) * heads_per_step + hh
        col_i = direction * (2 * n_heads) + head
        col_f = col_i + n_heads
        li_col = jnp.sum(jnp.where(lane == col_i, gc, 0.0), axis=1, keepdims=True)
        lf_col = _log_sigmoid(jnp.sum(jnp.where(lane == col_f, gc, 0.0), axis=1, keepdims=True))
        li_row = jnp.sum(jnp.where(sub == col_i, gr, 0.0), axis=0, keepdims=True)
        lf_row = _log_sigmoid(jnp.sum(jnp.where(sub == col_f, gr, 0.0), axis=0, keepdims=True))
        b_col = jnp.sum(jnp.where(causal, lf_row, 0.0), axis=1, keepdims=True)
        b_row = jnp.sum(jnp.where(anti, lf_col, 0.0), axis=0, keepdims=True)
        total = jnp.sum(lf_row, axis=1, keepdims=True)

        m = m_ref[hh]
        log_d = jnp.where(causal, b_col - b_row + li_row, -jnp.inf)
        log_inter = b_col + m
        m_t = jnp.maximum(log_inter, jnp.max(log_d, axis=1, keepdims=True))
        dmat = jnp.exp(log_d - m_t)
        a = jnp.exp(log_inter - m_t)

        q = q_ref[0, :, lanes]
        kt = kt_ref[0, lanes, :]
        v = v_ref[0, :, lanes]
        s = jnp.dot(q, kt, preferred_element_type=F32) * dmat
        num = a * jnp.dot(q, cb_ref[hh], preferred_element_type=F32) \
            + jnp.dot(s.astype(BF16), v, preferred_element_type=F32)
        qn = jnp.sum(q.astype(F32) * n_ref[hh], axis=1, keepdims=True)
        den = a * qn + jnp.sum(s, axis=1, keepdims=True)
        o_ref[0, 0, :, lanes] = (num / jnp.maximum(jnp.abs(den), jnp.exp(-m_t))).astype(o_ref.dtype)

        log_w_row = total - b_row + li_row
        m_new = jnp.maximum(total + m, jnp.max(log_w_row, axis=1, keepdims=True))
        decay = jnp.exp(total + m - m_new)
        kwt = kt * jnp.exp(log_w_row - m_new).astype(BF16)
        c_new = decay * c_ref[hh] + jnp.dot(kwt, v, preferred_element_type=F32)
        c_ref[hh] = c_new
        cb_ref[hh] = c_new.astype(BF16)
        kw = k_ref[0, :, lanes].astype(F32) * jnp.exp(total - b_col + li_col - m_new)
        n_ref[hh] = decay * n_ref[hh] + jnp.sum(kw, axis=0, keepdims=True)
        m_ref[hh] = m_new


def _mlstm_scan(q, k, kt, v, gc, gr, *, n_heads, n_ctx, chunk=SCAN_CHUNK, heads_per_step=SCAN_HEADS_PER_STEP):
    bsz, t, e = q.shape
    dh = e // n_heads
    hp = heads_per_step
    L = chunk
    assert t % L == 0 and n_ctx % L == 0 and n_heads % hp == 0
    nc, nc_ctx = t // L, n_ctx // L

    def cidx(d, c):
        rev = jnp.where(c < nc_ctx, nc_ctx - 1 - c, nc - 1 - c + nc_ctx)
        return jnp.where(d == 0, c, rev)

    tok = pl.BlockSpec((1, L, hp * dh), lambda d, b, h, c: (b, cidx(d, c), h))
    return pl.pallas_call(
        functools.partial(_mlstm_scan_kernel, n_heads=n_heads, heads_per_step=hp),
        grid=(2, bsz, n_heads // hp, nc),
        in_specs=[
            tok, tok,
            pl.BlockSpec((1, hp * dh, L), lambda d, b, h, c: (b, h, cidx(d, c))),
            tok,
            pl.BlockSpec((1, L, 4 * n_heads), lambda d, b, h, c: (b, cidx(d, c), 0)),
            pl.BlockSpec((1, 4 * n_heads, L), lambda d, b, h, c: (b, 0, cidx(d, c))),
        ],
        out_specs=pl.BlockSpec((1, 1, L, hp * dh), lambda d, b, h, c: (d, b, cidx(d, c), h)),
        out_shape=jax.ShapeDtypeStruct((2, bsz, t, e), BF16),
        scratch_shapes=[pltpu.VMEM((hp, dh, dh), F32), pltpu.VMEM((hp, dh, dh), BF16),
                        pltpu.VMEM((hp, 1, dh), F32), pltpu.VMEM((hp, 1, 1), F32)],
        compiler_params=_params(("arbitrary",) * 4),
        name="mlstm_scan",
    )(q, k, kt, v, gc, gr)


def _mlstm_out_kernel(hf_ref, hb_ref, o_ref, xc_ref, zs_ref, mh_ref, skip_ref, w_ref, res_ref, gt_ref,
                      out_ref, y_ref, *, n_heads):
    tm, e = o_ref.shape
    dh = e // n_heads
    for hd in range(n_heads):
        lanes = slice(hd * dh, (hd + 1) * dh)
        h = hf_ref[:, lanes].astype(F32) + hb_ref[:, lanes].astype(F32)
        mu = jnp.mean(h, axis=-1, keepdims=True)
        hc = h - mu
        var = jnp.mean(hc * hc, axis=-1, keepdims=True)
        hn = hc * lax.rsqrt(var + EPS) * mh_ref[:, lanes]
        y = (o_ref[:, lanes].astype(F32) * hn + skip_ref[:, lanes] * xc_ref[:, lanes].astype(F32)) \
            * zs_ref[:, lanes].astype(F32)
        y_ref[:, lanes] = y.astype(BF16)
    acc = jnp.dot(y_ref[...], w_ref[...], preferred_element_type=F32)
    out_ref[...] = res_ref[...] + gt_ref[0] * acc


def _mlstm_out(hs, o, xc, zs, mh_g, skip, w_out, h_lat, gt, *, lead, n_ctx, n_heads, tm=256):
    _, bsz, t, e = hs.shape
    n = t - n_ctx
    d = w_out.shape[1]
    assert n % tm == 0 and n_ctx % tm == 0 and lead % tm == 0
    tiles, off = n // tm, n_ctx // tm
    zs_off = off + lead // tm
    tok = pl.BlockSpec((None, tm, e), lambda i: (i // tiles, off + i % tiles, 0))
    zs_tok = pl.BlockSpec((None, tm, e), lambda i: (i // tiles, zs_off + i % tiles, 0))
    row = lambda a: a.reshape(1, -1)
    out = pl.pallas_call(
        functools.partial(_mlstm_out_kernel, n_heads=n_heads),
        grid=(bsz * tiles,),
        in_specs=[pl.BlockSpec((None, None, tm, e), lambda i: (0, i // tiles, off + i % tiles, 0)),
                  pl.BlockSpec((None, None, tm, e), lambda i: (1, i // tiles, off + i % tiles, 0)),
                  tok, tok, zs_tok, _resident((1, e), 1), _resident((1, e), 1), _resident((e, d), 1),
                  pl.BlockSpec((tm, d), lambda i: (i, 0)),
                  pl.BlockSpec((1, 1, d), lambda i: (i // tiles, 0, 0))],
        out_specs=pl.BlockSpec((tm, d), lambda i: (i, 0)),
        out_shape=jax.ShapeDtypeStruct((bsz * n, d), F32),
        scratch_shapes=[pltpu.VMEM((tm, e), BF16)],
        compiler_params=_params(("arbitrary",)),
        name="mlstm_out",
    )(hs, hs, o, xc, zs, row(mh_g), row(skip), w_out, h_lat.reshape(bsz * n, d), gt.reshape(bsz, 1, d))
    return out.reshape(bsz, n, d)


def _filter_features(seq):
    r = np.arange(2 * seq)
    lag = np.where(r < seq, r, 2 * seq - r).clip(0, seq - 1).astype(np.float64)
    t = lag / (seq - 1)
    bands = (HYENA_EMB_DIM - 1) // 2
    ang = 2.0 * math.pi * lag / seq
    fr = np.linspace(1e-4, bands - 1, bands)
    feat = np.concatenate([t[:, None], np.cos(fr[None] * ang[:, None]), -np.sin(fr[None] * ang[:, None])], axis=-1)
    return jnp.asarray(feat, F32), jnp.asarray(t[:, None], F32)


def _filter_kernel(feat_ref, t_ref, w1_ref, b1_ref, f1_ref, w2_ref, b2_ref, f2_ref, w3_ref, b3_ref, dl_ref,
                   o_ref, *, seq):
    tr = o_ref.shape[0]
    h = jnp.sin(f1_ref[...] * (_dot_bf16(feat_ref[...], w1_ref[...]) + b1_ref[...]))
    h = jnp.sin(f2_ref[...] * (_dot_bf16(h, w2_ref[...]) + b2_ref[...]))
    h = _dot_bf16(h, w3_ref[...]) + b3_ref[...]
    out = h * jnp.exp(-t_ref[...] * dl_ref[...])
    row = pl.program_id(0) * tr + lax.broadcasted_iota(jnp.int32, (tr, 1), 0)
    o_ref[...] = jnp.where(row == seq, 0.0, out)


def _hyena_filter(seq, f_w1, f_b1, f_freq1, f_w2, f_b2, f_freq2, f_w3, f_b3, *, tr=512, tn=1024):
    e = f_w3.shape[1] // 2
    feat, t = _filter_features(seq)
    lo = math.log(HYENA_DECAY_TARGET) / HYENA_FAST_DECAY
    hi = math.log(HYENA_DECAY_TARGET) / HYENA_SLOW_DECAY
    deltas = jnp.asarray(np.abs(np.linspace(lo, hi, e)), F32).reshape(1, e)
    fdim = f_w1.shape[1]
    row_tiles, col_tiles = 2 * seq // tr, e // tn
    half = lambda i: i // (row_tiles // 2)
    row = lambda a: a.reshape(1, -1)
    small = lambda shape: pl.BlockSpec(shape, lambda i, j: (0, 0))
    return pl.pallas_call(
        functools.partial(_filter_kernel, seq=seq),
        grid=(row_tiles, col_tiles),
        in_specs=[pl.BlockSpec((tr, HYENA_EMB_DIM), lambda i, j: (i, 0)),
                  pl.BlockSpec((tr, 1), lambda i, j: (i, 0)),
                  small((HYENA_EMB_DIM, fdim)), small((1, fdim)), small((1, fdim)),
                  small((fdim, fdim)), small((1, fdim)), small((1, fdim)),
                  pl.BlockSpec((fdim, tn), lambda i, j: (0, half(i) * col_tiles + j)),
                  pl.BlockSpec((1, tn), lambda i, j: (0, half(i) * col_tiles + j)),
                  pl.BlockSpec((1, tn), lambda i, j: (0, j))],
        out_specs=pl.BlockSpec((tr, tn), lambda i, j: (i, j)),
        out_shape=jax.ShapeDtypeStruct((2 * seq, e), F32),
        compiler_params=_params(("arbitrary", "arbitrary")),
        name="hyena_filter",
    )(feat, t, f_w1, row(f_b1), row(f_freq1), f_w2, row(f_b2), row(f_freq2), f_w3, row(f_b3), deltas)


FFT_N1 = 64
FFT_N2 = 128
FFT_N = FFT_N1 * FFT_N2
FFT_K1 = FFT_N1 // 2 + 1
SHORT_CHUNK = 256
K1_UNROLL = 3
OUTER_UNROLL = 2
_REAL_ONLY_COLS = (1, 2 * (FFT_K1 - 1) + 1)


def _dft_tables():
    n1h = FFT_N1 // 2
    k1 = np.arange(FFT_K1)
    eye = np.eye(SUBLANES)

    def outer_fwd(n_rows):
        th = 2 * np.pi * np.outer(k1, np.arange(n_rows)) / FFT_N1
        f = np.stack([np.cos(th), -np.sin(th)], axis=1).reshape(2 * FFT_K1, n_rows)
        return np.kron(f, eye)

    n2 = np.arange(FFT_N2)
    k2 = np.arange(FFT_N2)
    ph = 2 * np.pi * n2[None, None, :] * (k1[:, None, None] + FFT_N1 * k2[None, :, None]) / FFT_N
    gr, gi = np.cos(ph), -np.sin(ph)
    g = np.concatenate([np.concatenate([gr, -gi], axis=2), np.concatenate([gi, gr], axis=2)], axis=1)
    ginv = np.swapaxes(g, 1, 2)
    th2 = 2 * np.pi * np.outer(np.arange(n1h), k1) / FFT_N1
    c = np.where((k1 == 0) | (k1 == n1h), 1.0, 2.0)[None, :] / FFT_N
    finv = np.stack([c * np.cos(th2), -c * np.sin(th2)], axis=2).reshape(n1h, 2 * FFT_K1)
    finv = finv[:, [col for col in range(2 * FFT_K1) if col not in _REAL_ONLY_COLS]]
    bf = lambda a: jnp.asarray(a, F32).astype(BF16)
    return dict(s1_half=bf(outer_fwd(n1h)), s1_full=bf(outer_fwd(FFT_N1)), g=bf(g), ginv=bf(ginv),
                i2=bf(np.kron(finv, eye)))


def _outer_forward(src_ref, s1_ref, dst_ref, n1_rows):
    def body(n2h, carry):
        r0 = pl.multiple_of(n2h * SUBLANES, SUBLANES)
        tiles = [src_ref[pl.ds(n1 * FFT_N2 + r0, SUBLANES), :] for n1 in range(n1_rows)]
        rhs = jnp.concatenate(tiles, axis=0).astype(BF16)
        out = jnp.dot(s1_ref[...], rhs, preferred_element_type=F32)
        for k1 in range(FFT_K1):
            for ri in range(2):
                row = (2 * k1 + ri) * SUBLANES
                dst_ref[k1, pl.ds(ri * FFT_N2 + r0, SUBLANES), :] = out[row:row + SUBLANES, :]
        return carry
    lax.fori_loop(0, FFT_N2 // SUBLANES, body, 0, unroll=OUTER_UNROLL)


def _filter_spectrum_kernel(k_ref, s1_ref, g_ref, kf_ref, a_ref):
    _outer_forward(k_ref, s1_ref, a_ref, FFT_N1)

    def body(k1, carry):
        kf_ref[k1] = jnp.dot(g_ref[k1], a_ref[k1].astype(BF16), preferred_element_type=F32)
        return carry
    lax.fori_loop(0, FFT_K1, body, 0)


def _filter_spectrum(kfull, tables, *, cb=MXU_TILE):
    e = kfull.shape[1]
    cb = min(cb, e)
    return pl.pallas_call(
        _filter_spectrum_kernel,
        grid=(e // cb,),
        in_specs=[pl.BlockSpec((FFT_N, cb), lambda c: (0, c)),
                  _resident(tables["s1_full"].shape, 1), _resident(tables["g"].shape, 1)],
        out_specs=pl.BlockSpec((FFT_K1, 2 * FFT_N2, cb), lambda c: (0, 0, c)),
        out_shape=jax.ShapeDtypeStruct((FFT_K1, 2 * FFT_N2, e), F32),
        scratch_shapes=[pltpu.VMEM((FFT_K1, 2 * FFT_N2, cb), F32)],
        compiler_params=_params(("arbitrary",)),
        name="filter_spectrum",
    )(kfull, tables["s1_full"], tables["g"])


def _short_conv(x_ref, cw_ref, cb_ref, group, c, stage_ref, n_rows):
    r0 = pl.multiple_of(c * SHORT_CHUNK, SHORT_CHUNK)
    prev0 = pl.multiple_of(jnp.maximum(r0 - BF16_ROWS, 0), BF16_ROWS)
    next0 = pl.multiple_of(jnp.minimum(r0 + SHORT_CHUNK, n_rows - BF16_ROWS), BF16_ROWS)
    prev = x_ref[pl.ds(prev0, BF16_ROWS), :].astype(F32)[BF16_ROWS - 1:BF16_ROWS, :] * (c > 0).astype(F32)
    nxt = x_ref[pl.ds(next0, BF16_ROWS), :].astype(F32)[0:1, :] * (r0 + SHORT_CHUNK < n_rows).astype(F32)
    stage_ref[STAGE_PAD - 1:STAGE_PAD, :] = prev
    stage_ref[STAGE_PAD:STAGE_PAD + SHORT_CHUNK, :] = x_ref[pl.ds(r0, SHORT_CHUNK), :].astype(F32)
    stage_ref[STAGE_PAD + SHORT_CHUNK:STAGE_PAD + SHORT_CHUNK + 1, :] = nxt
    w = cw_ref[group]
    return (cb_ref[group]
            + w[0:1] * stage_ref[STAGE_PAD - 1:STAGE_PAD - 1 + SHORT_CHUNK, :]
            + w[1:2] * stage_ref[STAGE_PAD:STAGE_PAD + SHORT_CHUNK, :]
            + w[2:3] * stage_ref[STAGE_PAD + 1:STAGE_PAD + 1 + SHORT_CHUNK, :])


def _long_conv_kernel(x0_ref, x1_ref, v_ref, zs_ref, cw_ref, cb_ref, hb_ref, kf_ref,
                      s1_ref, g_ref, ginv_ref, i2_ref, o_ref, w_ref, ab_ref, stage_ref):
    n_rows = x1_ref.shape[0]
    n_chunks = n_rows // SHORT_CHUNK

    def make_w(c, carry):
        rows = pl.ds(pl.multiple_of(c * SHORT_CHUNK, SHORT_CHUNK), SHORT_CHUNK)
        x1c = _short_conv(x1_ref, cw_ref, cb_ref, 1, c, stage_ref, n_rows)
        vc = _short_conv(v_ref, cw_ref, cb_ref, 2, c, stage_ref, n_rows)
        w_ref[rows, :] = x1c * vc
        return carry
    lax.fori_loop(0, n_chunks, make_w, 0)

    _outer_forward(w_ref, s1_ref, ab_ref, FFT_N1 // 2)

    def per_k1(k1, carry):
        x = jnp.dot(g_ref[k1], ab_ref[k1].astype(BF16), preferred_element_type=F32)
        kf = kf_ref[k1]
        xr, xi = x[:FFT_N2], x[FFT_N2:]
        kr, ki = kf[:FFT_N2], kf[FFT_N2:]
        y = jnp.concatenate([xr * kr - xi * ki, xr * ki + xi * kr], axis=0).astype(BF16)
        ab_ref[k1] = jnp.dot(ginv_ref[k1], y, preferred_element_type=F32)
        return carry
    lax.fori_loop(0, FFT_K1, per_k1, 0, unroll=K1_UNROLL)

    def outer_inverse(n2h, carry):
        r0 = pl.multiple_of(n2h * SUBLANES, SUBLANES)
        tiles = [ab_ref[k1, pl.ds(ri * FFT_N2 + r0, SUBLANES), :] for k1 in range(FFT_K1) for ri in range(2)
                 if 2 * k1 + ri not in _REAL_ONLY_COLS]
        rhs = jnp.concatenate(tiles, axis=0).astype(BF16)
        out = jnp.dot(i2_ref[...], rhs, preferred_element_type=F32)
        for n1 in range(FFT_N1 // 2):
            rows = pl.ds(n1 * FFT_N2 + r0, SUBLANES)
            w_ref[rows, :] = out[n1 * SUBLANES:(n1 + 1) * SUBLANES, :] + hb_ref[...] * w_ref[rows, :]
        return carry
    lax.fori_loop(0, FFT_N2 // SUBLANES, outer_inverse, 0, unroll=OUTER_UNROLL)

    def finish(c, carry):
        rows = pl.ds(pl.multiple_of(c * SHORT_CHUNK, SHORT_CHUNK), SHORT_CHUNK)
        x0c = _short_conv(x0_ref, cw_ref, cb_ref, 0, c, stage_ref, n_rows)
        o_ref[rows, :] = (x0c * w_ref[rows, :] * zs_ref[rows, :].astype(F32)).astype(o_ref.dtype)
        return carry
    lax.fori_loop(0, n_chunks, finish, 0)


def _long_conv(x0, x1, v, zs, conv_w, conv_b, h_bias, kf, tables, *, cb=MXU_TILE):
    bsz, seq, e = x0.shape
    assert seq == FFT_N // 2
    cb = min(cb, e)
    cw = conv_w.reshape(3, 3, e).transpose(1, 0, 2)
    cbias = conv_b.reshape(3, 1, e)
    tok = pl.BlockSpec((None, seq, cb), lambda c, b: (b, 0, c))
    consts = [tables["s1_half"], tables["g"], tables["ginv"], tables["i2"]]
    return pl.pallas_call(
        _long_conv_kernel,
        grid=(e // cb, bsz),
        in_specs=[tok, tok, tok, tok,
                  pl.BlockSpec((3, 3, cb), lambda c, b: (0, 0, c)),
                  pl.BlockSpec((3, 1, cb), lambda c, b: (0, 0, c)),
                  pl.BlockSpec((1, cb), lambda c, b: (0, c)),
                  pl.BlockSpec((FFT_K1, 2 * FFT_N2, cb), lambda c, b: (0, 0, c), pipeline_mode=pl.Buffered(1))]
                 + [_resident(a.shape, 2) for a in consts],
        out_specs=tok,
        out_shape=jax.ShapeDtypeStruct((bsz, seq, e), BF16),
        scratch_shapes=[pltpu.VMEM((seq, cb), F32),
                        pltpu.VMEM((FFT_K1, 2 * FFT_N2, cb), F32),
                        pltpu.VMEM((SHORT_CHUNK + 2 * STAGE_PAD, cb), F32)],
        compiler_params=_params(("arbitrary", "arbitrary")),
        name="long_conv",
    )(x0, x1, v, zs, cw, cbias, h_bias.reshape(1, e), kf, *consts)


def _final_norm_kernel(h_ref, g_ref, o_ref):
    x = h_ref[...]
    ms = jnp.mean(x * x, axis=-1, keepdims=True)
    o_ref[...] = x * lax.rsqrt(ms + EPS) * g_ref[...]


def _final_norm(h, g, *, tm=1024):
    bn, t, d = h.shape
    m = bn * t
    tm = min(tm, m)
    out = pl.pallas_call(
        _final_norm_kernel,
        grid=(m // tm,),
        in_specs=[pl.BlockSpec((tm, d), lambda i: (i, 0)),
                  pl.BlockSpec((1, d), lambda i: (0, 0))],
        out_specs=pl.BlockSpec((tm, d), lambda i: (i, 0)),
        out_shape=jax.ShapeDtypeStruct((m, d), F32),
        compiler_params=_params(("arbitrary",)),
        name="final_norm",
    )(h.reshape(m, d), g.reshape(1, d))
    return out.reshape(bn, t, d)


def kernel(x, c, ctx, c_ctx, norm_g, ada_w, ada_b, final_g, cv_w_in, cv_dw_w, cv_dw_b, cv_ln_g, cv_ln_b, cv_w_out, ml_w_in, ml_conv_w, ml_conv_b, ml_w_q, ml_w_k, ml_w_v, ml_w_o, ml_b_o, ml_w_gates, ml_b_gates, ml_mh_g, ml_skip, ml_w_out, hy_w_in, hy_conv_w, hy_conv_b, hy_f_w1, hy_f_b1, hy_f_freq1, hy_f_w2, hy_f_b2, hy_f_freq2, hy_f_w3, hy_f_b3, hy_h_bias, hy_w_out):
    depth = norm_g.shape[0]
    bsz, seq, d = x.shape
    n_ctx = ctx.shape[1]
    readers = [i for i in range(depth) if i % N_MIXERS == 1]
    last_reader = readers[-1] if readers else -1

    cond_rows = -(-(bsz + 1) // SUBLANES) * SUBLANES
    cond = jnp.concatenate([c, c_ctx[None], jnp.zeros((cond_rows - bsz - 1, d), F32)], axis=0)
    ada = _ada_params(cond, ada_w, ada_b)

    h_lat, h_ctx = x, ctx
    for i in range(depth):
        kind, j = i % N_MIXERS, i // N_MIXERS
        ctx_in = i <= last_reader
        ctx_out = i < last_reader
        sh, sc, gt = jnp.split(ada[i, :bsz], 3, axis=-1)
        sh_c, sc_c, gt_c = (jnp.broadcast_to(a, (bsz, d)) for a in jnp.split(ada[i, bsz:bsz + 1], 3, axis=-1))
        if kind == 0:
            w_in = cv_w_in[j].astype(BF16)
            w_out = cv_w_out[j].astype(BF16)
            streams = [(h_lat, sh, sc, gt, GRID_W, _conv_mid_dft)]
            if ctx_out:
                streams.append((h_ctx, sh_c, sc_c, gt_c, n_ctx, _conv_mid))
            new = []
            for h, s_h, s_c, g_t, seg, conv_mid in streams:
                y, zs = _in_proj(h, norm_g[i], s_h, s_c, w_in, n_groups=3, n_out=2, epilogue=_epilogue_conv)
                yb = conv_mid(y, zs, cv_dw_w[j], cv_dw_b[j], cv_ln_g[j], cv_ln_b[j], seg=seg)
                new.append(_out_matmul(yb, w_out, h, g_t))
            h_lat = new[0]
            if ctx_out:
                h_ctx = new[1]
        elif kind == 1:
            assert ctx_in and not ctx_out
            w_in = ml_w_in[j].astype(BF16)
            w_out = ml_w_out[j].astype(BF16)
            lead = (-n_ctx) % IN_PROJ_ROWS
            h_all = jnp.concatenate([jnp.zeros((bsz, lead, d), F32), h_ctx, h_lat], axis=1)
            sh_all = jnp.concatenate([sh, sh_c[:1]], axis=0)
            sc_all = jnp.concatenate([sc, sc_c[:1]], axis=0)
            xm, zs = _in_proj(h_all, norm_g[i], sh_all, sc_all, w_in, n_groups=2, n_out=2,
                              epilogue=_epilogue_gate_last, n_ctx=lead + n_ctx, tm=IN_PROJ_ROWS)
            q, k, kt, v, o, xc, gc, gr = _mlstm_pre(
                xm, ml_conv_w[j], ml_conv_b[j], ml_w_q[j], ml_w_k[j], ml_w_v[j], ml_w_o[j], ml_b_o[j],
                ml_w_gates[j], ml_b_gates[j], lead=lead, n_ctx=n_ctx, n_heads=MLSTM_HEADS)
            hs = _mlstm_scan(q, k, kt, v, gc, gr, n_heads=MLSTM_HEADS, n_ctx=n_ctx)
            h_lat = _mlstm_out(hs, o, xc, zs, ml_mh_g[j], ml_skip[j], w_out, h_lat, gt,
                               lead=lead, n_ctx=n_ctx, n_heads=MLSTM_HEADS)
        else:
            assert not ctx_out
            w_in = hy_w_in[j].astype(BF16)
            w_out = hy_w_out[j].astype(BF16)
            x0, x1, v, zs = _in_proj(h_lat, norm_g[i], sh, sc, w_in, n_groups=4, n_out=4,
                                     epilogue=_epilogue_gate_last)
            tables = _dft_tables()
            kfull = _hyena_filter(seq, hy_f_w1[j], hy_f_b1[j], hy_f_freq1[j], hy_f_w2[j], hy_f_b2[j],
                                  hy_f_freq2[j], hy_f_w3[j], hy_f_b3[j])
            kf = _filter_spectrum(kfull, tables)
            yb = _long_conv(x0, x1, v, zs, hy_conv_w[j], hy_conv_b[j], hy_h_bias[j], kf, tables)
            h_lat = _out_matmul(yb, w_out, h_lat, gt)
    return _final_norm(h_lat, final_g)
```

```python
import functools
import math

import jax
import jax.numpy as jnp
import numpy as np
from jax import lax
from jax.experimental import pallas as pl
from jax.experimental.pallas import tpu as pltpu

GRID_W = 64
N_MIXERS = 3
EPS = 1e-6
MLSTM_HEADS = 8
HYENA_EMB_DIM = 33
HYENA_FAST_DECAY = 0.3
HYENA_SLOW_DECAY = 1.5
HYENA_DECAY_TARGET = 1e-2

V7X_VMEM_LIMIT_BYTES = 56 * 1024 * 1024
SUBLANES = 8
BF16_ROWS = 16
LANES = 128
MXU_TILE = 256

F32 = jnp.float32
BF16 = jnp.bfloat16


def _tiles(m, n, tm, tn):
    tm, tn = min(tm, m), min(tn, n)
    assert m % tm == 0 and n % tn == 0, (m, n, tm, tn)
    return tm, tn


def _sigmoid(x):
    return 1.0 / (1.0 + jnp.exp(-x))


def _silu(x):
    return x * _sigmoid(x)


def _dot_bf16(a, b):
    return jnp.dot(a.astype(BF16), b.astype(BF16), preferred_element_type=F32)


def _params(semantics):
    return pltpu.CompilerParams(dimension_semantics=semantics,
                                vmem_limit_bytes=V7X_VMEM_LIMIT_BYTES)


def _resident(shape, n_grid):
    zeros = (0,) * len(shape)
    return pl.BlockSpec(shape, lambda *_: zeros, pipeline_mode=pl.Buffered(1))


def _ada_kernel(c_ref, w_ref, b_ref, o_ref):
    o_ref[0] = _dot_bf16(_silu(c_ref[...]), w_ref[0]) + b_ref[0]


def _ada_params(cond, ada_w, ada_b, *, tn=512):
    depth, d, n = ada_w.shape
    r = cond.shape[0]
    return pl.pallas_call(
        _ada_kernel,
        grid=(depth, n // tn),
        in_specs=[pl.BlockSpec((r, d), lambda i, j: (0, 0)),
                  pl.BlockSpec((1, d, tn), lambda i, j: (i, 0, j)),
                  pl.BlockSpec((1, 1, tn), lambda i, j: (i, 0, j))],
        out_specs=pl.BlockSpec((1, r, tn), lambda i, j: (i, 0, j)),
        out_shape=jax.ShapeDtypeStruct((depth, r, n), F32),
        compiler_params=_params(("arbitrary", "arbitrary")),
        name="ada_params",
    )(cond, ada_w, ada_b.reshape(depth, 1, n))


def _epilogue_conv(a, g, z):
    return a * _sigmoid(g), _silu(z)


def _epilogue_gate_last(*accs):
    return accs[:-1] + (_silu(accs[-1]),)


def _in_proj_kernel(*refs, n_groups, epilogue):
    h_ref, g_ref, sh_ref, sc_ref = refs[:4]
    w_refs = refs[4:4 + n_groups]
    o_refs = refs[4 + n_groups:-1]
    u_ref = refs[-1]

    @pl.when(pl.program_id(1) == 0)
    def _():
        x = h_ref[...]
        ms = jnp.mean(x * x, axis=-1, keepdims=True)
        y = x * lax.rsqrt(ms + EPS) * g_ref[...]
        u_ref[...] = (y * (1.0 + sc_ref[0]) + sh_ref[0]).astype(u_ref.dtype)

    u = u_ref[...]
    outs = epilogue(*[jnp.dot(u, w[...], preferred_element_type=F32) for w in w_refs])
    for o_ref, val in zip(o_refs, outs, strict=True):
        o_ref[...] = val.astype(o_ref.dtype)


IN_PROJ_ROWS = 512


def _in_proj(h, g, sh, sc, w, *, n_groups, n_out, epilogue, n_ctx=0, tm=IN_PROJ_ROWS, tn=512):
    bn, t, d = h.shape
    e = w.shape[1] // n_groups
    tm, tn = _tiles(t, e, tm, tn)
    assert n_ctx % tm == 0
    tiles_per_seq, ctx_tiles = t // tm, n_ctx // tm
    col_tiles = e // tn
    n_mod = sh.shape[0]

    def mod_row(i, j):
        return (jnp.where(i % tiles_per_seq < ctx_tiles, n_mod - 1, i // tiles_per_seq), 0, 0)

    w_specs = [pl.BlockSpec((d, tn), functools.partial(lambda i, j, k: (0, k * col_tiles + j), k=k))
               for k in range(n_groups)]
    outs = pl.pallas_call(
        functools.partial(_in_proj_kernel, n_groups=n_groups, epilogue=epilogue),
        grid=(bn * tiles_per_seq, col_tiles),
        in_specs=[
            pl.BlockSpec((tm, d), lambda i, j: (i, 0)),
            pl.BlockSpec((1, d), lambda i, j: (0, 0)),
            pl.BlockSpec((1, 1, d), mod_row),
            pl.BlockSpec((1, 1, d), mod_row),
        ] + w_specs,
        out_specs=[pl.BlockSpec((tm, tn), lambda i, j: (i, j))] * n_out,
        out_shape=[jax.ShapeDtypeStruct((bn * t, e), BF16)] * n_out,
        scratch_shapes=[pltpu.VMEM((tm, d), BF16)],
        compiler_params=_params(("arbitrary", "arbitrary")),
        name="in_proj",
    )(h.reshape(bn * t, d), g.reshape(1, d), sh.reshape(n_mod, 1, d),
      sc.reshape(n_mod, 1, d), *([w] * n_groups))
    return [o.reshape(bn, t, e) for o in outs]


def _out_matmul_kernel(y_ref, w_ref, h_ref, gt_ref, o_ref):
    acc = jnp.dot(y_ref[...], w_ref[...], preferred_element_type=F32)
    o_ref[...] = h_ref[...] + gt_ref[0] * acc


def _out_matmul_norm_kernel(y_ref, w_ref, h_ref, gt_ref, g_ref, o_ref):
    acc = jnp.dot(y_ref[...], w_ref[...], preferred_element_type=F32)
    x = h_ref[...] + gt_ref[0] * acc
    ms = jnp.mean(x * x, axis=-1, keepdims=True)
    o_ref[...] = x * lax.rsqrt(ms + EPS) * g_ref[...]


def _out_matmul(y, w, h, gt, *, final_g=None, tm=512, tn=1024):
    bn, t, e = y.shape
    d = w.shape[1]
    if final_g is not None:
        tm, tn = tm // 2, d
    tm, tn = _tiles(t, d, tm, tn)
    tiles_per_seq = t // tm
    in_specs = [
        pl.BlockSpec((tm, e), lambda i, j: (i, 0)),
        pl.BlockSpec((e, tn), lambda i, j: (0, j)) if final_g is None else _resident((e, d), 2),
        pl.BlockSpec((tm, tn), lambda i, j: (i, j)),
        pl.BlockSpec((1, 1, tn), lambda i, j: (i // tiles_per_seq, 0, j)),
    ]
    args = [y.reshape(bn * t, e), w, h.reshape(bn * t, d), gt.reshape(bn, 1, d)]
    if final_g is not None:
        in_specs.append(_resident((1, d), 2))
        args.append(final_g.reshape(1, d))
    out = pl.pallas_call(
        _out_matmul_kernel if final_g is None else _out_matmul_norm_kernel,
        grid=(bn * tiles_per_seq, d // tn),
        in_specs=in_specs,
        out_specs=pl.BlockSpec((tm, tn), lambda i, j: (i, j)),
        out_shape=jax.ShapeDtypeStruct((bn * t, d), F32),
        compiler_params=_params(("arbitrary", "arbitrary")),
        name="out_matmul",
    )(*args)
    return out.reshape(bn, t, d)


CONV_PAD = 16
LN_UNROLL = 8


def _conv_mid_kernel(y_ref, zs_ref, w_ref, b_ref, lg_ref, lb_ref, o_ref, xp_ref, xs_ref, cv_ref, *, seg, width):
    tb, e = y_ref.shape
    n_seg = tb // seg
    half = width // 2
    rows_s = xs_ref.shape[2]
    rb = SUBLANES
    zeros = jnp.zeros((CONV_PAD, e), F32)
    for s in range(n_seg):
        xp_ref[s, 0:CONV_PAD, :] = zeros
        xp_ref[s, CONV_PAD:CONV_PAD + seg, :] = y_ref[s * seg:(s + 1) * seg, :].astype(F32)
        xp_ref[s, CONV_PAD + seg:2 * CONV_PAD + seg, :] = zeros

    def lane_block(cb, carry):
        lanes = pl.ds(pl.multiple_of(cb * LANES, LANES), LANES)
        for sh in range(rb):
            for s in range(n_seg):
                xs_ref[sh, s] = xp_ref[s, sh:sh + rows_s, lanes]
        taps = [w_ref[k:k + 1, lanes] for k in range(width)]
        bias = b_ref[:, lanes]
        for s in range(n_seg):
            for r in range(seg // rb):
                acc = bias
                for k in range(width):
                    off = CONV_PAD - half + k
                    base = (off // rb) * rb + r * rb
                    acc = acc + taps[k] * xs_ref[off % rb, s, base:base + rb, :]
                cv_ref[s * seg + r * rb:s * seg + (r + 1) * rb, lanes] = acc
        return carry

    lax.fori_loop(0, e // LANES, lane_block, 0)

    def row_block(i, carry):
        rows = pl.ds(pl.multiple_of(i * rb, rb), rb)
        cv = cv_ref[rows, :]
        mu = jnp.mean(cv, axis=-1, keepdims=True)
        xc = cv - mu
        var = jnp.mean(xc * xc, axis=-1, keepdims=True)
        yn = xc * lax.rsqrt(var + EPS) * lg_ref[...] + lb_ref[...]
        o_ref[rows, :] = (_silu(yn) * zs_ref[rows, :].astype(F32)).astype(o_ref.dtype)
        return carry

    lax.fori_loop(0, tb // rb, row_block, 0, unroll=LN_UNROLL)


def _conv_mid(y, zs, dw_w, dw_b, ln_g, ln_b, *, seg, tb=256):
    bn, t, e = y.shape
    width = dw_w.shape[0]
    m = bn * t
    tb = min(tb, m)
    assert m % tb == 0 and tb % seg == 0 and t % seg == 0 and width // 2 <= CONV_PAD
    row = lambda a: a.reshape(1, e)
    tok = pl.BlockSpec((tb, e), lambda i: (i, 0))
    vec = pl.BlockSpec((1, e), lambda i: (0, 0))
    out = pl.pallas_call(
        functools.partial(_conv_mid_kernel, seg=seg, width=width),
        grid=(m // tb,),
        in_specs=[tok, tok, pl.BlockSpec((width, e), lambda i: (0, 0)), vec, vec, vec],
        out_specs=tok,
        out_shape=jax.ShapeDtypeStruct((m, e), BF16),
        scratch_shapes=[pltpu.VMEM((tb // seg, seg + 2 * CONV_PAD, e), F32),
                        pltpu.VMEM((SUBLANES, tb // seg, seg + 2 * CONV_PAD - SUBLANES, LANES), F32),
                        pltpu.VMEM((tb, e), F32)],
        compiler_params=_params(("arbitrary",)),
        name="conv_mid",
    )(y.reshape(m, e), zs.reshape(m, e), dw_w, row(dw_b), row(ln_g), row(ln_b))
    return out.reshape(bn, t, e)


CONV_DFT = 96
CONV_FREQS = CONV_DFT // 2 + 1
CONV_HALF = -(-CONV_FREQS // SUBLANES) * SUBLANES
CONV_LANE_UNROLL = 4


def _conv_dft_tables(seg, width, n_seg):
    f = np.arange(CONV_FREQS)
    th = 2 * np.pi * np.outer(f, np.arange(seg)) / CONV_DFT
    fwd = np.zeros((2 * CONV_HALF, seg))
    fwd[:CONV_FREQS] = np.cos(th)
    fwd[CONV_HALF:CONV_HALF + CONV_FREQS] = -np.sin(th)
    c = np.where((f == 0) | (f == CONV_DFT // 2), 1.0, 2.0) / CONV_DFT
    inv = np.zeros((seg, 2 * CONV_HALF))
    inv[:, :CONV_FREQS] = (c[:, None] * np.cos(th)).T
    inv[:, CONV_HALF:CONV_HALF + CONV_FREQS] = (-c[:, None] * np.sin(th)).T
    eye = np.eye(n_seg)
    thw = 2 * np.pi * np.outer(f, np.arange(width) - width // 2) / CONV_DFT
    wtab = np.zeros((2 * CONV_HALF, width))
    wtab[:CONV_FREQS] = np.cos(thw)
    wtab[CONV_HALF:CONV_HALF + CONV_FREQS] = np.sin(thw)
    bf = lambda a: jnp.asarray(a, F32).astype(BF16)
    return bf(np.kron(eye, fwd)), bf(np.kron(eye, inv)), jnp.asarray(wtab, F32)


def _conv_dft_kernel(y_ref, zs_ref, w_ref, b_ref, lg_ref, lb_ref, fwd_ref, inv_ref, wtab_ref, o_ref,
                     h_ref, cv_ref, *, n_seg):
    tb, e = y_ref.shape
    rb = SUBLANES

    @pl.when(pl.program_id(0) == 0)
    def _():
        h_ref[...] = jnp.dot(wtab_ref[...], w_ref[...], preferred_element_type=F32,
                             precision=lax.Precision.HIGHEST)

    def lane_block(cb, carry):
        lanes = pl.ds(pl.multiple_of(cb * MXU_TILE, MXU_TILE), MXU_TILE)
        x = jnp.dot(fwd_ref[...], y_ref[:, lanes], preferred_element_type=F32)
        hr = h_ref[0:CONV_HALF, lanes]
        hi = h_ref[CONV_HALF:2 * CONV_HALF, lanes]
        parts = []
        for s in range(n_seg):
            xr = x[s * 2 * CONV_HALF:s * 2 * CONV_HALF + CONV_HALF]
            xi = x[s * 2 * CONV_HALF + CONV_HALF:(s + 1) * 2 * CONV_HALF]
            parts += [xr * hr - xi * hi, xr * hi + xi * hr]
        yf = jnp.concatenate(parts, axis=0).astype(BF16)
        cv_ref[:, lanes] = jnp.dot(inv_ref[...], yf, preferred_element_type=F32) + b_ref[:, lanes]
        return carry

    lax.fori_loop(0, e // MXU_TILE, lane_block, 0, unroll=CONV_LANE_UNROLL)

    def row_block(i, carry):
        rows = pl.ds(pl.multiple_of(i * rb, rb), rb)
        cv = cv_ref[rows, :]
        mu = jnp.mean(cv, axis=-1, keepdims=True)
        xc = cv - mu
        var = jnp.mean(xc * xc, axis=-1, keepdims=True)
        yn = xc * lax.rsqrt(var + EPS) * lg_ref[...] + lb_ref[...]
        o_ref[rows, :] = (_silu(yn) * zs_ref[rows, :].astype(F32)).astype(o_ref.dtype)
        return carry

    lax.fori_loop(0, tb // rb, row_block, 0, unroll=LN_UNROLL)


def _conv_mid_dft(y, zs, dw_w, dw_b, ln_g, ln_b, *, seg, tb=256):
    bn, t, e = y.shape
    width = dw_w.shape[0]
    m = bn * t
    tb = min(tb, m)
    assert m % tb == 0 and tb % seg == 0 and t % seg == 0 and seg + width - 1 <= CONV_DFT
    n_seg = tb // seg
    fwd, inv, wtab = _conv_dft_tables(seg, width, n_seg)
    row = lambda a: a.reshape(1, e)
    tok = pl.BlockSpec((tb, e), lambda i: (i, 0))
    out = pl.pallas_call(
        functools.partial(_conv_dft_kernel, n_seg=n_seg),
        grid=(m // tb,),
        in_specs=[tok, tok, _resident((width, e), 1), _resident((1, e), 1), _resident((1, e), 1),
                  _resident((1, e), 1), _resident(fwd.shape, 1), _resident(inv.shape, 1),
                  _resident(wtab.shape, 1)],
        out_specs=tok,
        out_shape=jax.ShapeDtypeStruct((m, e), BF16),
        scratch_shapes=[pltpu.VMEM((2 * CONV_HALF, e), F32), pltpu.VMEM((tb, e), F32)],
        compiler_params=_params(("arbitrary",)),
        name="conv_mid_dft",
    )(y.reshape(m, e), zs.reshape(m, e), dw_w, row(dw_b), row(ln_g), row(ln_b), fwd, inv, wtab)
    return out.reshape(bn, t, e)


STAGE_PAD = 8


def _expand_block_diag(w):
    g, bi, bo = w.shape
    per = MXU_TILE // bi
    wt = w.reshape(g // per, per, bi, bo)
    eye = jnp.eye(per, dtype=w.dtype)
    return jnp.einsum('tgio,gh->tgiho', wt, eye).reshape(g // per, per * bi, per * bo)


def _mlstm_pre_kernel(xm_ref, prev_ref, next_ref, cw_ref, cb_ref, wq_ref, wk_ref, wkt_ref, wv_ref, wo_ref,
                      bo_ref, wg_ref, wgt_ref, bg_ref, bgt_ref,
                      q_ref, k_ref, kt_ref, v_ref, o_ref, xc_ref, gc_ref, gr_ref, stage_ref,
                      *, tiles_per_seq, ctx_tiles, n_heads):
    tb, e = xm_ref.shape
    q_scale = (e // n_heads) ** -0.5
    t = pl.program_id(0) % tiles_per_seq
    has_prev = jnp.logical_and(t != 0, t != ctx_tiles).astype(F32)
    has_next = jnp.logical_and(t != ctx_tiles - 1, t != tiles_per_seq - 1).astype(F32)
    gc = jnp.zeros(gc_ref.shape, F32) + bg_ref[...]
    gr = jnp.zeros(gr_ref.shape, F32) + bgt_ref[...]
    nt = (((1,), (1,)), ((), ()))
    for j in range(e // MXU_TILE):
        lanes = slice(j * MXU_TILE, (j + 1) * MXU_TILE)
        xm = xm_ref[:, lanes]
        stage_ref[STAGE_PAD - 1:STAGE_PAD, :] = \
            prev_ref[:, lanes].astype(F32)[BF16_ROWS - 1:BF16_ROWS, :] * has_prev
        stage_ref[STAGE_PAD:STAGE_PAD + tb, :] = xm.astype(F32)
        stage_ref[STAGE_PAD + tb:STAGE_PAD + tb + 1, :] = next_ref[:, lanes].astype(F32)[0:1, :] * has_next
        cw = cw_ref[:, lanes]
        pre = (cb_ref[:, lanes] + cw[0:1] * stage_ref[STAGE_PAD - 1:STAGE_PAD - 1 + tb, :]
               + cw[1:2] * stage_ref[STAGE_PAD:STAGE_PAD + tb, :]
               + cw[2:3] * stage_ref[STAGE_PAD + 1:STAGE_PAD + 1 + tb, :])
        xcb = _silu(pre).astype(BF16)
        xc_ref[:, lanes] = xcb
        q = jnp.dot(xcb, wq_ref[j], preferred_element_type=F32)
        k = jnp.dot(xcb, wk_ref[j], preferred_element_type=F32)
        v = jnp.dot(xm, wv_ref[j], preferred_element_type=F32)
        o = jnp.dot(xcb, wo_ref[j], preferred_element_type=F32) + bo_ref[:, lanes]
        kt_ref[lanes, :] = lax.dot_general(wkt_ref[j], xcb, nt, preferred_element_type=F32).astype(BF16)
        qb, kb, vb = q.astype(BF16), k.astype(BF16), v.astype(BF16)
        q_ref[:, lanes] = (q * q_scale).astype(BF16)
        k_ref[:, lanes] = kb
        v_ref[:, lanes] = vb
        o_ref[:, lanes] = _sigmoid(o).astype(BF16)
        for i, a in enumerate((qb, kb, vb)):
            gc = gc + jnp.dot(a, wg_ref[i, lanes, :], preferred_element_type=F32)
            gr = gr + lax.dot_general(wgt_ref[i, :, lanes], a, nt, preferred_element_type=F32)
    col = lax.broadcasted_iota(jnp.int32, gc.shape, 1)
    row = lax.broadcasted_iota(jnp.int32, gr.shape, 0)
    gc_ref[...] = jnp.where((col // n_heads) % 2 == 1, _log_sigmoid(gc), gc)
    gr_ref[...] = jnp.where((row // n_heads) % 2 == 1, _log_sigmoid(gr), gr)


def _mlstm_pre(xm, conv_w, conv_b, w_q, w_k, w_v, w_o, b_o, w_gates, b_gates, *, lead, n_ctx, n_heads, tb=256):
    bsz, t_all, e = xm.shape
    t = t_all - lead
    assert t % tb == 0 and n_ctx % tb == 0 and lead % tb == 0
    tiles_per_seq, ctx_tiles = t // tb, n_ctx // tb
    tiles_all, lead_tiles = t_all // tb, lead // tb
    m = bsz * t
    n_g = w_gates.shape[1]
    hpt = tb // BF16_ROWS
    n_halo = bsz * t_all // BF16_ROWS

    def src_tile(i):
        return (i // tiles_per_seq) * tiles_all + lead_tiles + i % tiles_per_seq
    bd = lambda w: _expand_block_diag(w).astype(BF16)
    wq, wk, wv, wo = bd(w_q), bd(w_k), bd(w_v), bd(w_o)
    wkt = jnp.swapaxes(wk, 1, 2)
    wg = w_gates.reshape(3, e, n_g).astype(BF16)
    wgt = jnp.swapaxes(wg, 1, 2)
    tok = pl.BlockSpec((tb, e), lambda i: (i, 0))
    row = lambda a: a.reshape(1, -1)
    consts = [conv_w, row(conv_b), wq, wk, wkt, wv, wo, row(b_o), wg, wgt, row(b_gates), b_gates.reshape(-1, 1)]
    sds = jax.ShapeDtypeStruct
    xm2 = xm.reshape(bsz * t_all, e)
    outs = pl.pallas_call(
        functools.partial(_mlstm_pre_kernel, tiles_per_seq=tiles_per_seq, ctx_tiles=ctx_tiles,
                          n_heads=n_heads),
        grid=(m // tb,),
        in_specs=[pl.BlockSpec((tb, e), lambda i: (src_tile(i), 0)),
                  pl.BlockSpec((BF16_ROWS, e), lambda i: (jnp.maximum(src_tile(i) * hpt - 1, 0), 0)),
                  pl.BlockSpec((BF16_ROWS, e), lambda i: (jnp.minimum((src_tile(i) + 1) * hpt, n_halo - 1), 0))]
                 + [_resident(a.shape, 1) for a in consts],
        out_specs=[tok, tok,
                   pl.BlockSpec((None, e, tb), lambda i: (i // tiles_per_seq, 0, i % tiles_per_seq)),
                   tok, tok, tok,
                   pl.BlockSpec((tb, n_g), lambda i: (i, 0)),
                   pl.BlockSpec((None, n_g, tb), lambda i: (i // tiles_per_seq, 0, i % tiles_per_seq))],
        out_shape=[sds((m, e), BF16), sds((m, e), BF16), sds((bsz, e, t), BF16), sds((m, e), BF16),
                   sds((m, e), BF16), sds((m, e), BF16), sds((m, n_g), F32), sds((bsz, n_g, t), F32)],
        scratch_shapes=[pltpu.VMEM((tb + 2 * STAGE_PAD, MXU_TILE), F32)],
        compiler_params=_params(("arbitrary",)),
        name="mlstm_pre",
    )(xm2, xm2, xm2, *consts)
    q, k, kt, v, o, xc, gc, gr = outs
    r3 = lambda a: a.reshape(bsz, t, -1)
    return r3(q), r3(k), kt, r3(v), r3(o), r3(xc), r3(gc), gr


SCAN_CHUNK = 256


def _log_sigmoid(x):
    return jnp.minimum(x, 0.0) - jnp.log1p(jnp.exp(-jnp.abs(x)))


SCAN_HEADS_PER_STEP = 4


def _split_bf16(x):
    hi = x.astype(BF16)
    return hi, (x - hi.astype(F32)).astype(BF16)


def _mlstm_scan_kernel(q_ref, k_ref, kt_ref, v_ref, gc_ref, gr_ref, o_ref,
                       c_ref, cb_ref, n_ref, m_ref, *, n_heads, heads_per_step):
    direction = pl.program_id(0)

    @pl.when(pl.program_id(3) == 0)
    def _():
        c_ref[...] = jnp.zeros_like(c_ref)
        cb_ref[...] = jnp.zeros_like(cb_ref)
        n_ref[...] = jnp.zeros_like(n_ref)
        m_ref[...] = jnp.full_like(m_ref, -1e30)

    L = q_ref.shape[1]
    dh = q_ref.shape[2] // heads_per_step
    n_g = gc_ref.shape[2]
    gc = gc_ref[0]
    gr = gr_ref[0]
    lane = lax.broadcasted_iota(jnp.int32, (L, n_g), 1)
    sub = lax.broadcasted_iota(jnp.int32, (n_g, L), 0)
    sign = 1 - 2 * direction
    diff = (lax.broadcasted_iota(jnp.int32, (L, L), 0)
            - lax.broadcasted_iota(jnp.int32, (L, L), 1)) * sign
    causal = diff >= 0
    anti = diff <= 0

    for hh in range(heads_per_step):
        lanes = slice(hh * dh, (hh + 1) * dh)
        head = pl.program_id(2) * heads_per_step + hh
        col_i = direction * (2 * n_heads) + head
        col_f = col_i + n_heads
        lf_col = jnp.sum(jnp.where(lane == col_f, gc, 0.0), axis=1, keepdims=True)
        li_row = jnp.sum(jnp.where(sub == col_i, gr, 0.0), axis=0, keepdims=True)
        lf_row = jnp.sum(jnp.where(sub == col_f, gr, 0.0), axis=0, keepdims=True)
        b_col = jnp.sum(jnp.where(causal, lf_row, 0.0), axis=1, keepdims=True)
        b_row = jnp.sum(jnp.where(anti, lf_col, 0.0), axis=0, keepdims=True)
        total = jnp.sum(lf_row, axis=1, keepdims=True)

        m = m_ref[hh]
        log_d = jnp.where(causal, b_col - b_row + li_row, -jnp.inf)
        log_inter = b_col + m
        m_t = jnp.maximum(log_inter, jnp.max(log_d, axis=1, keepdims=True))
        dmat = jnp.exp(log_d - m_t)
        a = jnp.exp(log_inter - m_t)

        q = q_ref[0, :, lanes]
        kt = kt_ref[0, lanes, :]
        v = v_ref[0, :, lanes]
        s = jnp.dot(q, kt, preferred_element_type=F32) * dmat
        num = a * jnp.dot(q, cb_ref[hh], preferred_element_type=F32) \
            + jnp.dot(s.astype(BF16), v, preferred_element_type=F32)
        n_hi, n_lo = _split_bf16(n_ref[hh])
        qn2 = lax.dot_general(q, jnp.concatenate([n_hi, n_lo], axis=0), (((1,), (1,)), ((), ())),
                              preferred_element_type=F32)
        qn = qn2[:, 0:1] + qn2[:, SUBLANES:SUBLANES + 1]
        den = a * qn + jnp.sum(s, axis=1, keepdims=True)
        o_ref[0, 0, :, lanes] = (num / jnp.maximum(jnp.abs(den), jnp.exp(-m_t))).astype(o_ref.dtype)

        log_w_row = total - b_row + li_row
        m_new = jnp.maximum(total + m, jnp.max(log_w_row, axis=1, keepdims=True))
        decay = jnp.exp(total + m - m_new)
        w_hi, w_lo = _split_bf16(jnp.exp(log_w_row - m_new))
        kwt = kt * w_hi
        c_new = decay * c_ref[hh] + jnp.dot(kwt, v, preferred_element_type=F32)
        c_ref[hh] = c_new
        cb_ref[hh] = c_new.astype(BF16)
        w2 = jnp.concatenate([jnp.broadcast_to(w_hi, (SUBLANES, L)), jnp.broadcast_to(w_lo, (SUBLANES, L))], axis=0)
        n_add = jnp.dot(w2, k_ref[0, :, lanes], preferred_element_type=F32)
        n_ref[hh] = decay * n_ref[hh] + n_add[0:SUBLANES] + n_add[SUBLANES:2 * SUBLANES]
        m_ref[hh] = m_new


def _mlstm_scan(q, k, kt, v, gc, gr, *, n_heads, n_ctx, chunk=SCAN_CHUNK, heads_per_step=SCAN_HEADS_PER_STEP):
    bsz, t, e = q.shape
    dh = e // n_heads
    hp = heads_per_step
    L = chunk
    assert t % L == 0 and n_ctx % L == 0 and n_heads % hp == 0
    nc, nc_ctx = t // L, n_ctx // L

    def cidx(d, c):
        rev = jnp.where(c < nc_ctx, nc_ctx - 1 - c, nc - 1 - c + nc_ctx)
        return jnp.where(d == 0, c, rev)

    tok = pl.BlockSpec((1, L, hp * dh), lambda d, b, h, c: (b, cidx(d, c), h))
    return pl.pallas_call(
        functools.partial(_mlstm_scan_kernel, n_heads=n_heads, heads_per_step=hp),
        grid=(2, bsz, n_heads // hp, nc),
        in_specs=[
            tok, tok,
            pl.BlockSpec((1, hp * dh, L), lambda d, b, h, c: (b, h, cidx(d, c))),
            tok,
            pl.BlockSpec((1, L, 4 * n_heads), lambda d, b, h, c: (b, cidx(d, c), 0)),
            pl.BlockSpec((1, 4 * n_heads, L), lambda d, b, h, c: (b, 0, cidx(d, c))),
        ],
        out_specs=pl.BlockSpec((1, 1, L, hp * dh), lambda d, b, h, c: (d, b, cidx(d, c), h)),
        out_shape=jax.ShapeDtypeStruct((2, bsz, t, e), BF16),
        scratch_shapes=[pltpu.VMEM((hp, dh, dh), F32), pltpu.VMEM((hp, dh, dh), BF16),
                        pltpu.VMEM((hp, SUBLANES, dh), F32), pltpu.VMEM((hp, 1, 1), F32)],
        compiler_params=_params(("arbitrary",) * 4),
        name="mlstm_scan",
    )(q, k, kt, v, gc, gr)


def _mlstm_out_kernel(hf_ref, hb_ref, o_ref, xc_ref, zs_ref, mh_ref, skip_ref, w_ref, res_ref, gt_ref,
                      out_ref, y_ref, *, n_heads):
    tm, e = o_ref.shape
    dh = e // n_heads
    for hd in range(n_heads):
        lanes = slice(hd * dh, (hd + 1) * dh)
        h = hf_ref[:, lanes].astype(F32) + hb_ref[:, lanes].astype(F32)
        mu = jnp.mean(h, axis=-1, keepdims=True)
        hc = h - mu
        var = jnp.mean(hc * hc, axis=-1, keepdims=True)
        hn = hc * lax.rsqrt(var + EPS) * mh_ref[:, lanes]
        y = (o_ref[:, lanes].astype(F32) * hn + skip_ref[:, lanes] * xc_ref[:, lanes].astype(F32)) \
            * zs_ref[:, lanes].astype(F32)
        y_ref[:, lanes] = y.astype(BF16)
    acc = jnp.dot(y_ref[...], w_ref[...], preferred_element_type=F32)
    out_ref[...] = res_ref[...] + gt_ref[0] * acc


def _mlstm_out(hs, o, xc, zs, mh_g, skip, w_out, h_lat, gt, *, lead, n_ctx, n_heads, tm=256):
    _, bsz, t, e = hs.shape
    n = t - n_ctx
    d = w_out.shape[1]
    assert n % tm == 0 and n_ctx % tm == 0 and lead % tm == 0
    tiles, off = n // tm, n_ctx // tm
    zs_off = off + lead // tm
    tok = pl.BlockSpec((None, tm, e), lambda i: (i // tiles, off + i % tiles, 0))
    zs_tok = pl.BlockSpec((None, tm, e), lambda i: (i // tiles, zs_off + i % tiles, 0))
    row = lambda a: a.reshape(1, -1)
    out = pl.pallas_call(
        functools.partial(_mlstm_out_kernel, n_heads=n_heads),
        grid=(bsz * tiles,),
        in_specs=[pl.BlockSpec((None, None, tm, e), lambda i: (0, i // tiles, off + i % tiles, 0)),
                  pl.BlockSpec((None, None, tm, e), lambda i: (1, i // tiles, off + i % tiles, 0)),
                  tok, tok, zs_tok, _resident((1, e), 1), _resident((1, e), 1), _resident((e, d), 1),
                  pl.BlockSpec((tm, d), lambda i: (i, 0)),
                  pl.BlockSpec((1, 1, d), lambda i: (i // tiles, 0, 0))],
        out_specs=pl.BlockSpec((tm, d), lambda i: (i, 0)),
        out_shape=jax.ShapeDtypeStruct((bsz * n, d), F32),
        scratch_shapes=[pltpu.VMEM((tm, e), BF16)],
        compiler_params=_params(("arbitrary",)),
        name="mlstm_out",
    )(hs, hs, o, xc, zs, row(mh_g), row(skip), w_out, h_lat.reshape(bsz * n, d), gt.reshape(bsz, 1, d))
    return out.reshape(bsz, n, d)


def _filter_features(seq):
    r = np.arange(2 * seq)
    lag = np.where(r < seq, r, 2 * seq - r).clip(0, seq - 1).astype(np.float64)
    t = lag / (seq - 1)
    bands = (HYENA_EMB_DIM - 1) // 2
    ang = 2.0 * math.pi * lag / seq
    fr = np.linspace(1e-4, bands - 1, bands)
    feat = np.concatenate([t[:, None], np.cos(fr[None] * ang[:, None]), -np.sin(fr[None] * ang[:, None])], axis=-1)
    return jnp.asarray(feat, F32), jnp.asarray(t[:, None], F32)


def _filter_kernel(feat_ref, t_ref, w1_ref, b1_ref, f1_ref, w2_ref, b2_ref, f2_ref, w3_ref, b3_ref, dl_ref,
                   o_ref, *, seq):
    tr = o_ref.shape[0]
    h = jnp.sin(f1_ref[...] * (_dot_bf16(feat_ref[...], w1_ref[...]) + b1_ref[...]))
    h = jnp.sin(f2_ref[...] * (_dot_bf16(h, w2_ref[...]) + b2_ref[...]))
    h = _dot_bf16(h, w3_ref[...]) + b3_ref[...]
    out = h * jnp.exp(-t_ref[...] * dl_ref[...])
    row = pl.program_id(0) * tr + lax.broadcasted_iota(jnp.int32, (tr, 1), 0)
    o_ref[...] = jnp.where(row == seq, 0.0, out)


def _hyena_filter(seq, f_w1, f_b1, f_freq1, f_w2, f_b2, f_freq2, f_w3, f_b3, *, tr=512, tn=1024):
    e = f_w3.shape[1] // 2
    feat, t = _filter_features(seq)
    lo = math.log(HYENA_DECAY_TARGET) / HYENA_FAST_DECAY
    hi = math.log(HYENA_DECAY_TARGET) / HYENA_SLOW_DECAY
    deltas = jnp.asarray(np.abs(np.linspace(lo, hi, e)), F32).reshape(1, e)
    fdim = f_w1.shape[1]
    row_tiles, col_tiles = 2 * seq // tr, e // tn
    half = lambda i: i // (row_tiles // 2)
    row = lambda a: a.reshape(1, -1)
    small = lambda shape: pl.BlockSpec(shape, lambda i, j: (0, 0))
    return pl.pallas_call(
        functools.partial(_filter_kernel, seq=seq),
        grid=(row_tiles, col_tiles),
        in_specs=[pl.BlockSpec((tr, HYENA_EMB_DIM), lambda i, j: (i, 0)),
                  pl.BlockSpec((tr, 1), lambda i, j: (i, 0)),
                  small((HYENA_EMB_DIM, fdim)), small((1, fdim)), small((1, fdim)),
                  small((fdim, fdim)), small((1, fdim)), small((1, fdim)),
                  pl.BlockSpec((fdim, tn), lambda i, j: (0, half(i) * col_tiles + j)),
                  pl.BlockSpec((1, tn), lambda i, j: (0, half(i) * col_tiles + j)),
                  pl.BlockSpec((1, tn), lambda i, j: (0, j))],
        out_specs=pl.BlockSpec((tr, tn), lambda i, j: (i, j)),
        out_shape=jax.ShapeDtypeStruct((2 * seq, e), F32),
        compiler_params=_params(("arbitrary", "arbitrary")),
        name="hyena_filter",
    )(feat, t, f_w1, row(f_b1), row(f_freq1), f_w2, row(f_b2), row(f_freq2), f_w3, row(f_b3), deltas)


FFT_N1 = 64
FFT_N2 = 128
FFT_N = FFT_N1 * FFT_N2
FFT_K1 = FFT_N1 // 2 + 1
SHORT_CHUNK = 256
K1_UNROLL = 11
OUTER_UNROLL = 4
_REAL_ONLY_COLS = (1, 2 * (FFT_K1 - 1) + 1)


def _dft_tables():
    n1h = FFT_N1 // 2
    k1 = np.arange(FFT_K1)
    eye = np.eye(SUBLANES)

    def outer_fwd(n_rows):
        th = 2 * np.pi * np.outer(k1, np.arange(n_rows)) / FFT_N1
        f = np.stack([np.cos(th), -np.sin(th)], axis=1).reshape(2 * FFT_K1, n_rows)
        return np.kron(f, eye)

    n2 = np.arange(FFT_N2)
    k2 = np.arange(FFT_N2)
    ph = 2 * np.pi * n2[None, None, :] * (k1[:, None, None] + FFT_N1 * k2[None, :, None]) / FFT_N
    gr, gi = np.cos(ph), -np.sin(ph)
    g = np.concatenate([np.concatenate([gr, -gi], axis=2), np.concatenate([gi, gr], axis=2)], axis=1)
    ginv = np.swapaxes(g, 1, 2)
    th2 = 2 * np.pi * np.outer(np.arange(n1h), k1) / FFT_N1
    c = np.where((k1 == 0) | (k1 == n1h), 1.0, 2.0)[None, :] / FFT_N
    finv = np.stack([c * np.cos(th2), -c * np.sin(th2)], axis=2).reshape(n1h, 2 * FFT_K1)
    finv = finv[:, [col for col in range(2 * FFT_K1) if col not in _REAL_ONLY_COLS]]
    bf = lambda a: jnp.asarray(a, F32).astype(BF16)
    return dict(s1_half=bf(outer_fwd(n1h)), s1_full=bf(outer_fwd(FFT_N1)), g=bf(g), ginv=bf(ginv),
                i2=bf(np.kron(finv, eye)))


def _outer_forward(src_ref, s1_ref, dst_ref, n1_rows):
    def body(n2h, carry):
        r0 = pl.multiple_of(n2h * SUBLANES, SUBLANES)
        tiles = [src_ref[pl.ds(n1 * FFT_N2 + r0, SUBLANES), :] for n1 in range(n1_rows)]
        rhs = jnp.concatenate(tiles, axis=0).astype(BF16)
        out = jnp.dot(s1_ref[...], rhs, preferred_element_type=F32)
        for k1 in range(FFT_K1):
            for ri in range(2):
                row = (2 * k1 + ri) * SUBLANES
                dst_ref[k1, pl.ds(ri * FFT_N2 + r0, SUBLANES), :] = out[row:row + SUBLANES, :]
        return carry
    lax.fori_loop(0, FFT_N2 // SUBLANES, body, 0, unroll=OUTER_UNROLL)


def _filter_spectrum_kernel(k_ref, s1_ref, g_ref, kf_ref, a_ref):
    _outer_forward(k_ref, s1_ref, a_ref, FFT_N1)

    def body(k1, carry):
        kf_ref[k1] = jnp.dot(g_ref[k1], a_ref[k1].astype(BF16), preferred_element_type=F32)
        return carry
    lax.fori_loop(0, FFT_K1, body, 0)


def _filter_spectrum(kfull, tables, *, cb=MXU_TILE):
    e = kfull.shape[1]
    cb = min(cb, e)
    return pl.pallas_call(
        _filter_spectrum_kernel,
        grid=(e // cb,),
        in_specs=[pl.BlockSpec((FFT_N, cb), lambda c: (0, c)),
                  _resident(tables["s1_full"].shape, 1), _resident(tables["g"].shape, 1)],
        out_specs=pl.BlockSpec((FFT_K1, 2 * FFT_N2, cb), lambda c: (0, 0, c)),
        out_shape=jax.ShapeDtypeStruct((FFT_K1, 2 * FFT_N2, e), F32),
        scratch_shapes=[pltpu.VMEM((FFT_K1, 2 * FFT_N2, cb), F32)],
        compiler_params=_params(("arbitrary",)),
        name="filter_spectrum",
    )(kfull, tables["s1_full"], tables["g"])


def _short_conv(x_ref, cw_ref, cb_ref, group, c, stage_ref, n_rows):
    r0 = pl.multiple_of(c * SHORT_CHUNK, SHORT_CHUNK)
    prev0 = pl.multiple_of(jnp.maximum(r0 - BF16_ROWS, 0), BF16_ROWS)
    next0 = pl.multiple_of(jnp.minimum(r0 + SHORT_CHUNK, n_rows - BF16_ROWS), BF16_ROWS)
    prev = x_ref[pl.ds(prev0, BF16_ROWS), :].astype(F32)[BF16_ROWS - 1:BF16_ROWS, :] * jnp.where(c > 0, 1.0, 0.0)
    nxt = x_ref[pl.ds(next0, BF16_ROWS), :].astype(F32)[0:1, :] * jnp.where(r0 + SHORT_CHUNK < n_rows, 1.0, 0.0)
    stage_ref[STAGE_PAD - 1:STAGE_PAD, :] = prev
    stage_ref[STAGE_PAD:STAGE_PAD + SHORT_CHUNK, :] = x_ref[pl.ds(r0, SHORT_CHUNK), :].astype(F32)
    stage_ref[STAGE_PAD + SHORT_CHUNK:STAGE_PAD + SHORT_CHUNK + 1, :] = nxt
    w = cw_ref[group]
    return (cb_ref[group]
            + w[0:1] * stage_ref[STAGE_PAD - 1:STAGE_PAD - 1 + SHORT_CHUNK, :]
            + w[1:2] * stage_ref[STAGE_PAD:STAGE_PAD + SHORT_CHUNK, :]
            + w[2:3] * stage_ref[STAGE_PAD + 1:STAGE_PAD + 1 + SHORT_CHUNK, :])


def _long_conv_kernel(x0_ref, x1_ref, v_ref, zs_ref, cw_ref, cb_ref, hb_ref, kf_ref,
                      s1_ref, g_ref, ginv_ref, i2_ref, o_ref, w_ref, ab_ref, stage_ref):
    n_rows = x1_ref.shape[0]
    n_chunks = n_rows // SHORT_CHUNK

    def make_w(c, carry):
        rows = pl.ds(pl.multiple_of(c * SHORT_CHUNK, SHORT_CHUNK), SHORT_CHUNK)
        x1c = _short_conv(x1_ref, cw_ref, cb_ref, 1, c, stage_ref, n_rows)
        vc = _short_conv(v_ref, cw_ref, cb_ref, 2, c, stage_ref, n_rows)
        w_ref[rows, :] = x1c * vc
        return carry
    lax.fori_loop(0, n_chunks, make_w, 0)

    _outer_forward(w_ref, s1_ref, ab_ref, FFT_N1 // 2)

    def per_k1(k1, carry):
        x = jnp.dot(g_ref[k1], ab_ref[k1].astype(BF16), preferred_element_type=F32)
        kf = kf_ref[k1]
        xr, xi = x[:FFT_N2], x[FFT_N2:]
        kr, ki = kf[:FFT_N2], kf[FFT_N2:]
        y = jnp.concatenate([xr * kr - xi * ki, xr * ki + xi * kr], axis=0).astype(BF16)
        ab_ref[k1] = jnp.dot(ginv_ref[k1], y, preferred_element_type=F32)
        return carry
    lax.fori_loop(0, FFT_K1, per_k1, 0, unroll=K1_UNROLL)

    def outer_inverse(n2h, carry):
        r0 = pl.multiple_of(n2h * SUBLANES, SUBLANES)
        tiles = [ab_ref[k1, pl.ds(ri * FFT_N2 + r0, SUBLANES), :] for k1 in range(FFT_K1) for ri in range(2)
                 if 2 * k1 + ri not in _REAL_ONLY_COLS]
        rhs = jnp.concatenate(tiles, axis=0).astype(BF16)
        out = jnp.dot(i2_ref[...], rhs, preferred_element_type=F32)
        for n1 in range(FFT_N1 // 2):
            rows = pl.ds(n1 * FFT_N2 + r0, SUBLANES)
            w_ref[rows, :] = out[n1 * SUBLANES:(n1 + 1) * SUBLANES, :] + hb_ref[...] * w_ref[rows, :]
        return carry
    lax.fori_loop(0, FFT_N2 // SUBLANES, outer_inverse, 0, unroll=OUTER_UNROLL)

    def finish(c, carry):
        rows = pl.ds(pl.multiple_of(c * SHORT_CHUNK, SHORT_CHUNK), SHORT_CHUNK)
        x0c = _short_conv(x0_ref, cw_ref, cb_ref, 0, c, stage_ref, n_rows)
        o_ref[rows, :] = (x0c * w_ref[rows, :] * zs_ref[rows, :].astype(F32)).astype(o_ref.dtype)
        return carry
    lax.fori_loop(0, n_chunks, finish, 0)


def _long_conv(x0, x1, v, zs, conv_w, conv_b, h_bias, kf, tables, *, cb=MXU_TILE):
    bsz, seq, e = x0.shape
    assert seq == FFT_N // 2
    cb = min(cb, e)
    cw = conv_w.reshape(3, 3, e).transpose(1, 0, 2)
    cbias = conv_b.reshape(3, 1, e)
    tok = pl.BlockSpec((None, seq, cb), lambda c, b: (b, 0, c))
    consts = [tables["s1_half"], tables["g"], tables["ginv"], tables["i2"]]
    return pl.pallas_call(
        _long_conv_kernel,
        grid=(e // cb, bsz),
        in_specs=[tok, tok, tok, tok,
                  pl.BlockSpec((3, 3, cb), lambda c, b: (0, 0, c)),
                  pl.BlockSpec((3, 1, cb), lambda c, b: (0, 0, c)),
                  pl.BlockSpec((1, cb), lambda c, b: (0, c)),
                  pl.BlockSpec((FFT_K1, 2 * FFT_N2, cb), lambda c, b: (0, 0, c), pipeline_mode=pl.Buffered(1))]
                 + [_resident(a.shape, 2) for a in consts],
        out_specs=tok,
        out_shape=jax.ShapeDtypeStruct((bsz, seq, e), BF16),
        scratch_shapes=[pltpu.VMEM((seq, cb), F32),
                        pltpu.VMEM((FFT_K1, 2 * FFT_N2, cb), F32),
                        pltpu.VMEM((SHORT_CHUNK + 2 * STAGE_PAD, cb), F32)],
        compiler_params=_params(("arbitrary", "arbitrary")),
        name="long_conv",
    )(x0, x1, v, zs, cw, cbias, h_bias.reshape(1, e), kf, *consts)


def _final_norm_kernel(h_ref, g_ref, o_ref):
    x = h_ref[...]
    ms = jnp.mean(x * x, axis=-1, keepdims=True)
    o_ref[...] = x * lax.rsqrt(ms + EPS) * g_ref[...]


def _final_norm(h, g, *, tm=1024):
    bn, t, d = h.shape
    m = bn * t
    tm = min(tm, m)
    out = pl.pallas_call(
        _final_norm_kernel,
        grid=(m // tm,),
        in_specs=[pl.BlockSpec((tm, d), lambda i: (i, 0)),
                  pl.BlockSpec((1, d), lambda i: (0, 0))],
        out_specs=pl.BlockSpec((tm, d), lambda i: (i, 0)),
        out_shape=jax.ShapeDtypeStruct((m, d), F32),
        compiler_params=_params(("arbitrary",)),
        name="final_norm",
    )(h.reshape(m, d), g.reshape(1, d))
    return out.reshape(bn, t, d)


def kernel(x, c, ctx, c_ctx, norm_g, ada_w, ada_b, final_g, cv_w_in, cv_dw_w, cv_dw_b, cv_ln_g, cv_ln_b, cv_w_out, ml_w_in, ml_conv_w, ml_conv_b, ml_w_q, ml_w_k, ml_w_v, ml_w_o, ml_b_o, ml_w_gates, ml_b_gates, ml_mh_g, ml_skip, ml_w_out, hy_w_in, hy_conv_w, hy_conv_b, hy_f_w1, hy_f_b1, hy_f_freq1, hy_f_w2, hy_f_b2, hy_f_freq2, hy_f_w3, hy_f_b3, hy_h_bias, hy_w_out):
    depth = norm_g.shape[0]
    bsz, seq, d = x.shape
    n_ctx = ctx.shape[1]
    readers = [i for i in range(depth) if i % N_MIXERS == 1]
    last_reader = readers[-1] if readers else -1

    cond_rows = -(-(bsz + 1) // SUBLANES) * SUBLANES
    cond = jnp.concatenate([c, c_ctx[None], jnp.zeros((cond_rows - bsz - 1, d), F32)], axis=0)
    ada = _ada_params(cond, ada_w, ada_b)

    h_lat, h_ctx = x, ctx
    norm_fused = (depth - 1) % N_MIXERS != 1
    for i in range(depth):
        kind, j = i % N_MIXERS, i // N_MIXERS
        close_g = final_g if (norm_fused and i == depth - 1) else None
        ctx_in = i <= last_reader
        ctx_out = i < last_reader
        sh, sc, gt = jnp.split(ada[i, :bsz], 3, axis=-1)
        sh_c, sc_c, gt_c = (jnp.broadcast_to(a, (bsz, d)) for a in jnp.split(ada[i, bsz:bsz + 1], 3, axis=-1))
        if kind == 0:
            w_in = cv_w_in[j].astype(BF16)
            w_out = cv_w_out[j].astype(BF16)
            streams = [(h_lat, sh, sc, gt, GRID_W, _conv_mid_dft)]
            if ctx_out:
                streams.append((h_ctx, sh_c, sc_c, gt_c, n_ctx, _conv_mid))
            new = []
            for h, s_h, s_c, g_t, seg, conv_mid in streams:
                y, zs = _in_proj(h, norm_g[i], s_h, s_c, w_in, n_groups=3, n_out=2, epilogue=_epilogue_conv)
                yb = conv_mid(y, zs, cv_dw_w[j], cv_dw_b[j], cv_ln_g[j], cv_ln_b[j], seg=seg)
                new.append(_out_matmul(yb, w_out, h, g_t, final_g=close_g if h is h_lat else None))
            h_lat = new[0]
            if ctx_out:
                h_ctx = new[1]
        elif kind == 1:
            assert ctx_in and not ctx_out
            w_in = ml_w_in[j].astype(BF16)
            w_out = ml_w_out[j].astype(BF16)
            lead = (-n_ctx) % IN_PROJ_ROWS
            h_all = jnp.concatenate([jnp.zeros((bsz, lead, d), F32), h_ctx, h_lat], axis=1)
            sh_all = jnp.concatenate([sh, sh_c[:1]], axis=0)
            sc_all = jnp.concatenate([sc, sc_c[:1]], axis=0)
            xm, zs = _in_proj(h_all, norm_g[i], sh_all, sc_all, w_in, n_groups=2, n_out=2,
                              epilogue=_epilogue_gate_last, n_ctx=lead + n_ctx, tm=IN_PROJ_ROWS, tn=1024)
            q, k, kt, v, o, xc, gc, gr = _mlstm_pre(
                xm, ml_conv_w[j], ml_conv_b[j], ml_w_q[j], ml_w_k[j], ml_w_v[j], ml_w_o[j], ml_b_o[j],
                ml_w_gates[j], ml_b_gates[j], lead=lead, n_ctx=n_ctx, n_heads=MLSTM_HEADS)
            hs = _mlstm_scan(q, k, kt, v, gc, gr, n_heads=MLSTM_HEADS, n_ctx=n_ctx)
            h_lat = _mlstm_out(hs, o, xc, zs, ml_mh_g[j], ml_skip[j], w_out, h_lat, gt,
                               lead=lead, n_ctx=n_ctx, n_heads=MLSTM_HEADS)
        else:
            assert not ctx_out
            w_in = hy_w_in[j].astype(BF16)
            w_out = hy_w_out[j].astype(BF16)
            x0, x1, v, zs = _in_proj(h_lat, norm_g[i], sh, sc, w_in, n_groups=4, n_out=4,
                                     epilogue=_epilogue_gate_last)
            tables = _dft_tables()
            kfull = _hyena_filter(seq, hy_f_w1[j], hy_f_b1[j], hy_f_freq1[j], hy_f_w2[j], hy_f_b2[j],
                                  hy_f_freq2[j], hy_f_w3[j], hy_f_b3[j])
            kf = _filter_spectrum(kfull, tables)
            yb = _long_conv(x0, x1, v, zs, hy_conv_w[j], hy_conv_b[j], hy_h_bias[j], kf, tables)
            h_lat = _out_matmul(yb, w_out, h_lat, gt, final_g=close_g)
    return h_lat if norm_fused else _final_norm(h_lat, final_g)
```

```python
import functools
import math

import jax
import jax.numpy as jnp
import numpy as np
from jax import lax
from jax.experimental import pallas as pl
from jax.experimental.pallas import tpu as pltpu

GRID_W = 64
N_MIXERS = 3
EPS = 1e-6
MLSTM_HEADS = 8
HYENA_EMB_DIM = 33
HYENA_FAST_DECAY = 0.3
HYENA_SLOW_DECAY = 1.5
HYENA_DECAY_TARGET = 1e-2

V7X_VMEM_LIMIT_BYTES = 56 * 1024 * 1024
SUBLANES = 8
BF16_ROWS = 16
LANES = 128
MXU_TILE = 256

F32 = jnp.float32
BF16 = jnp.bfloat16


def _tiles(m, n, tm, tn):
    tm, tn = min(tm, m), min(tn, n)
    assert m % tm == 0 and n % tn == 0, (m, n, tm, tn)
    return tm, tn


def _sigmoid(x):
    return 1.0 / (1.0 + jnp.exp(-x))


def _silu(x):
    return x * _sigmoid(x)


def _dot_bf16(a, b):
    return jnp.dot(a.astype(BF16), b.astype(BF16), preferred_element_type=F32)


def _params(semantics):
    return pltpu.CompilerParams(dimension_semantics=semantics,
                                vmem_limit_bytes=V7X_VMEM_LIMIT_BYTES)


def _resident(shape, n_grid):
    zeros = (0,) * len(shape)
    return pl.BlockSpec(shape, lambda *_: zeros, pipeline_mode=pl.Buffered(1))


def _ada_kernel(c_ref, w_ref, b_ref, o_ref):
    o_ref[0] = _dot_bf16(_silu(c_ref[...]), w_ref[0]) + b_ref[0]


def _ada_params(cond, ada_w, ada_b, *, tn=512):
    depth, d, n = ada_w.shape
    r = cond.shape[0]
    return pl.pallas_call(
        _ada_kernel,
        grid=(depth, n // tn),
        in_specs=[pl.BlockSpec((r, d), lambda i, j: (0, 0)),
                  pl.BlockSpec((1, d, tn), lambda i, j: (i, 0, j)),
                  pl.BlockSpec((1, 1, tn), lambda i, j: (i, 0, j))],
        out_specs=pl.BlockSpec((1, r, tn), lambda i, j: (i, 0, j)),
        out_shape=jax.ShapeDtypeStruct((depth, r, n), F32),
        compiler_params=_params(("arbitrary", "arbitrary")),
        name="ada_params",
    )(cond, ada_w, ada_b.reshape(depth, 1, n))


def _epilogue_conv(a, g, z):
    return a * _sigmoid(g), _silu(z)


def _epilogue_gate_last(*accs):
    return accs[:-1] + (_silu(accs[-1]),)


def _in_proj_kernel(*refs, n_groups, epilogue):
    h_ref, g_ref, sh_ref, sc_ref = refs[:4]
    w_refs = refs[4:4 + n_groups]
    o_refs = refs[4 + n_groups:-1]
    u_ref = refs[-1]

    @pl.when(pl.program_id(1) == 0)
    def _():
        x = h_ref[...]
        ms = jnp.mean(x * x, axis=-1, keepdims=True)
        y = x * lax.rsqrt(ms + EPS) * g_ref[...]
        u_ref[...] = (y * (1.0 + sc_ref[0]) + sh_ref[0]).astype(u_ref.dtype)

    u = u_ref[...]
    outs = epilogue(*[jnp.dot(u, w[...], preferred_element_type=F32) for w in w_refs])
    for o_ref, val in zip(o_refs, outs, strict=True):
        o_ref[...] = val.astype(o_ref.dtype)


IN_PROJ_ROWS = 512


def _in_proj(h, g, sh, sc, w, *, n_groups, n_out, epilogue, n_ctx=0, tm=IN_PROJ_ROWS, tn=512):
    bn, t, d = h.shape
    e = w.shape[1] // n_groups
    tm, tn = _tiles(t, e, tm, tn)
    assert n_ctx % tm == 0
    tiles_per_seq, ctx_tiles = t // tm, n_ctx // tm
    col_tiles = e // tn
    n_mod = sh.shape[0]

    def mod_row(i, j):
        return (jnp.where(i % tiles_per_seq < ctx_tiles, n_mod - 1, i // tiles_per_seq), 0, 0)

    w_specs = [pl.BlockSpec((d, tn), functools.partial(lambda i, j, k: (0, k * col_tiles + j), k=k))
               for k in range(n_groups)]
    outs = pl.pallas_call(
        functools.partial(_in_proj_kernel, n_groups=n_groups, epilogue=epilogue),
        grid=(bn * tiles_per_seq, col_tiles),
        in_specs=[
            pl.BlockSpec((tm, d), lambda i, j: (i, 0)),
            pl.BlockSpec((1, d), lambda i, j: (0, 0)),
            pl.BlockSpec((1, 1, d), mod_row),
            pl.BlockSpec((1, 1, d), mod_row),
        ] + w_specs,
        out_specs=[pl.BlockSpec((tm, tn), lambda i, j: (i, j))] * n_out,
        out_shape=[jax.ShapeDtypeStruct((bn * t, e), BF16)] * n_out,
        scratch_shapes=[pltpu.VMEM((tm, d), BF16)],
        compiler_params=_params(("arbitrary", "arbitrary")),
        name="in_proj",
    )(h.reshape(bn * t, d), g.reshape(1, d), sh.reshape(n_mod, 1, d),
      sc.reshape(n_mod, 1, d), *([w] * n_groups))
    return [o.reshape(bn, t, e) for o in outs]


def _out_matmul_kernel(y_ref, w_ref, h_ref, gt_ref, o_ref):
    acc = jnp.dot(y_ref[...], w_ref[...], preferred_element_type=F32)
    o_ref[...] = h_ref[...] + gt_ref[0] * acc


def _out_matmul_norm_kernel(y_ref, w_ref, h_ref, gt_ref, g_ref, o_ref):
    acc = jnp.dot(y_ref[...], w_ref[...], preferred_element_type=F32)
    x = h_ref[...] + gt_ref[0] * acc
    ms = jnp.mean(x * x, axis=-1, keepdims=True)
    o_ref[...] = x * lax.rsqrt(ms + EPS) * g_ref[...]


def _out_matmul(y, w, h, gt, *, final_g=None, tm=512):
    bn, t, e = y.shape
    d = w.shape[1]
    tm, tn = _tiles(t, d, tm, d)
    tiles_per_seq = t // tm
    in_specs = [
        pl.BlockSpec((tm, e), lambda i, j: (i, 0)),
        _resident((e, d), 2),
        pl.BlockSpec((tm, tn), lambda i, j: (i, j)),
        pl.BlockSpec((1, 1, tn), lambda i, j: (i // tiles_per_seq, 0, j)),
    ]
    args = [y.reshape(bn * t, e), w, h.reshape(bn * t, d), gt.reshape(bn, 1, d)]
    if final_g is not None:
        in_specs.append(_resident((1, d), 2))
        args.append(final_g.reshape(1, d))
    out = pl.pallas_call(
        _out_matmul_kernel if final_g is None else _out_matmul_norm_kernel,
        grid=(bn * tiles_per_seq, d // tn),
        in_specs=in_specs,
        out_specs=pl.BlockSpec((tm, tn), lambda i, j: (i, j)),
        out_shape=jax.ShapeDtypeStruct((bn * t, d), F32),
        compiler_params=_params(("arbitrary", "arbitrary")),
        name="out_matmul",
    )(*args)
    return out.reshape(bn, t, d)


CONV_PAD = 16
LN_UNROLL = 16


def _conv_mid_kernel(y_ref, zs_ref, w_ref, b_ref, lg_ref, lb_ref, o_ref, xp_ref, xs_ref, cv_ref, *, seg, width):
    tb, e = y_ref.shape
    n_seg = tb // seg
    half = width // 2
    rows_s = xs_ref.shape[2]
    rb = SUBLANES
    zeros = jnp.zeros((CONV_PAD, e), F32)
    for s in range(n_seg):
        xp_ref[s, 0:CONV_PAD, :] = zeros
        xp_ref[s, CONV_PAD:CONV_PAD + seg, :] = y_ref[s * seg:(s + 1) * seg, :].astype(F32)
        xp_ref[s, CONV_PAD + seg:2 * CONV_PAD + seg, :] = zeros

    def lane_block(cb, carry):
        lanes = pl.ds(pl.multiple_of(cb * LANES, LANES), LANES)
        for sh in range(rb):
            for s in range(n_seg):
                xs_ref[sh, s] = xp_ref[s, sh:sh + rows_s, lanes]
        taps = [w_ref[k:k + 1, lanes] for k in range(width)]
        bias = b_ref[:, lanes]
        for s in range(n_seg):
            for r in range(seg // rb):
                acc = bias
                for k in range(width):
                    off = CONV_PAD - half + k
                    base = (off // rb) * rb + r * rb
                    acc = acc + taps[k] * xs_ref[off % rb, s, base:base + rb, :]
                cv_ref[s * seg + r * rb:s * seg + (r + 1) * rb, lanes] = acc
        return carry

    lax.fori_loop(0, e // LANES, lane_block, 0)

    def row_block(i, carry):
        rows = pl.ds(pl.multiple_of(i * rb, rb), rb)
        cv = cv_ref[rows, :]
        mu = jnp.mean(cv, axis=-1, keepdims=True)
        xc = cv - mu
        var = jnp.mean(xc * xc, axis=-1, keepdims=True)
        yn = xc * lax.rsqrt(var + EPS) * lg_ref[...] + lb_ref[...]
        o_ref[rows, :] = (_silu(yn) * zs_ref[rows, :].astype(F32)).astype(o_ref.dtype)
        return carry

    lax.fori_loop(0, tb // rb, row_block, 0, unroll=LN_UNROLL)


def _conv_mid(y, zs, dw_w, dw_b, ln_g, ln_b, *, seg, tb=256):
    bn, t, e = y.shape
    width = dw_w.shape[0]
    m = bn * t
    tb = min(tb, m)
    assert m % tb == 0 and tb % seg == 0 and t % seg == 0 and width // 2 <= CONV_PAD
    row = lambda a: a.reshape(1, e)
    tok = pl.BlockSpec((tb, e), lambda i: (i, 0))
    vec = pl.BlockSpec((1, e), lambda i: (0, 0))
    out = pl.pallas_call(
        functools.partial(_conv_mid_kernel, seg=seg, width=width),
        grid=(m // tb,),
        in_specs=[tok, tok, pl.BlockSpec((width, e), lambda i: (0, 0)), vec, vec, vec],
        out_specs=tok,
        out_shape=jax.ShapeDtypeStruct((m, e), BF16),
        scratch_shapes=[pltpu.VMEM((tb // seg, seg + 2 * CONV_PAD, e), F32),
                        pltpu.VMEM((SUBLANES, tb // seg, seg + 2 * CONV_PAD - SUBLANES, LANES), F32),
                        pltpu.VMEM((tb, e), F32)],
        compiler_params=_params(("arbitrary",)),
        name="conv_mid",
    )(y.reshape(m, e), zs.reshape(m, e), dw_w, row(dw_b), row(ln_g), row(ln_b))
    return out.reshape(bn, t, e)


CONV_DFT = 96
CONV_FREQS = CONV_DFT // 2 + 1
CONV_HALF = -(-CONV_FREQS // SUBLANES) * SUBLANES
CONV_LANE_UNROLL = 4


def _conv_dft_tables(seg, width, n_seg):
    f = np.arange(CONV_FREQS)
    th = 2 * np.pi * np.outer(f, np.arange(seg)) / CONV_DFT
    fwd = np.zeros((2 * CONV_HALF, seg))
    fwd[:CONV_FREQS] = np.cos(th)
    fwd[CONV_HALF:CONV_HALF + CONV_FREQS] = -np.sin(th)
    c = np.where((f == 0) | (f == CONV_DFT // 2), 1.0, 2.0) / CONV_DFT
    inv = np.zeros((seg, 2 * CONV_HALF))
    inv[:, :CONV_FREQS] = (c[:, None] * np.cos(th)).T
    inv[:, CONV_HALF:CONV_HALF + CONV_FREQS] = (-c[:, None] * np.sin(th)).T
    eye = np.eye(n_seg)
    thw = 2 * np.pi * np.outer(f, np.arange(width) - width // 2) / CONV_DFT
    wtab = np.zeros((2 * CONV_HALF, width))
    wtab[:CONV_FREQS] = np.cos(thw)
    wtab[CONV_HALF:CONV_HALF + CONV_FREQS] = np.sin(thw)
    bf = lambda a: jnp.asarray(a, F32).astype(BF16)
    return bf(np.kron(eye, fwd)), bf(np.kron(eye, inv)), jnp.asarray(wtab, F32)


def _conv_dft_kernel(y_ref, zs_ref, w_ref, b_ref, lg_ref, lb_ref, fwd_ref, inv_ref, wtab_ref, o_ref,
                     h_ref, cv_ref, *, n_seg):
    tb, e = y_ref.shape
    rb = SUBLANES

    @pl.when(pl.program_id(0) == 0)
    def _():
        h_ref[...] = jnp.dot(wtab_ref[...], w_ref[...], preferred_element_type=F32,
                             precision=lax.Precision.HIGHEST)

    def lane_block(cb, carry):
        lanes = pl.ds(pl.multiple_of(cb * MXU_TILE, MXU_TILE), MXU_TILE)
        x = jnp.dot(fwd_ref[...], y_ref[:, lanes], preferred_element_type=F32)
        hr = h_ref[0:CONV_HALF, lanes]
        hi = h_ref[CONV_HALF:2 * CONV_HALF, lanes]
        parts = []
        for s in range(n_seg):
            xr = x[s * 2 * CONV_HALF:s * 2 * CONV_HALF + CONV_HALF]
            xi = x[s * 2 * CONV_HALF + CONV_HALF:(s + 1) * 2 * CONV_HALF]
            parts += [xr * hr - xi * hi, xr * hi + xi * hr]
        yf = jnp.concatenate(parts, axis=0).astype(BF16)
        cv_ref[:, lanes] = jnp.dot(inv_ref[...], yf, preferred_element_type=F32) + b_ref[:, lanes]
        return carry

    lax.fori_loop(0, e // MXU_TILE, lane_block, 0, unroll=CONV_LANE_UNROLL)

    def row_block(i, carry):
        rows = pl.ds(pl.multiple_of(i * rb, rb), rb)
        cv = cv_ref[rows, :]
        mu = jnp.mean(cv, axis=-1, keepdims=True)
        xc = cv - mu
        var = jnp.mean(xc * xc, axis=-1, keepdims=True)
        yn = xc * lax.rsqrt(var + EPS) * lg_ref[...] + lb_ref[...]
        o_ref[rows, :] = (_silu(yn) * zs_ref[rows, :].astype(F32)).astype(o_ref.dtype)
        return carry

    lax.fori_loop(0, tb // rb, row_block, 0, unroll=LN_UNROLL)


def _conv_mid_dft(y, zs, dw_w, dw_b, ln_g, ln_b, *, seg, tb=256):
    bn, t, e = y.shape
    width = dw_w.shape[0]
    m = bn * t
    tb = min(tb, m)
    assert m % tb == 0 and tb % seg == 0 and t % seg == 0 and seg + width - 1 <= CONV_DFT
    n_seg = tb // seg
    fwd, inv, wtab = _conv_dft_tables(seg, width, n_seg)
    row = lambda a: a.reshape(1, e)
    tok = pl.BlockSpec((tb, e), lambda i: (i, 0))
    out = pl.pallas_call(
        functools.partial(_conv_dft_kernel, n_seg=n_seg),
        grid=(m // tb,),
        in_specs=[tok, tok, _resident((width, e), 1), _resident((1, e), 1), _resident((1, e), 1),
                  _resident((1, e), 1), _resident(fwd.shape, 1), _resident(inv.shape, 1),
                  _resident(wtab.shape, 1)],
        out_specs=tok,
        out_shape=jax.ShapeDtypeStruct((m, e), BF16),
        scratch_shapes=[pltpu.VMEM((2 * CONV_HALF, e), F32), pltpu.VMEM((tb, e), F32)],
        compiler_params=_params(("arbitrary",)),
        name="conv_mid_dft",
    )(y.reshape(m, e), zs.reshape(m, e), dw_w, row(dw_b), row(ln_g), row(ln_b), fwd, inv, wtab)
    return out.reshape(bn, t, e)


STAGE_PAD = 8


def _expand_block_diag(w):
    g, bi, bo = w.shape
    per = MXU_TILE // bi
    wt = w.reshape(g // per, per, bi, bo)
    eye = jnp.eye(per, dtype=w.dtype)
    return jnp.einsum('tgio,gh->tgiho', wt, eye).reshape(g // per, per * bi, per * bo)


def _mlstm_pre_kernel(xm_ref, prev_ref, next_ref, cw_ref, cb_ref, wq_ref, wk_ref, wkt_ref, wv_ref, wo_ref,
                      bo_ref, wg_ref, wgt_ref, bg_ref, bgt_ref,
                      q_ref, k_ref, kt_ref, v_ref, o_ref, xc_ref, gc_ref, gr_ref, stage_ref,
                      *, tiles_per_seq, ctx_tiles, n_heads):
    tb, e = xm_ref.shape
    q_scale = (e // n_heads) ** -0.5
    t = pl.program_id(0) % tiles_per_seq
    has_prev = jnp.logical_and(t != 0, t != ctx_tiles).astype(F32)
    has_next = jnp.logical_and(t != ctx_tiles - 1, t != tiles_per_seq - 1).astype(F32)
    gc = jnp.zeros(gc_ref.shape, F32) + bg_ref[...]
    gr = jnp.zeros(gr_ref.shape, F32) + bgt_ref[...]
    nt = (((1,), (1,)), ((), ()))
    for j in range(e // MXU_TILE):
        lanes = slice(j * MXU_TILE, (j + 1) * MXU_TILE)
        xm = xm_ref[:, lanes]
        stage_ref[STAGE_PAD - 1:STAGE_PAD, :] = \
            prev_ref[:, lanes].astype(F32)[BF16_ROWS - 1:BF16_ROWS, :] * has_prev
        stage_ref[STAGE_PAD:STAGE_PAD + tb, :] = xm.astype(F32)
        stage_ref[STAGE_PAD + tb:STAGE_PAD + tb + 1, :] = next_ref[:, lanes].astype(F32)[0:1, :] * has_next
        cw = cw_ref[:, lanes]
        pre = (cb_ref[:, lanes] + cw[0:1] * stage_ref[STAGE_PAD - 1:STAGE_PAD - 1 + tb, :]
               + cw[1:2] * stage_ref[STAGE_PAD:STAGE_PAD + tb, :]
               + cw[2:3] * stage_ref[STAGE_PAD + 1:STAGE_PAD + 1 + tb, :])
        xcb = _silu(pre).astype(BF16)
        xc_ref[:, lanes] = xcb
        q = jnp.dot(xcb, wq_ref[j], preferred_element_type=F32)
        k = jnp.dot(xcb, wk_ref[j], preferred_element_type=F32)
        v = jnp.dot(xm, wv_ref[j], preferred_element_type=F32)
        o = jnp.dot(xcb, wo_ref[j], preferred_element_type=F32) + bo_ref[:, lanes]
        kt_ref[lanes, :] = lax.dot_general(wkt_ref[j], xcb, nt, preferred_element_type=F32).astype(BF16)
        qb, kb, vb = q.astype(BF16), k.astype(BF16), v.astype(BF16)
        q_ref[:, lanes] = (q * q_scale).astype(BF16)
        k_ref[:, lanes] = kb
        v_ref[:, lanes] = vb
        o_ref[:, lanes] = _sigmoid(o).astype(BF16)
        for i, a in enumerate((qb, kb, vb)):
            gc = gc + jnp.dot(a, wg_ref[i, lanes, :], preferred_element_type=F32)
            gr = gr + lax.dot_general(wgt_ref[i, :, lanes], a, nt, preferred_element_type=F32)
    col = lax.broadcasted_iota(jnp.int32, gc.shape, 1)
    row = lax.broadcasted_iota(jnp.int32, gr.shape, 0)
    gc_ref[...] = jnp.where((col // n_heads) % 2 == 1, _log_sigmoid(gc), gc)
    gr_ref[...] = jnp.where((row // n_heads) % 2 == 1, _log_sigmoid(gr), gr)


def _mlstm_pre(xm, conv_w, conv_b, w_q, w_k, w_v, w_o, b_o, w_gates, b_gates, *, lead, n_ctx, n_heads, tb=256):
    bsz, t_all, e = xm.shape
    t = t_all - lead
    assert t % tb == 0 and n_ctx % tb == 0 and lead % tb == 0
    tiles_per_seq, ctx_tiles = t // tb, n_ctx // tb
    tiles_all, lead_tiles = t_all // tb, lead // tb
    m = bsz * t
    n_g = w_gates.shape[1]
    hpt = tb // BF16_ROWS
    n_halo = bsz * t_all // BF16_ROWS

    def src_tile(i):
        return (i // tiles_per_seq) * tiles_all + lead_tiles + i % tiles_per_seq
    bd = lambda w: _expand_block_diag(w).astype(BF16)
    wq, wk, wv, wo = bd(w_q), bd(w_k), bd(w_v), bd(w_o)
    wkt = jnp.swapaxes(wk, 1, 2)
    wg = w_gates.reshape(3, e, n_g).astype(BF16)
    wgt = jnp.swapaxes(wg, 1, 2)
    tok = pl.BlockSpec((tb, e), lambda i: (i, 0))
    row = lambda a: a.reshape(1, -1)
    consts = [conv_w, row(conv_b), wq, wk, wkt, wv, wo, row(b_o), wg, wgt, row(b_gates), b_gates.reshape(-1, 1)]
    sds = jax.ShapeDtypeStruct
    xm2 = xm.reshape(bsz * t_all, e)
    outs = pl.pallas_call(
        functools.partial(_mlstm_pre_kernel, tiles_per_seq=tiles_per_seq, ctx_tiles=ctx_tiles,
                          n_heads=n_heads),
        grid=(m // tb,),
        in_specs=[pl.BlockSpec((tb, e), lambda i: (src_tile(i), 0)),
                  pl.BlockSpec((BF16_ROWS, e), lambda i: (jnp.maximum(src_tile(i) * hpt - 1, 0), 0)),
                  pl.BlockSpec((BF16_ROWS, e), lambda i: (jnp.minimum((src_tile(i) + 1) * hpt, n_halo - 1), 0))]
                 + [_resident(a.shape, 1) for a in consts],
        out_specs=[tok, tok,
                   pl.BlockSpec((None, e, tb), lambda i: (i // tiles_per_seq, 0, i % tiles_per_seq)),
                   tok, tok, tok,
                   pl.BlockSpec((tb, n_g), lambda i: (i, 0)),
                   pl.BlockSpec((None, n_g, tb), lambda i: (i // tiles_per_seq, 0, i % tiles_per_seq))],
        out_shape=[sds((m, e), BF16), sds((m, e), BF16), sds((bsz, e, t), BF16), sds((m, e), BF16),
                   sds((m, e), BF16), sds((m, e), BF16), sds((m, n_g), F32), sds((bsz, n_g, t), F32)],
        scratch_shapes=[pltpu.VMEM((tb + 2 * STAGE_PAD, MXU_TILE), F32)],
        compiler_params=_params(("arbitrary",)),
        name="mlstm_pre",
    )(xm2, xm2, xm2, *consts)
    q, k, kt, v, o, xc, gc, gr = outs
    r3 = lambda a: a.reshape(bsz, t, -1)
    return r3(q), r3(k), kt, r3(v), r3(o), r3(xc), r3(gc), gr


SCAN_CHUNK = 256


def _log_sigmoid(x):
    return jnp.minimum(x, 0.0) - jnp.log1p(jnp.exp(-jnp.abs(x)))


SCAN_HEADS_PER_STEP = 4


def _split_bf16(x):
    hi = x.astype(BF16)
    return hi, (x - hi.astype(F32)).astype(BF16)


def _mlstm_scan_kernel(q_ref, k_ref, kt_ref, v_ref, gc_ref, gr_ref, o_ref,
                       c_ref, cb_ref, n_ref, m_ref, *, n_heads, heads_per_step):
    direction = pl.program_id(0)

    @pl.when(pl.program_id(3) == 0)
    def _():
        c_ref[...] = jnp.zeros_like(c_ref)
        cb_ref[...] = jnp.zeros_like(cb_ref)
        n_ref[...] = jnp.zeros_like(n_ref)
        m_ref[...] = jnp.full_like(m_ref, -1e30)

    L = q_ref.shape[1]
    dh = q_ref.shape[2] // heads_per_step
    n_g = gc_ref.shape[2]
    gc = gc_ref[0]
    gr = gr_ref[0]
    lane = lax.broadcasted_iota(jnp.int32, (L, n_g), 1)
    sub = lax.broadcasted_iota(jnp.int32, (n_g, L), 0)
    sign = 1 - 2 * direction
    diff = (lax.broadcasted_iota(jnp.int32, (L, L), 0)
            - lax.broadcasted_iota(jnp.int32, (L, L), 1)) * sign
    causal = diff >= 0
    anti = diff <= 0

    for hh in range(heads_per_step):
        lanes = slice(hh * dh, (hh + 1) * dh)
        head = pl.program_id(2) * heads_per_step + hh
        col_i = direction * (2 * n_heads) + head
        col_f = col_i + n_heads
        lf_col = jnp.sum(jnp.where(lane == col_f, gc, 0.0), axis=1, keepdims=True)
        li_row = jnp.sum(jnp.where(sub == col_i, gr, 0.0), axis=0, keepdims=True)
        lf_row = jnp.sum(jnp.where(sub == col_f, gr, 0.0), axis=0, keepdims=True)
        b_col = jnp.sum(jnp.where(causal, lf_row, 0.0), axis=1, keepdims=True)
        b_row = jnp.sum(jnp.where(anti, lf_col, 0.0), axis=0, keepdims=True)
        total = jnp.sum(lf_row, axis=1, keepdims=True)

        m = m_ref[hh]
        log_d = jnp.where(causal, b_col - b_row + li_row, -jnp.inf)
        log_inter = b_col + m
        m_t = jnp.maximum(log_inter, jnp.max(log_d, axis=1, keepdims=True))
        dmat = jnp.exp(log_d - m_t)
        a = jnp.exp(log_inter - m_t)

        q = q_ref[0, :, lanes]
        kt = kt_ref[0, lanes, :]
        v = v_ref[0, :, lanes]
        s = jnp.dot(q, kt, preferred_element_type=F32) * dmat
        num = a * jnp.dot(q, cb_ref[hh], preferred_element_type=F32) \
            + jnp.dot(s.astype(BF16), v, preferred_element_type=F32)
        n_hi, n_lo = _split_bf16(n_ref[hh])
        qn2 = lax.dot_general(q, jnp.concatenate([n_hi, n_lo], axis=0), (((1,), (1,)), ((), ())),
                              preferred_element_type=F32)
        qn = qn2[:, 0:1] + qn2[:, SUBLANES:SUBLANES + 1]
        den = a * qn + jnp.sum(s, axis=1, keepdims=True)
        o_ref[0, 0, :, lanes] = (num / jnp.maximum(jnp.abs(den), jnp.exp(-m_t))).astype(o_ref.dtype)

        log_w_row = total - b_row + li_row
        m_new = jnp.maximum(total + m, jnp.max(log_w_row, axis=1, keepdims=True))
        decay = jnp.exp(total + m - m_new)
        w_hi, w_lo = _split_bf16(jnp.exp(log_w_row - m_new))
        kwt = kt * w_hi
        c_new = decay * c_ref[hh] + jnp.dot(kwt, v, preferred_element_type=F32)
        c_ref[hh] = c_new
        cb_ref[hh] = c_new.astype(BF16)
        w2 = jnp.concatenate([jnp.broadcast_to(w_hi, (SUBLANES, L)), jnp.broadcast_to(w_lo, (SUBLANES, L))], axis=0)
        n_add = jnp.dot(w2, k_ref[0, :, lanes], preferred_element_type=F32)
        n_ref[hh] = decay * n_ref[hh] + n_add[0:SUBLANES] + n_add[SUBLANES:2 * SUBLANES]
        m_ref[hh] = m_new


def _mlstm_scan(q, k, kt, v, gc, gr, *, n_heads, n_ctx, chunk=SCAN_CHUNK, heads_per_step=SCAN_HEADS_PER_STEP):
    bsz, t, e = q.shape
    dh = e // n_heads
    hp = heads_per_step
    L = chunk
    assert t % L == 0 and n_ctx % L == 0 and n_heads % hp == 0
    nc, nc_ctx = t // L, n_ctx // L

    def cidx(d, c):
        rev = jnp.where(c < nc_ctx, nc_ctx - 1 - c, nc - 1 - c + nc_ctx)
        return jnp.where(d == 0, c, rev)

    tok = pl.BlockSpec((1, L, hp * dh), lambda d, b, h, c: (b, cidx(d, c), h))
    return pl.pallas_call(
        functools.partial(_mlstm_scan_kernel, n_heads=n_heads, heads_per_step=hp),
        grid=(2, bsz, n_heads // hp, nc),
        in_specs=[
            tok, tok,
            pl.BlockSpec((1, hp * dh, L), lambda d, b, h, c: (b, h, cidx(d, c))),
            tok,
            pl.BlockSpec((1, L, 4 * n_heads), lambda d, b, h, c: (b, cidx(d, c), 0)),
            pl.BlockSpec((1, 4 * n_heads, L), lambda d, b, h, c: (b, 0, cidx(d, c))),
        ],
        out_specs=pl.BlockSpec((1, 1, L, hp * dh), lambda d, b, h, c: (d, b, cidx(d, c), h)),
        out_shape=jax.ShapeDtypeStruct((2, bsz, t, e), BF16),
        scratch_shapes=[pltpu.VMEM((hp, dh, dh), F32), pltpu.VMEM((hp, dh, dh), BF16),
                        pltpu.VMEM((hp, SUBLANES, dh), F32), pltpu.VMEM((hp, 1, 1), F32)],
        compiler_params=_params(("arbitrary",) * 4),
        name="mlstm_scan",
    )(q, k, kt, v, gc, gr)


def _mlstm_out_kernel(hf_ref, hb_ref, o_ref, xc_ref, zs_ref, mh_ref, skip_ref, w_ref, res_ref, gt_ref,
                      out_ref, y_ref, *, n_heads):
    tm, e = o_ref.shape
    dh = e // n_heads
    for hd in range(n_heads):
        lanes = slice(hd * dh, (hd + 1) * dh)
        h = hf_ref[:, lanes].astype(F32) + hb_ref[:, lanes].astype(F32)
        mu = jnp.mean(h, axis=-1, keepdims=True)
        hc = h - mu
        var = jnp.mean(hc * hc, axis=-1, keepdims=True)
        hn = hc * lax.rsqrt(var + EPS) * mh_ref[:, lanes]
        y = (o_ref[:, lanes].astype(F32) * hn + skip_ref[:, lanes] * xc_ref[:, lanes].astype(F32)) \
            * zs_ref[:, lanes].astype(F32)
        y_ref[:, lanes] = y.astype(BF16)
    acc = jnp.dot(y_ref[...], w_ref[...], preferred_element_type=F32)
    out_ref[...] = res_ref[...] + gt_ref[0] * acc


def _mlstm_out(hs, o, xc, zs, mh_g, skip, w_out, h_lat, gt, *, lead, n_ctx, n_heads, tm=256):
    _, bsz, t, e = hs.shape
    n = t - n_ctx
    d = w_out.shape[1]
    assert n % tm == 0 and n_ctx % tm == 0 and lead % tm == 0
    tiles, off = n // tm, n_ctx // tm
    zs_off = off + lead // tm
    tok = pl.BlockSpec((None, tm, e), lambda i: (i // tiles, off + i % tiles, 0))
    zs_tok = pl.BlockSpec((None, tm, e), lambda i: (i // tiles, zs_off + i % tiles, 0))
    row = lambda a: a.reshape(1, -1)
    out = pl.pallas_call(
        functools.partial(_mlstm_out_kernel, n_heads=n_heads),
        grid=(bsz * tiles,),
        in_specs=[pl.BlockSpec((None, None, tm, e), lambda i: (0, i // tiles, off + i % tiles, 0)),
                  pl.BlockSpec((None, None, tm, e), lambda i: (1, i // tiles, off + i % tiles, 0)),
                  tok, tok, zs_tok, _resident((1, e), 1), _resident((1, e), 1), _resident((e, d), 1),
                  pl.BlockSpec((tm, d), lambda i: (i, 0)),
                  pl.BlockSpec((1, 1, d), lambda i: (i // tiles, 0, 0))],
        out_specs=pl.BlockSpec((tm, d), lambda i: (i, 0)),
        out_shape=jax.ShapeDtypeStruct((bsz * n, d), F32),
        scratch_shapes=[pltpu.VMEM((tm, e), BF16)],
        compiler_params=_params(("arbitrary",)),
        name="mlstm_out",
    )(hs, hs, o, xc, zs, row(mh_g), row(skip), w_out, h_lat.reshape(bsz * n, d), gt.reshape(bsz, 1, d))
    return out.reshape(bsz, n, d)


def _filter_features(seq):
    r = np.arange(2 * seq)
    lag = np.where(r < seq, r, 2 * seq - r).clip(0, seq - 1).astype(np.float64)
    t = lag / (seq - 1)
    bands = (HYENA_EMB_DIM - 1) // 2
    ang = 2.0 * math.pi * lag / seq
    fr = np.linspace(1e-4, bands - 1, bands)
    feat = np.concatenate([t[:, None], np.cos(fr[None] * ang[:, None]), -np.sin(fr[None] * ang[:, None])], axis=-1)
    return jnp.asarray(feat, F32), jnp.asarray(t[:, None], F32)


def _filter_kernel(feat_ref, t_ref, w1_ref, b1_ref, f1_ref, w2_ref, b2_ref, f2_ref, w3_ref, b3_ref, dl_ref,
                   o_ref, *, seq):
    tr = o_ref.shape[0]
    h = jnp.sin(f1_ref[...] * (_dot_bf16(feat_ref[...], w1_ref[...]) + b1_ref[...]))
    h = jnp.sin(f2_ref[...] * (_dot_bf16(h, w2_ref[...]) + b2_ref[...]))
    h = _dot_bf16(h, w3_ref[...]) + b3_ref[...]
    out = h * jnp.exp(-t_ref[...] * dl_ref[...])
    row = pl.program_id(0) * tr + lax.broadcasted_iota(jnp.int32, (tr, 1), 0)
    o_ref[...] = jnp.where(row == seq, 0.0, out)


def _hyena_filter(seq, f_w1, f_b1, f_freq1, f_w2, f_b2, f_freq2, f_w3, f_b3, *, tr=512, tn=1024):
    e = f_w3.shape[1] // 2
    feat, t = _filter_features(seq)
    lo = math.log(HYENA_DECAY_TARGET) / HYENA_FAST_DECAY
    hi = math.log(HYENA_DECAY_TARGET) / HYENA_SLOW_DECAY
    deltas = jnp.asarray(np.abs(np.linspace(lo, hi, e)), F32).reshape(1, e)
    fdim = f_w1.shape[1]
    row_tiles, col_tiles = 2 * seq // tr, e // tn
    half = lambda i: i // (row_tiles // 2)
    row = lambda a: a.reshape(1, -1)
    small = lambda shape: pl.BlockSpec(shape, lambda i, j: (0, 0))
    return pl.pallas_call(
        functools.partial(_filter_kernel, seq=seq),
        grid=(row_tiles, col_tiles),
        in_specs=[pl.BlockSpec((tr, HYENA_EMB_DIM), lambda i, j: (i, 0)),
                  pl.BlockSpec((tr, 1), lambda i, j: (i, 0)),
                  small((HYENA_EMB_DIM, fdim)), small((1, fdim)), small((1, fdim)),
                  small((fdim, fdim)), small((1, fdim)), small((1, fdim)),
                  pl.BlockSpec((fdim, tn), lambda i, j: (0, half(i) * col_tiles + j)),
                  pl.BlockSpec((1, tn), lambda i, j: (0, half(i) * col_tiles + j)),
                  pl.BlockSpec((1, tn), lambda i, j: (0, j))],
        out_specs=pl.BlockSpec((tr, tn), lambda i, j: (i, j)),
        out_shape=jax.ShapeDtypeStruct((2 * seq, e), F32),
        compiler_params=_params(("arbitrary", "arbitrary")),
        name="hyena_filter",
    )(feat, t, f_w1, row(f_b1), row(f_freq1), f_w2, row(f_b2), row(f_freq2), f_w3, row(f_b3), deltas)


FFT_N1 = 64
FFT_N2 = 128
FFT_N = FFT_N1 * FFT_N2
FFT_K1 = FFT_N1 // 2 + 1
SHORT_CHUNK = 256
K1_UNROLL = 11
OUTER_UNROLL = 4
_REAL_ONLY_COLS = (1, 2 * (FFT_K1 - 1) + 1)


def _dft_tables():
    n1h = FFT_N1 // 2
    k1 = np.arange(FFT_K1)
    eye = np.eye(SUBLANES)

    def outer_fwd(n_rows):
        th = 2 * np.pi * np.outer(k1, np.arange(n_rows)) / FFT_N1
        f = np.stack([np.cos(th), -np.sin(th)], axis=1).reshape(2 * FFT_K1, n_rows)
        return np.kron(f, eye)

    n2 = np.arange(FFT_N2)
    k2 = np.arange(FFT_N2)
    ph = 2 * np.pi * n2[None, None, :] * (k1[:, None, None] + FFT_N1 * k2[None, :, None]) / FFT_N
    gr, gi = np.cos(ph), -np.sin(ph)
    g = np.concatenate([np.concatenate([gr, -gi], axis=2), np.concatenate([gi, gr], axis=2)], axis=1)
    ginv = np.swapaxes(g, 1, 2)
    th2 = 2 * np.pi * np.outer(np.arange(n1h), k1) / FFT_N1
    c = np.where((k1 == 0) | (k1 == n1h), 1.0, 2.0)[None, :] / FFT_N
    finv = np.stack([c * np.cos(th2), -c * np.sin(th2)], axis=2).reshape(n1h, 2 * FFT_K1)
    finv = finv[:, [col for col in range(2 * FFT_K1) if col not in _REAL_ONLY_COLS]]
    bf = lambda a: jnp.asarray(a, F32).astype(BF16)
    return dict(s1_half=bf(outer_fwd(n1h)), s1_full=bf(outer_fwd(FFT_N1)), g=bf(g), ginv=bf(ginv),
                i2=bf(np.kron(finv, eye)))


def _outer_forward(src_ref, s1_ref, dst_ref, n1_rows):
    def body(n2h, carry):
        r0 = pl.multiple_of(n2h * SUBLANES, SUBLANES)
        tiles = [src_ref[pl.ds(n1 * FFT_N2 + r0, SUBLANES), :] for n1 in range(n1_rows)]
        rhs = jnp.concatenate(tiles, axis=0).astype(BF16)
        out = jnp.dot(s1_ref[...], rhs, preferred_element_type=F32)
        for k1 in range(FFT_K1):
            for ri in range(2):
                row = (2 * k1 + ri) * SUBLANES
                dst_ref[k1, pl.ds(ri * FFT_N2 + r0, SUBLANES), :] = out[row:row + SUBLANES, :]
        return carry
    lax.fori_loop(0, FFT_N2 // SUBLANES, body, 0, unroll=OUTER_UNROLL)


def _filter_spectrum_kernel(k_ref, s1_ref, g_ref, kf_ref, a_ref):
    _outer_forward(k_ref, s1_ref, a_ref, FFT_N1)

    def body(k1, carry):
        kf_ref[k1] = jnp.dot(g_ref[k1], a_ref[k1].astype(BF16), preferred_element_type=F32)
        return carry
    lax.fori_loop(0, FFT_K1, body, 0)


def _filter_spectrum(kfull, tables, *, cb=MXU_TILE):
    e = kfull.shape[1]
    cb = min(cb, e)
    return pl.pallas_call(
        _filter_spectrum_kernel,
        grid=(e // cb,),
        in_specs=[pl.BlockSpec((FFT_N, cb), lambda c: (0, c)),
                  _resident(tables["s1_full"].shape, 1), _resident(tables["g"].shape, 1)],
        out_specs=pl.BlockSpec((FFT_K1, 2 * FFT_N2, cb), lambda c: (0, 0, c)),
        out_shape=jax.ShapeDtypeStruct((FFT_K1, 2 * FFT_N2, e), F32),
        scratch_shapes=[pltpu.VMEM((FFT_K1, 2 * FFT_N2, cb), F32)],
        compiler_params=_params(("arbitrary",)),
        name="filter_spectrum",
    )(kfull, tables["s1_full"], tables["g"])


def _short_conv(x_ref, cw_ref, cb_ref, group, c, stage_ref, n_rows):
    r0 = pl.multiple_of(c * SHORT_CHUNK, SHORT_CHUNK)
    prev0 = pl.multiple_of(jnp.maximum(r0 - BF16_ROWS, 0), BF16_ROWS)
    next0 = pl.multiple_of(jnp.minimum(r0 + SHORT_CHUNK, n_rows - BF16_ROWS), BF16_ROWS)
    prev = x_ref[pl.ds(prev0, BF16_ROWS), :].astype(F32)[BF16_ROWS - 1:BF16_ROWS, :] * jnp.where(c > 0, 1.0, 0.0)
    nxt = x_ref[pl.ds(next0, BF16_ROWS), :].astype(F32)[0:1, :] * jnp.where(r0 + SHORT_CHUNK < n_rows, 1.0, 0.0)
    stage_ref[STAGE_PAD - 1:STAGE_PAD, :] = prev
    stage_ref[STAGE_PAD:STAGE_PAD + SHORT_CHUNK, :] = x_ref[pl.ds(r0, SHORT_CHUNK), :].astype(F32)
    stage_ref[STAGE_PAD + SHORT_CHUNK:STAGE_PAD + SHORT_CHUNK + 1, :] = nxt
    w = cw_ref[group]
    return (cb_ref[group]
            + w[0:1] * stage_ref[STAGE_PAD - 1:STAGE_PAD - 1 + SHORT_CHUNK, :]
            + w[1:2] * stage_ref[STAGE_PAD:STAGE_PAD + SHORT_CHUNK, :]
            + w[2:3] * stage_ref[STAGE_PAD + 1:STAGE_PAD + 1 + SHORT_CHUNK, :])


def _long_conv_kernel(x0_ref, x1_ref, v_ref, zs_ref, cw_ref, cb_ref, hb_ref, kf_ref,
                      s1_ref, g_ref, ginv_ref, i2_ref, o_ref, w_ref, ab_ref, stage_ref):
    n_rows = x1_ref.shape[0]
    n_chunks = n_rows // SHORT_CHUNK

    def make_w(c, carry):
        rows = pl.ds(pl.multiple_of(c * SHORT_CHUNK, SHORT_CHUNK), SHORT_CHUNK)
        x1c = _short_conv(x1_ref, cw_ref, cb_ref, 1, c, stage_ref, n_rows)
        vc = _short_conv(v_ref, cw_ref, cb_ref, 2, c, stage_ref, n_rows)
        w_ref[rows, :] = x1c * vc
        return carry
    lax.fori_loop(0, n_chunks, make_w, 0)

    _outer_forward(w_ref, s1_ref, ab_ref, FFT_N1 // 2)

    def per_k1(k1, carry):
        x = jnp.dot(g_ref[k1], ab_ref[k1].astype(BF16), preferred_element_type=F32)
        kf = kf_ref[k1]
        xr, xi = x[:FFT_N2], x[FFT_N2:]
        kr, ki = kf[:FFT_N2], kf[FFT_N2:]
        y = jnp.concatenate([xr * kr - xi * ki, xr * ki + xi * kr], axis=0).astype(BF16)
        ab_ref[k1] = jnp.dot(ginv_ref[k1], y, preferred_element_type=F32)
        return carry
    lax.fori_loop(0, FFT_K1, per_k1, 0, unroll=K1_UNROLL)

    def outer_inverse(n2h, carry):
        r0 = pl.multiple_of(n2h * SUBLANES, SUBLANES)
        tiles = [ab_ref[k1, pl.ds(ri * FFT_N2 + r0, SUBLANES), :] for k1 in range(FFT_K1) for ri in range(2)
                 if 2 * k1 + ri not in _REAL_ONLY_COLS]
        rhs = jnp.concatenate(tiles, axis=0).astype(BF16)
        out = jnp.dot(i2_ref[...], rhs, preferred_element_type=F32)
        for n1 in range(FFT_N1 // 2):
            rows = pl.ds(n1 * FFT_N2 + r0, SUBLANES)
            w_ref[rows, :] = out[n1 * SUBLANES:(n1 + 1) * SUBLANES, :] + hb_ref[...] * w_ref[rows, :]
        return carry
    lax.fori_loop(0, FFT_N2 // SUBLANES, outer_inverse, 0, unroll=OUTER_UNROLL)

    def finish(c, carry):
        rows = pl.ds(pl.multiple_of(c * SHORT_CHUNK, SHORT_CHUNK), SHORT_CHUNK)
        x0c = _short_conv(x0_ref, cw_ref, cb_ref, 0, c, stage_ref, n_rows)
        o_ref[rows, :] = (x0c * w_ref[rows, :] * zs_ref[rows, :].astype(F32)).astype(o_ref.dtype)
        return carry
    lax.fori_loop(0, n_chunks, finish, 0)


def _long_conv(x0, x1, v, zs, conv_w, conv_b, h_bias, kf, tables, *, cb=MXU_TILE):
    bsz, seq, e = x0.shape
    assert seq == FFT_N // 2
    cb = min(cb, e)
    cw = conv_w.reshape(3, 3, e).transpose(1, 0, 2)
    cbias = conv_b.reshape(3, 1, e)
    tok = pl.BlockSpec((None, seq, cb), lambda c, b: (b, 0, c))
    consts = [tables["s1_half"], tables["g"], tables["ginv"], tables["i2"]]
    return pl.pallas_call(
        _long_conv_kernel,
        grid=(e // cb, bsz),
        in_specs=[tok, tok, tok, tok,
                  pl.BlockSpec((3, 3, cb), lambda c, b: (0, 0, c)),
                  pl.BlockSpec((3, 1, cb), lambda c, b: (0, 0, c)),
                  pl.BlockSpec((1, cb), lambda c, b: (0, c)),
                  pl.BlockSpec((FFT_K1, 2 * FFT_N2, cb), lambda c, b: (0, 0, c), pipeline_mode=pl.Buffered(1))]
                 + [_resident(a.shape, 2) for a in consts],
        out_specs=tok,
        out_shape=jax.ShapeDtypeStruct((bsz, seq, e), BF16),
        scratch_shapes=[pltpu.VMEM((seq, cb), F32),
                        pltpu.VMEM((FFT_K1, 2 * FFT_N2, cb), F32),
                        pltpu.VMEM((SHORT_CHUNK + 2 * STAGE_PAD, cb), F32)],
        compiler_params=_params(("arbitrary", "arbitrary")),
        name="long_conv",
    )(x0, x1, v, zs, cw, cbias, h_bias.reshape(1, e), kf, *consts)


def _final_norm_kernel(h_ref, g_ref, o_ref):
    x = h_ref[...]
    ms = jnp.mean(x * x, axis=-1, keepdims=True)
    o_ref[...] = x * lax.rsqrt(ms + EPS) * g_ref[...]


def _final_norm(h, g, *, tm=1024):
    bn, t, d = h.shape
    m = bn * t
    tm = min(tm, m)
    out = pl.pallas_call(
        _final_norm_kernel,
        grid=(m // tm,),
        in_specs=[pl.BlockSpec((tm, d), lambda i: (i, 0)),
                  pl.BlockSpec((1, d), lambda i: (0, 0))],
        out_specs=pl.BlockSpec((tm, d), lambda i: (i, 0)),
        out_shape=jax.ShapeDtypeStruct((m, d), F32),
        compiler_params=_params(("arbitrary",)),
        name="final_norm",
    )(h.reshape(m, d), g.reshape(1, d))
    return out.reshape(bn, t, d)


def kernel(x, c, ctx, c_ctx, norm_g, ada_w, ada_b, final_g, cv_w_in, cv_dw_w, cv_dw_b, cv_ln_g, cv_ln_b, cv_w_out, ml_w_in, ml_conv_w, ml_conv_b, ml_w_q, ml_w_k, ml_w_v, ml_w_o, ml_b_o, ml_w_gates, ml_b_gates, ml_mh_g, ml_skip, ml_w_out, hy_w_in, hy_conv_w, hy_conv_b, hy_f_w1, hy_f_b1, hy_f_freq1, hy_f_w2, hy_f_b2, hy_f_freq2, hy_f_w3, hy_f_b3, hy_h_bias, hy_w_out):
    depth = norm_g.shape[0]
    bsz, seq, d = x.shape
    n_ctx = ctx.shape[1]
    readers = [i for i in range(depth) if i % N_MIXERS == 1]
    last_reader = readers[-1] if readers else -1

    cond_rows = -(-(bsz + 1) // SUBLANES) * SUBLANES
    cond = jnp.concatenate([c, c_ctx[None], jnp.zeros((cond_rows - bsz - 1, d), F32)], axis=0)
    ada = _ada_params(cond, ada_w, ada_b)

    h_lat, h_ctx = x, ctx
    norm_fused = (depth - 1) % N_MIXERS != 1
    for i in range(depth):
        kind, j = i % N_MIXERS, i // N_MIXERS
        close_g = final_g if (norm_fused and i == depth - 1) else None
        ctx_in = i <= last_reader
        ctx_out = i < last_reader
        sh, sc, gt = jnp.split(ada[i, :bsz], 3, axis=-1)
        sh_c, sc_c, gt_c = (jnp.broadcast_to(a, (bsz, d)) for a in jnp.split(ada[i, bsz:bsz + 1], 3, axis=-1))
        if kind == 0:
            w_in = cv_w_in[j].astype(BF16)
            w_out = cv_w_out[j].astype(BF16)
            streams = [(h_lat, sh, sc, gt, GRID_W, _conv_mid_dft)]
            if ctx_out:
                streams.append((h_ctx, sh_c, sc_c, gt_c, n_ctx, _conv_mid))
            new = []
            for h, s_h, s_c, g_t, seg, conv_mid in streams:
                y, zs = _in_proj(h, norm_g[i], s_h, s_c, w_in, n_groups=3, n_out=2, epilogue=_epilogue_conv,
                                 tm=2 * IN_PROJ_ROWS)
                yb = conv_mid(y, zs, cv_dw_w[j], cv_dw_b[j], cv_ln_g[j], cv_ln_b[j], seg=seg)
                new.append(_out_matmul(yb, w_out, h, g_t, final_g=close_g if h is h_lat else None))
            h_lat = new[0]
            if ctx_out:
                h_ctx = new[1]
        elif kind == 1:
            assert ctx_in and not ctx_out
            w_in = ml_w_in[j].astype(BF16)
            w_out = ml_w_out[j].astype(BF16)
            lead = (-n_ctx) % IN_PROJ_ROWS
            h_all = jnp.concatenate([jnp.zeros((bsz, lead, d), F32), h_ctx, h_lat], axis=1)
            sh_all = jnp.concatenate([sh, sh_c[:1]], axis=0)
            sc_all = jnp.concatenate([sc, sc_c[:1]], axis=0)
            xm, zs = _in_proj(h_all, norm_g[i], sh_all, sc_all, w_in, n_groups=2, n_out=2,
                              epilogue=_epilogue_gate_last, n_ctx=lead + n_ctx, tm=IN_PROJ_ROWS, tn=1024)
            q, k, kt, v, o, xc, gc, gr = _mlstm_pre(
                xm, ml_conv_w[j], ml_conv_b[j], ml_w_q[j], ml_w_k[j], ml_w_v[j], ml_w_o[j], ml_b_o[j],
                ml_w_gates[j], ml_b_gates[j], lead=lead, n_ctx=n_ctx, n_heads=MLSTM_HEADS)
            hs = _mlstm_scan(q, k, kt, v, gc, gr, n_heads=MLSTM_HEADS, n_ctx=n_ctx)
            h_lat = _mlstm_out(hs, o, xc, zs, ml_mh_g[j], ml_skip[j], w_out, h_lat, gt,
                               lead=lead, n_ctx=n_ctx, n_heads=MLSTM_HEADS)
        else:
            assert not ctx_out
            w_in = hy_w_in[j].astype(BF16)
            w_out = hy_w_out[j].astype(BF16)
            x0, x1, v, zs = _in_proj(h_lat, norm_g[i], sh, sc, w_in, n_groups=4, n_out=4,
                                     epilogue=_epilogue_gate_last)
            tables = _dft_tables()
            kfull = _hyena_filter(seq, hy_f_w1[j], hy_f_b1[j], hy_f_freq1[j], hy_f_w2[j], hy_f_b2[j],
                                  hy_f_freq2[j], hy_f_w3[j], hy_f_b3[j])
            kf = _filter_spectrum(kfull, tables)
            yb = _long_conv(x0, x1, v, zs, hy_conv_w[j], hy_conv_b[j], hy_h_bias[j], kf, tables)
            h_lat = _out_matmul(yb, w_out, h_lat, gt, final_g=close_g)
    return h_lat if norm_fused else _final_norm(h_lat, final_g)
```

```python
import functools
import math

import jax
import jax.numpy as jnp
import numpy as np
from jax import lax
from jax.experimental import pallas as pl
from jax.experimental.pallas import tpu as pltpu

GRID_W = 64
N_MIXERS = 3
EPS = 1e-6
MLSTM_HEADS = 8
HYENA_EMB_DIM = 33
HYENA_FAST_DECAY = 0.3
HYENA_SLOW_DECAY = 1.5
HYENA_DECAY_TARGET = 1e-2

V7X_VMEM_LIMIT_BYTES = 56 * 1024 * 1024
SUBLANES = 8
BF16_ROWS = 16
LANES = 128
MXU_TILE = 256

F32 = jnp.float32
BF16 = jnp.bfloat16


def _tiles(m, n, tm, tn):
    tm, tn = min(tm, m), min(tn, n)
    assert m % tm == 0 and n % tn == 0, (m, n, tm, tn)
    return tm, tn


def _sigmoid(x):
    return 1.0 / (1.0 + jnp.exp(-x))


def _silu(x):
    return x * _sigmoid(x)


def _dot_bf16(a, b):
    return jnp.dot(a.astype(BF16), b.astype(BF16), preferred_element_type=F32)


def _params(semantics):
    return pltpu.CompilerParams(dimension_semantics=semantics,
                                vmem_limit_bytes=V7X_VMEM_LIMIT_BYTES)


def _resident(shape, n_grid):
    zeros = (0,) * len(shape)
    return pl.BlockSpec(shape, lambda *_: zeros, pipeline_mode=pl.Buffered(1))


def _ada_kernel(c_ref, w_ref, b_ref, o_ref):
    o_ref[0] = _dot_bf16(_silu(c_ref[...]), w_ref[0]) + b_ref[0]


def _ada_params(cond, ada_w, ada_b, *, tn=512):
    depth, d, n = ada_w.shape
    r = cond.shape[0]
    return pl.pallas_call(
        _ada_kernel,
        grid=(depth, n // tn),
        in_specs=[pl.BlockSpec((r, d), lambda i, j: (0, 0)),
                  pl.BlockSpec((1, d, tn), lambda i, j: (i, 0, j)),
                  pl.BlockSpec((1, 1, tn), lambda i, j: (i, 0, j))],
        out_specs=pl.BlockSpec((1, r, tn), lambda i, j: (i, 0, j)),
        out_shape=jax.ShapeDtypeStruct((depth, r, n), F32),
        compiler_params=_params(("arbitrary", "arbitrary")),
        name="ada_params",
    )(cond, ada_w, ada_b.reshape(depth, 1, n))


def _epilogue_conv(a, g, z):
    return a * _sigmoid(g), _silu(z)


def _epilogue_gate_last(*accs):
    return accs[:-1] + (_silu(accs[-1]),)


def _in_proj_kernel(*refs, n_groups, epilogue):
    h_ref, g_ref, sh_ref, sc_ref = refs[:4]
    w_refs = refs[4:4 + n_groups]
    o_refs = refs[4 + n_groups:-1]
    u_ref = refs[-1]

    @pl.when(pl.program_id(1) == 0)
    def _():
        x = h_ref[...]
        ms = jnp.mean(x * x, axis=-1, keepdims=True)
        y = x * lax.rsqrt(ms + EPS) * g_ref[...]
        u_ref[...] = (y * (1.0 + sc_ref[0]) + sh_ref[0]).astype(u_ref.dtype)

    u = u_ref[...]
    outs = epilogue(*[jnp.dot(u, w[...], preferred_element_type=F32) for w in w_refs])
    for o_ref, val in zip(o_refs, outs, strict=True):
        o_ref[...] = val.astype(o_ref.dtype)


IN_PROJ_ROWS = 512


def _in_proj(h, g, sh, sc, w, *, n_groups, n_out, epilogue, n_ctx=0, tm=IN_PROJ_ROWS, tn=512):
    bn, t, d = h.shape
    e = w.shape[1] // n_groups
    tm, tn = _tiles(t, e, tm, tn)
    assert n_ctx % tm == 0
    tiles_per_seq, ctx_tiles = t // tm, n_ctx // tm
    col_tiles = e // tn
    n_mod = sh.shape[0]

    def mod_row(i, j):
        return (jnp.where(i % tiles_per_seq < ctx_tiles, n_mod - 1, i // tiles_per_seq), 0, 0)

    w_specs = [pl.BlockSpec((d, tn), functools.partial(lambda i, j, k: (0, k * col_tiles + j), k=k))
               for k in range(n_groups)]
    outs = pl.pallas_call(
        functools.partial(_in_proj_kernel, n_groups=n_groups, epilogue=epilogue),
        grid=(bn * tiles_per_seq, col_tiles),
        in_specs=[
            pl.BlockSpec((tm, d), lambda i, j: (i, 0)),
            pl.BlockSpec((1, d), lambda i, j: (0, 0)),
            pl.BlockSpec((1, 1, d), mod_row),
            pl.BlockSpec((1, 1, d), mod_row),
        ] + w_specs,
        out_specs=[pl.BlockSpec((tm, tn), lambda i, j: (i, j))] * n_out,
        out_shape=[jax.ShapeDtypeStruct((bn * t, e), BF16)] * n_out,
        scratch_shapes=[pltpu.VMEM((tm, d), BF16)],
        compiler_params=_params(("arbitrary", "arbitrary")),
        name="in_proj",
    )(h.reshape(bn * t, d), g.reshape(1, d), sh.reshape(n_mod, 1, d),
      sc.reshape(n_mod, 1, d), *([w] * n_groups))
    return [o.reshape(bn, t, e) for o in outs]


def _out_matmul_kernel(y_ref, w_ref, h_ref, gt_ref, o_ref):
    acc = jnp.dot(y_ref[...], w_ref[...], preferred_element_type=F32)
    o_ref[...] = h_ref[...] + gt_ref[0] * acc


def _out_matmul_norm_kernel(y_ref, w_ref, h_ref, gt_ref, g_ref, o_ref):
    acc = jnp.dot(y_ref[...], w_ref[...], preferred_element_type=F32)
    x = h_ref[...] + gt_ref[0] * acc
    ms = jnp.mean(x * x, axis=-1, keepdims=True)
    o_ref[...] = x * lax.rsqrt(ms + EPS) * g_ref[...]


def _out_matmul(y, w, h, gt, *, final_g=None, tm=512):
    bn, t, e = y.shape
    d = w.shape[1]
    tm, tn = _tiles(t, d, tm, d)
    tiles_per_seq = t // tm
    in_specs = [
        pl.BlockSpec((tm, e), lambda i, j: (i, 0)),
        _resident((e, d), 2),
        pl.BlockSpec((tm, tn), lambda i, j: (i, j)),
        pl.BlockSpec((1, 1, tn), lambda i, j: (i // tiles_per_seq, 0, j)),
    ]
    args = [y.reshape(bn * t, e), w, h.reshape(bn * t, d), gt.reshape(bn, 1, d)]
    if final_g is not None:
        in_specs.append(_resident((1, d), 2))
        args.append(final_g.reshape(1, d))
    out = pl.pallas_call(
        _out_matmul_kernel if final_g is None else _out_matmul_norm_kernel,
        grid=(bn * tiles_per_seq, d // tn),
        in_specs=in_specs,
        out_specs=pl.BlockSpec((tm, tn), lambda i, j: (i, j)),
        out_shape=jax.ShapeDtypeStruct((bn * t, d), F32),
        compiler_params=_params(("arbitrary", "arbitrary")),
        name="out_matmul",
    )(*args)
    return out.reshape(bn, t, d)


CONV_PAD = 16
LN_UNROLL = 16


def _conv_mid_kernel(y_ref, zs_ref, w_ref, b_ref, lg_ref, lb_ref, o_ref, xp_ref, xs_ref, cv_ref, *, seg, width):
    tb, e = y_ref.shape
    n_seg = tb // seg
    half = width // 2
    rows_s = xs_ref.shape[2]
    rb = SUBLANES
    zeros = jnp.zeros((CONV_PAD, e), F32)
    for s in range(n_seg):
        xp_ref[s, 0:CONV_PAD, :] = zeros
        xp_ref[s, CONV_PAD:CONV_PAD + seg, :] = y_ref[s * seg:(s + 1) * seg, :].astype(F32)
        xp_ref[s, CONV_PAD + seg:2 * CONV_PAD + seg, :] = zeros

    def lane_block(cb, carry):
        lanes = pl.ds(pl.multiple_of(cb * LANES, LANES), LANES)
        for sh in range(rb):
            for s in range(n_seg):
                xs_ref[sh, s] = xp_ref[s, sh:sh + rows_s, lanes]
        taps = [w_ref[k:k + 1, lanes] for k in range(width)]
        bias = b_ref[:, lanes]
        for s in range(n_seg):
            for r in range(seg // rb):
                acc = bias
                for k in range(width):
                    off = CONV_PAD - half + k
                    base = (off // rb) * rb + r * rb
                    acc = acc + taps[k] * xs_ref[off % rb, s, base:base + rb, :]
                cv_ref[s * seg + r * rb:s * seg + (r + 1) * rb, lanes] = acc
        return carry

    lax.fori_loop(0, e // LANES, lane_block, 0)

    def row_block(i, carry):
        rows = pl.ds(pl.multiple_of(i * rb, rb), rb)
        cv = cv_ref[rows, :]
        mu = jnp.mean(cv, axis=-1, keepdims=True)
        xc = cv - mu
        var = jnp.mean(xc * xc, axis=-1, keepdims=True)
        yn = xc * lax.rsqrt(var + EPS) * lg_ref[...] + lb_ref[...]
        o_ref[rows, :] = (_silu(yn) * zs_ref[rows, :].astype(F32)).astype(o_ref.dtype)
        return carry

    lax.fori_loop(0, tb // rb, row_block, 0, unroll=LN_UNROLL)


def _conv_mid(y, zs, dw_w, dw_b, ln_g, ln_b, *, seg, tb=256):
    bn, t, e = y.shape
    width = dw_w.shape[0]
    m = bn * t
    tb = min(tb, m)
    assert m % tb == 0 and tb % seg == 0 and t % seg == 0 and width // 2 <= CONV_PAD
    row = lambda a: a.reshape(1, e)
    tok = pl.BlockSpec((tb, e), lambda i: (i, 0))
    vec = pl.BlockSpec((1, e), lambda i: (0, 0))
    out = pl.pallas_call(
        functools.partial(_conv_mid_kernel, seg=seg, width=width),
        grid=(m // tb,),
        in_specs=[tok, tok, pl.BlockSpec((width, e), lambda i: (0, 0)), vec, vec, vec],
        out_specs=tok,
        out_shape=jax.ShapeDtypeStruct((m, e), BF16),
        scratch_shapes=[pltpu.VMEM((tb // seg, seg + 2 * CONV_PAD, e), F32),
                        pltpu.VMEM((SUBLANES, tb // seg, seg + 2 * CONV_PAD - SUBLANES, LANES), F32),
                        pltpu.VMEM((tb, e), F32)],
        compiler_params=_params(("arbitrary",)),
        name="conv_mid",
    )(y.reshape(m, e), zs.reshape(m, e), dw_w, row(dw_b), row(ln_g), row(ln_b))
    return out.reshape(bn, t, e)


CONV_DFT = 96
CONV_FREQS = CONV_DFT // 2 + 1
CONV_HALF = -(-CONV_FREQS // SUBLANES) * SUBLANES
CONV_LANE_UNROLL = 4


def _conv_dft_tables(seg, width, n_seg):
    f = np.arange(CONV_FREQS)
    th = 2 * np.pi * np.outer(f, np.arange(seg)) / CONV_DFT
    fwd = np.zeros((2 * CONV_HALF, seg))
    fwd[:CONV_FREQS] = np.cos(th)
    fwd[CONV_HALF:CONV_HALF + CONV_FREQS] = -np.sin(th)
    c = np.where((f == 0) | (f == CONV_DFT // 2), 1.0, 2.0) / CONV_DFT
    inv = np.zeros((seg, 2 * CONV_HALF))
    inv[:, :CONV_FREQS] = (c[:, None] * np.cos(th)).T
    inv[:, CONV_HALF:CONV_HALF + CONV_FREQS] = (-c[:, None] * np.sin(th)).T
    eye = np.eye(n_seg)
    thw = 2 * np.pi * np.outer(f, np.arange(width) - width // 2) / CONV_DFT
    wtab = np.zeros((2 * CONV_HALF, width))
    wtab[:CONV_FREQS] = np.cos(thw)
    wtab[CONV_HALF:CONV_HALF + CONV_FREQS] = np.sin(thw)
    bf = lambda a: jnp.asarray(a, F32).astype(BF16)
    return bf(np.kron(eye, fwd)), bf(np.kron(eye, inv)), jnp.asarray(wtab, F32)


def _conv_dft_kernel(y_ref, zs_ref, w_ref, b_ref, lg_ref, lb_ref, fwd_ref, inv_ref, wtab_ref, o_ref,
                     h_ref, cv_ref, *, n_seg):
    tb, e = y_ref.shape
    rb = SUBLANES

    @pl.when(pl.program_id(0) == 0)
    def _():
        h_ref[...] = jnp.dot(wtab_ref[...], w_ref[...], preferred_element_type=F32,
                             precision=lax.Precision.HIGHEST)

    def lane_block(cb, carry):
        lanes = pl.ds(pl.multiple_of(cb * MXU_TILE, MXU_TILE), MXU_TILE)
        x = jnp.dot(fwd_ref[...], y_ref[:, lanes], preferred_element_type=F32)
        hr = h_ref[0:CONV_HALF, lanes]
        hi = h_ref[CONV_HALF:2 * CONV_HALF, lanes]
        parts = []
        for s in range(n_seg):
            xr = x[s * 2 * CONV_HALF:s * 2 * CONV_HALF + CONV_HALF]
            xi = x[s * 2 * CONV_HALF + CONV_HALF:(s + 1) * 2 * CONV_HALF]
            parts += [xr * hr - xi * hi, xr * hi + xi * hr]
        yf = jnp.concatenate(parts, axis=0).astype(BF16)
        cv_ref[:, lanes] = jnp.dot(inv_ref[...], yf, preferred_element_type=F32) + b_ref[:, lanes]
        return carry

    lax.fori_loop(0, e // MXU_TILE, lane_block, 0, unroll=CONV_LANE_UNROLL)

    def row_block(i, carry):
        rows = pl.ds(pl.multiple_of(i * rb, rb), rb)
        cv = cv_ref[rows, :]
        mu = jnp.mean(cv, axis=-1, keepdims=True)
        xc = cv - mu
        var = jnp.mean(xc * xc, axis=-1, keepdims=True)
        yn = xc * lax.rsqrt(var + EPS) * lg_ref[...] + lb_ref[...]
        o_ref[rows, :] = (_silu(yn) * zs_ref[rows, :].astype(F32)).astype(o_ref.dtype)
        return carry

    lax.fori_loop(0, tb // rb, row_block, 0, unroll=LN_UNROLL)


def _conv_mid_dft(y, zs, dw_w, dw_b, ln_g, ln_b, *, seg, tb=256):
    bn, t, e = y.shape
    width = dw_w.shape[0]
    m = bn * t
    tb = min(tb, m)
    assert m % tb == 0 and tb % seg == 0 and t % seg == 0 and seg + width - 1 <= CONV_DFT
    n_seg = tb // seg
    fwd, inv, wtab = _conv_dft_tables(seg, width, n_seg)
    row = lambda a: a.reshape(1, e)
    tok = pl.BlockSpec((tb, e), lambda i: (i, 0))
    out = pl.pallas_call(
        functools.partial(_conv_dft_kernel, n_seg=n_seg),
        grid=(m // tb,),
        in_specs=[tok, tok, _resident((width, e), 1), _resident((1, e), 1), _resident((1, e), 1),
                  _resident((1, e), 1), _resident(fwd.shape, 1), _resident(inv.shape, 1),
                  _resident(wtab.shape, 1)],
        out_specs=tok,
        out_shape=jax.ShapeDtypeStruct((m, e), BF16),
        scratch_shapes=[pltpu.VMEM((2 * CONV_HALF, e), F32), pltpu.VMEM((tb, e), F32)],
        compiler_params=_params(("arbitrary",)),
        name="conv_mid_dft",
    )(y.reshape(m, e), zs.reshape(m, e), dw_w, row(dw_b), row(ln_g), row(ln_b), fwd, inv, wtab)
    return out.reshape(bn, t, e)


STAGE_PAD = 8


def _expand_block_diag(w):
    g, bi, bo = w.shape
    per = MXU_TILE // bi
    wt = w.reshape(g // per, per, bi, bo)
    eye = jnp.eye(per, dtype=w.dtype)
    return jnp.einsum('tgio,gh->tgiho', wt, eye).reshape(g // per, per * bi, per * bo)


def _mlstm_pre_kernel(xm_ref, prev_ref, next_ref, cw_ref, cb_ref, wq_ref, wk_ref, wkt_ref, wv_ref, wo_ref,
                      bo_ref, wg_ref, bg_ref,
                      q_ref, k_ref, kt_ref, v_ref, o_ref, xc_ref, gc_ref, gr_ref, stage_ref,
                      *, tiles_per_seq, ctx_tiles, n_heads):
    tb, e = xm_ref.shape
    n_g = gc_ref.shape[1]
    q_scale = (e // n_heads) ** -0.5
    t = pl.program_id(0) % tiles_per_seq
    has_prev = jnp.logical_and(t != 0, t != ctx_tiles).astype(F32)
    has_next = jnp.logical_and(t != ctx_tiles - 1, t != tiles_per_seq - 1).astype(F32)
    gc = jnp.zeros(gc_ref.shape, F32) + bg_ref[...]
    nt = (((1,), (1,)), ((), ()))
    for j in range(e // MXU_TILE):
        lanes = slice(j * MXU_TILE, (j + 1) * MXU_TILE)
        xm = xm_ref[:, lanes]
        stage_ref[STAGE_PAD - 1:STAGE_PAD, :] = \
            prev_ref[:, lanes].astype(F32)[BF16_ROWS - 1:BF16_ROWS, :] * has_prev
        stage_ref[STAGE_PAD:STAGE_PAD + tb, :] = xm.astype(F32)
        stage_ref[STAGE_PAD + tb:STAGE_PAD + tb + 1, :] = next_ref[:, lanes].astype(F32)[0:1, :] * has_next
        cw = cw_ref[:, lanes]
        pre = (cb_ref[:, lanes] + cw[0:1] * stage_ref[STAGE_PAD - 1:STAGE_PAD - 1 + tb, :]
               + cw[1:2] * stage_ref[STAGE_PAD:STAGE_PAD + tb, :]
               + cw[2:3] * stage_ref[STAGE_PAD + 1:STAGE_PAD + 1 + tb, :])
        xcb = _silu(pre).astype(BF16)
        xc_ref[:, lanes] = xcb
        q = jnp.dot(xcb, wq_ref[j], preferred_element_type=F32)
        k = jnp.dot(xcb, wk_ref[j], preferred_element_type=F32)
        v = jnp.dot(xm, wv_ref[j], preferred_element_type=F32)
        o = jnp.dot(xcb, wo_ref[j], preferred_element_type=F32) + bo_ref[:, lanes]
        kt_ref[lanes, :] = lax.dot_general(wkt_ref[j], xcb, nt, preferred_element_type=F32).astype(BF16)
        qb, kb, vb = q.astype(BF16), k.astype(BF16), v.astype(BF16)
        q_ref[:, lanes] = (q * q_scale).astype(BF16)
        k_ref[:, lanes] = kb
        v_ref[:, lanes] = vb
        o_ref[:, lanes] = _sigmoid(o).astype(BF16)
        for i, a in enumerate((qb, kb, vb)):
            gc = gc + jnp.dot(a, wg_ref[i, lanes, :], preferred_element_type=F32)
    col = lax.broadcasted_iota(jnp.int32, gc.shape, 1)
    gc = jnp.where((col // n_heads) % 2 == 1, _log_sigmoid(gc), gc)
    gc_ref[...] = gc
    gc_wide = jnp.concatenate([gc, jnp.zeros((tb, LANES - n_g), F32)], axis=1)
    gr_ref[...] = gc_wide.T[:n_g, :]


def _mlstm_pre(xm, conv_w, conv_b, w_q, w_k, w_v, w_o, b_o, w_gates, b_gates, *, lead, n_ctx, n_heads, tb=256):
    bsz, t_all, e = xm.shape
    t = t_all - lead
    assert t % tb == 0 and n_ctx % tb == 0 and lead % tb == 0
    tiles_per_seq, ctx_tiles = t // tb, n_ctx // tb
    tiles_all, lead_tiles = t_all // tb, lead // tb
    m = bsz * t
    n_g = w_gates.shape[1]
    hpt = tb // BF16_ROWS
    n_halo = bsz * t_all // BF16_ROWS

    def src_tile(i):
        return (i // tiles_per_seq) * tiles_all + lead_tiles + i % tiles_per_seq
    bd = lambda w: _expand_block_diag(w).astype(BF16)
    wq, wk, wv, wo = bd(w_q), bd(w_k), bd(w_v), bd(w_o)
    wkt = jnp.swapaxes(wk, 1, 2)
    wg = w_gates.reshape(3, e, n_g).astype(BF16)
    tok = pl.BlockSpec((tb, e), lambda i: (i, 0))
    row = lambda a: a.reshape(1, -1)
    consts = [conv_w, row(conv_b), wq, wk, wkt, wv, wo, row(b_o), wg, row(b_gates)]
    sds = jax.ShapeDtypeStruct
    xm2 = xm.reshape(bsz * t_all, e)
    outs = pl.pallas_call(
        functools.partial(_mlstm_pre_kernel, tiles_per_seq=tiles_per_seq, ctx_tiles=ctx_tiles,
                          n_heads=n_heads),
        grid=(m // tb,),
        in_specs=[pl.BlockSpec((tb, e), lambda i: (src_tile(i), 0)),
                  pl.BlockSpec((BF16_ROWS, e), lambda i: (jnp.maximum(src_tile(i) * hpt - 1, 0), 0)),
                  pl.BlockSpec((BF16_ROWS, e), lambda i: (jnp.minimum((src_tile(i) + 1) * hpt, n_halo - 1), 0))]
                 + [_resident(a.shape, 1) for a in consts],
        out_specs=[tok, tok,
                   pl.BlockSpec((None, e, tb), lambda i: (i // tiles_per_seq, 0, i % tiles_per_seq)),
                   tok, tok, tok,
                   pl.BlockSpec((tb, n_g), lambda i: (i, 0)),
                   pl.BlockSpec((None, n_g, tb), lambda i: (i // tiles_per_seq, 0, i % tiles_per_seq))],
        out_shape=[sds((m, e), BF16), sds((m, e), BF16), sds((bsz, e, t), BF16), sds((m, e), BF16),
                   sds((m, e), BF16), sds((m, e), BF16), sds((m, n_g), F32), sds((bsz, n_g, t), F32)],
        scratch_shapes=[pltpu.VMEM((tb + 2 * STAGE_PAD, MXU_TILE), F32)],
        compiler_params=_params(("arbitrary",)),
        name="mlstm_pre",
    )(xm2, xm2, xm2, *consts)
    q, k, kt, v, o, xc, gc, gr = outs
    r3 = lambda a: a.reshape(bsz, t, -1)
    return r3(q), r3(k), kt, r3(v), r3(o), r3(xc), r3(gc), gr


SCAN_CHUNK = 256


def _log_sigmoid(x):
    return jnp.minimum(x, 0.0) - jnp.log1p(jnp.exp(-jnp.abs(x)))


SCAN_HEADS_PER_STEP = 4


def _split_bf16(x):
    hi = x.astype(BF16)
    return hi, (x - hi.astype(F32)).astype(BF16)


def _mlstm_scan_kernel(q_ref, k_ref, kt_ref, v_ref, gc_ref, gr_ref, o_ref,
                       c_ref, cb_ref, n_ref, m_ref, *, n_heads, heads_per_step):
    direction = pl.program_id(0)

    @pl.when(pl.program_id(3) == 0)
    def _():
        c_ref[...] = jnp.zeros_like(c_ref)
        cb_ref[...] = jnp.zeros_like(cb_ref)
        n_ref[...] = jnp.zeros_like(n_ref)
        m_ref[...] = jnp.full_like(m_ref, -1e30)

    L = q_ref.shape[1]
    dh = q_ref.shape[2] // heads_per_step
    n_g = gc_ref.shape[2]
    gc = gc_ref[0]
    gr = gr_ref[0]
    lane = lax.broadcasted_iota(jnp.int32, (L, n_g), 1)
    sub = lax.broadcasted_iota(jnp.int32, (n_g, L), 0)
    sign = 1 - 2 * direction
    diff = (lax.broadcasted_iota(jnp.int32, (L, L), 0)
            - lax.broadcasted_iota(jnp.int32, (L, L), 1)) * sign
    causal = diff >= 0
    anti = diff <= 0

    for hh in range(heads_per_step):
        lanes = slice(hh * dh, (hh + 1) * dh)
        head = pl.program_id(2) * heads_per_step + hh
        col_i = direction * (2 * n_heads) + head
        col_f = col_i + n_heads
        lf_col = jnp.sum(jnp.where(lane == col_f, gc, 0.0), axis=1, keepdims=True)
        li_row = jnp.sum(jnp.where(sub == col_i, gr, 0.0), axis=0, keepdims=True)
        lf_row = jnp.sum(jnp.where(sub == col_f, gr, 0.0), axis=0, keepdims=True)
        b_col = jnp.sum(jnp.where(causal, lf_row, 0.0), axis=1, keepdims=True)
        b_row = jnp.sum(jnp.where(anti, lf_col, 0.0), axis=0, keepdims=True)
        total = jnp.sum(lf_row, axis=1, keepdims=True)

        m = m_ref[hh]
        log_d = jnp.where(causal, b_col - b_row + li_row, -jnp.inf)
        log_inter = b_col + m
        m_t = jnp.maximum(log_inter, jnp.max(log_d, axis=1, keepdims=True))
        dmat = jnp.exp(log_d - m_t)
        a = jnp.exp(log_inter - m_t)

        q = q_ref[0, :, lanes]
        kt = kt_ref[0, lanes, :]
        v = v_ref[0, :, lanes]
        s = jnp.dot(q, kt, preferred_element_type=F32) * dmat
        num = a * jnp.dot(q, cb_ref[hh], preferred_element_type=F32) \
            + jnp.dot(s.astype(BF16), v, preferred_element_type=F32)
        n_hi, n_lo = _split_bf16(n_ref[hh])
        qn2 = lax.dot_general(q, jnp.concatenate([n_hi, n_lo], axis=0), (((1,), (1,)), ((), ())),
                              preferred_element_type=F32)
        qn = qn2[:, 0:1] + qn2[:, SUBLANES:SUBLANES + 1]
        den = a * qn + jnp.sum(s, axis=1, keepdims=True)
        o_ref[0, 0, :, lanes] = (num / jnp.maximum(jnp.abs(den), jnp.exp(-m_t))).astype(o_ref.dtype)

        log_w_row = total - b_row + li_row
        m_new = jnp.maximum(total + m, jnp.max(log_w_row, axis=1, keepdims=True))
        decay = jnp.exp(total + m - m_new)
        w_hi, w_lo = _split_bf16(jnp.exp(log_w_row - m_new))
        kwt = kt * w_hi
        c_new = decay * c_ref[hh] + jnp.dot(kwt, v, preferred_element_type=F32)
        c_ref[hh] = c_new
        cb_ref[hh] = c_new.astype(BF16)
        w2 = jnp.concatenate([jnp.broadcast_to(w_hi, (SUBLANES, L)), jnp.broadcast_to(w_lo, (SUBLANES, L))], axis=0)
        n_add = jnp.dot(w2, k_ref[0, :, lanes], preferred_element_type=F32)
        n_ref[hh] = decay * n_ref[hh] + n_add[0:SUBLANES] + n_add[SUBLANES:2 * SUBLANES]
        m_ref[hh] = m_new


def _mlstm_scan(q, k, kt, v, gc, gr, *, n_heads, n_ctx, chunk=SCAN_CHUNK, heads_per_step=SCAN_HEADS_PER_STEP):
    bsz, t, e = q.shape
    dh = e // n_heads
    hp = heads_per_step
    L = chunk
    assert t % L == 0 and n_ctx % L == 0 and n_heads % hp == 0
    nc, nc_ctx = t // L, n_ctx // L

    def cidx(d, c):
        rev = jnp.where(c < nc_ctx, nc_ctx - 1 - c, nc - 1 - c + nc_ctx)
        return jnp.where(d == 0, c, rev)

    tok = pl.BlockSpec((1, L, hp * dh), lambda d, b, h, c: (b, cidx(d, c), h))
    return pl.pallas_call(
        functools.partial(_mlstm_scan_kernel, n_heads=n_heads, heads_per_step=hp),
        grid=(2, bsz, n_heads // hp, nc),
        in_specs=[
            tok, tok,
            pl.BlockSpec((1, hp * dh, L), lambda d, b, h, c: (b, h, cidx(d, c))),
            tok,
            pl.BlockSpec((1, L, 4 * n_heads), lambda d, b, h, c: (b, cidx(d, c), 0)),
            pl.BlockSpec((1, 4 * n_heads, L), lambda d, b, h, c: (b, 0, cidx(d, c))),
        ],
        out_specs=pl.BlockSpec((1, 1, L, hp * dh), lambda d, b, h, c: (d, b, cidx(d, c), h)),
        out_shape=jax.ShapeDtypeStruct((2, bsz, t, e), BF16),
        scratch_shapes=[pltpu.VMEM((hp, dh, dh), F32), pltpu.VMEM((hp, dh, dh), BF16),
                        pltpu.VMEM((hp, SUBLANES, dh), F32), pltpu.VMEM((hp, 1, 1), F32)],
        compiler_params=_params(("arbitrary",) * 4),
        name="mlstm_scan",
    )(q, k, kt, v, gc, gr)


def _mlstm_out_kernel(hf_ref, hb_ref, o_ref, xc_ref, zs_ref, mh_ref, skip_ref, w_ref, res_ref, gt_ref,
                      out_ref, y_ref, *, n_heads):
    tm, e = o_ref.shape
    dh = e // n_heads
    for hd in range(n_heads):
        lanes = slice(hd * dh, (hd + 1) * dh)
        h = hf_ref[:, lanes].astype(F32) + hb_ref[:, lanes].astype(F32)
        mu = jnp.mean(h, axis=-1, keepdims=True)
        hc = h - mu
        var = jnp.mean(hc * hc, axis=-1, keepdims=True)
        hn = hc * lax.rsqrt(var + EPS) * mh_ref[:, lanes]
        y = (o_ref[:, lanes].astype(F32) * hn + skip_ref[:, lanes] * xc_ref[:, lanes].astype(F32)) \
            * zs_ref[:, lanes].astype(F32)
        y_ref[:, lanes] = y.astype(BF16)
    acc = jnp.dot(y_ref[...], w_ref[...], preferred_element_type=F32)
    out_ref[...] = res_ref[...] + gt_ref[0] * acc


def _mlstm_out(hs, o, xc, zs, mh_g, skip, w_out, h_lat, gt, *, lead, n_ctx, n_heads, tm=256):
    _, bsz, t, e = hs.shape
    n = t - n_ctx
    d = w_out.shape[1]
    assert n % tm == 0 and n_ctx % tm == 0 and lead % tm == 0
    tiles, off = n // tm, n_ctx // tm
    zs_off = off + lead // tm
    tok = pl.BlockSpec((None, tm, e), lambda i: (i // tiles, off + i % tiles, 0))
    zs_tok = pl.BlockSpec((None, tm, e), lambda i: (i // tiles, zs_off + i % tiles, 0))
    row = lambda a: a.reshape(1, -1)
    out = pl.pallas_call(
        functools.partial(_mlstm_out_kernel, n_heads=n_heads),
        grid=(bsz * tiles,),
        in_specs=[pl.BlockSpec((None, None, tm, e), lambda i: (0, i // tiles, off + i % tiles, 0)),
                  pl.BlockSpec((None, None, tm, e), lambda i: (1, i // tiles, off + i % tiles, 0)),
                  tok, tok, zs_tok, _resident((1, e), 1), _resident((1, e), 1), _resident((e, d), 1),
                  pl.BlockSpec((tm, d), lambda i: (i, 0)),
                  pl.BlockSpec((1, 1, d), lambda i: (i // tiles, 0, 0))],
        out_specs=pl.BlockSpec((tm, d), lambda i: (i, 0)),
        out_shape=jax.ShapeDtypeStruct((bsz * n, d), F32),
        scratch_shapes=[pltpu.VMEM((tm, e), BF16)],
        compiler_params=_params(("arbitrary",)),
        name="mlstm_out",
    )(hs, hs, o, xc, zs, row(mh_g), row(skip), w_out, h_lat.reshape(bsz * n, d), gt.reshape(bsz, 1, d))
    return out.reshape(bsz, n, d)


def _filter_features(seq):
    r = np.arange(2 * seq)
    lag = np.where(r < seq, r, 2 * seq - r).clip(0, seq - 1).astype(np.float64)
    t = lag / (seq - 1)
    bands = (HYENA_EMB_DIM - 1) // 2
    ang = 2.0 * math.pi * lag / seq
    fr = np.linspace(1e-4, bands - 1, bands)
    feat = np.concatenate([t[:, None], np.cos(fr[None] * ang[:, None]), -np.sin(fr[None] * ang[:, None])], axis=-1)
    return jnp.asarray(feat, F32), jnp.asarray(t[:, None], F32)


def _filter_kernel(feat_ref, t_ref, w1_ref, b1_ref, f1_ref, w2_ref, b2_ref, f2_ref, w3_ref, b3_ref, dl_ref,
                   o_ref, *, seq):
    tr = o_ref.shape[0]
    h = jnp.sin(f1_ref[...] * (_dot_bf16(feat_ref[...], w1_ref[...]) + b1_ref[...]))
    h = jnp.sin(f2_ref[...] * (_dot_bf16(h, w2_ref[...]) + b2_ref[...]))
    h = _dot_bf16(h, w3_ref[...]) + b3_ref[...]
    out = h * jnp.exp(-t_ref[...] * dl_ref[...])
    row = pl.program_id(0) * tr + lax.broadcasted_iota(jnp.int32, (tr, 1), 0)
    o_ref[...] = jnp.where(row == seq, 0.0, out)


def _hyena_filter(seq, f_w1, f_b1, f_freq1, f_w2, f_b2, f_freq2, f_w3, f_b3, *, tr=512, tn=1024):
    e = f_w3.shape[1] // 2
    feat, t = _filter_features(seq)
    lo = math.log(HYENA_DECAY_TARGET) / HYENA_FAST_DECAY
    hi = math.log(HYENA_DECAY_TARGET) / HYENA_SLOW_DECAY
    deltas = jnp.asarray(np.abs(np.linspace(lo, hi, e)), F32).reshape(1, e)
    fdim = f_w1.shape[1]
    row_tiles, col_tiles = 2 * seq // tr, e // tn
    half = lambda i: i // (row_tiles // 2)
    row = lambda a: a.reshape(1, -1)
    small = lambda shape: pl.BlockSpec(shape, lambda i, j: (0, 0))
    return pl.pallas_call(
        functools.partial(_filter_kernel, seq=seq),
        grid=(row_tiles, col_tiles),
        in_specs=[pl.BlockSpec((tr, HYENA_EMB_DIM), lambda i, j: (i, 0)),
                  pl.BlockSpec((tr, 1), lambda i, j: (i, 0)),
                  small((HYENA_EMB_DIM, fdim)), small((1, fdim)), small((1, fdim)),
                  small((fdim, fdim)), small((1, fdim)), small((1, fdim)),
                  pl.BlockSpec((fdim, tn), lambda i, j: (0, half(i) * col_tiles + j)),
                  pl.BlockSpec((1, tn), lambda i, j: (0, half(i) * col_tiles + j)),
                  pl.BlockSpec((1, tn), lambda i, j: (0, j))],
        out_specs=pl.BlockSpec((tr, tn), lambda i, j: (i, j)),
        out_shape=jax.ShapeDtypeStruct((2 * seq, e), F32),
        compiler_params=_params(("arbitrary", "arbitrary")),
        name="hyena_filter",
    )(feat, t, f_w1, row(f_b1), row(f_freq1), f_w2, row(f_b2), row(f_freq2), f_w3, row(f_b3), deltas)


FFT_N1 = 64
FFT_N2 = 128
FFT_N = FFT_N1 * FFT_N2
FFT_K1 = FFT_N1 // 2 + 1
SHORT_CHUNK = 256
K1_UNROLL = 11
OUTER_UNROLL = 4
_REAL_ONLY_COLS = (1, 2 * (FFT_K1 - 1) + 1)


def _dft_tables():
    n1h = FFT_N1 // 2
    k1 = np.arange(FFT_K1)
    eye = np.eye(SUBLANES)

    def outer_fwd(n_rows):
        th = 2 * np.pi * np.outer(k1, np.arange(n_rows)) / FFT_N1
        f = np.stack([np.cos(th), -np.sin(th)], axis=1).reshape(2 * FFT_K1, n_rows)
        return np.kron(f, eye)

    n2 = np.arange(FFT_N2)
    k2 = np.arange(FFT_N2)
    ph = 2 * np.pi * n2[None, None, :] * (k1[:, None, None] + FFT_N1 * k2[None, :, None]) / FFT_N
    gr, gi = np.cos(ph), -np.sin(ph)
    g = np.concatenate([np.concatenate([gr, -gi], axis=2), np.concatenate([gi, gr], axis=2)], axis=1)
    ginv = np.swapaxes(g, 1, 2)
    th2 = 2 * np.pi * np.outer(np.arange(n1h), k1) / FFT_N1
    c = np.where((k1 == 0) | (k1 == n1h), 1.0, 2.0)[None, :] / FFT_N
    finv = np.stack([c * np.cos(th2), -c * np.sin(th2)], axis=2).reshape(n1h, 2 * FFT_K1)
    finv = finv[:, [col for col in range(2 * FFT_K1) if col not in _REAL_ONLY_COLS]]
    bf = lambda a: jnp.asarray(a, F32).astype(BF16)
    return dict(s1_half=bf(outer_fwd(n1h)), s1_full=bf(outer_fwd(FFT_N1)), g=bf(g), ginv=bf(ginv),
                i2=bf(np.kron(finv, eye)))


def _outer_forward(src_ref, s1_ref, dst_ref, n1_rows):
    def body(n2h, carry):
        r0 = pl.multiple_of(n2h * SUBLANES, SUBLANES)
        tiles = [src_ref[pl.ds(n1 * FFT_N2 + r0, SUBLANES), :] for n1 in range(n1_rows)]
        rhs = jnp.concatenate(tiles, axis=0).astype(BF16)
        out = jnp.dot(s1_ref[...], rhs, preferred_element_type=F32)
        for k1 in range(FFT_K1):
            for ri in range(2):
                row = (2 * k1 + ri) * SUBLANES
                dst_ref[k1, pl.ds(ri * FFT_N2 + r0, SUBLANES), :] = out[row:row + SUBLANES, :]
        return carry
    lax.fori_loop(0, FFT_N2 // SUBLANES, body, 0, unroll=OUTER_UNROLL)


def _filter_spectrum_kernel(k_ref, s1_ref, g_ref, kf_ref, a_ref):
    _outer_forward(k_ref, s1_ref, a_ref, FFT_N1)

    def body(k1, carry):
        kf_ref[k1] = jnp.dot(g_ref[k1], a_ref[k1].astype(BF16), preferred_element_type=F32)
        return carry
    lax.fori_loop(0, FFT_K1, body, 0)


def _filter_spectrum(kfull, tables, *, cb=MXU_TILE):
    e = kfull.shape[1]
    cb = min(cb, e)
    return pl.pallas_call(
        _filter_spectrum_kernel,
        grid=(e // cb,),
        in_specs=[pl.BlockSpec((FFT_N, cb), lambda c: (0, c)),
                  _resident(tables["s1_full"].shape, 1), _resident(tables["g"].shape, 1)],
        out_specs=pl.BlockSpec((FFT_K1, 2 * FFT_N2, cb), lambda c: (0, 0, c)),
        out_shape=jax.ShapeDtypeStruct((FFT_K1, 2 * FFT_N2, e), F32),
        scratch_shapes=[pltpu.VMEM((FFT_K1, 2 * FFT_N2, cb), F32)],
        compiler_params=_params(("arbitrary",)),
        name="filter_spectrum",
    )(kfull, tables["s1_full"], tables["g"])


def _short_conv(x_ref, cw_ref, cb_ref, group, c, stage_ref, n_rows):
    r0 = pl.multiple_of(c * SHORT_CHUNK, SHORT_CHUNK)
    prev0 = pl.multiple_of(jnp.maximum(r0 - BF16_ROWS, 0), BF16_ROWS)
    next0 = pl.multiple_of(jnp.minimum(r0 + SHORT_CHUNK, n_rows - BF16_ROWS), BF16_ROWS)
    prev = x_ref[pl.ds(prev0, BF16_ROWS), :].astype(F32)[BF16_ROWS - 1:BF16_ROWS, :] * jnp.where(c > 0, 1.0, 0.0)
    nxt = x_ref[pl.ds(next0, BF16_ROWS), :].astype(F32)[0:1, :] * jnp.where(r0 + SHORT_CHUNK < n_rows, 1.0, 0.0)
    stage_ref[STAGE_PAD - 1:STAGE_PAD, :] = prev
    stage_ref[STAGE_PAD:STAGE_PAD + SHORT_CHUNK, :] = x_ref[pl.ds(r0, SHORT_CHUNK), :].astype(F32)
    stage_ref[STAGE_PAD + SHORT_CHUNK:STAGE_PAD + SHORT_CHUNK + 1, :] = nxt
    w = cw_ref[group]
    return (cb_ref[group]
            + w[0:1] * stage_ref[STAGE_PAD - 1:STAGE_PAD - 1 + SHORT_CHUNK, :]
            + w[1:2] * stage_ref[STAGE_PAD:STAGE_PAD + SHORT_CHUNK, :]
            + w[2:3] * stage_ref[STAGE_PAD + 1:STAGE_PAD + 1 + SHORT_CHUNK, :])


def _long_conv_kernel(x0_ref, x1_ref, v_ref, zs_ref, cw_ref, cb_ref, hb_ref, kf_ref,
                      s1_ref, g_ref, ginv_ref, i2_ref, o_ref, w_ref, ab_ref, stage_ref):
    n_rows = x1_ref.shape[0]
    n_chunks = n_rows // SHORT_CHUNK

    def make_w(c, carry):
        rows = pl.ds(pl.multiple_of(c * SHORT_CHUNK, SHORT_CHUNK), SHORT_CHUNK)
        x1c = _short_conv(x1_ref, cw_ref, cb_ref, 1, c, stage_ref, n_rows)
        vc = _short_conv(v_ref, cw_ref, cb_ref, 2, c, stage_ref, n_rows)
        w_ref[rows, :] = x1c * vc
        return carry
    lax.fori_loop(0, n_chunks, make_w, 0)

    _outer_forward(w_ref, s1_ref, ab_ref, FFT_N1 // 2)

    def per_k1(k1, carry):
        x = jnp.dot(g_ref[k1], ab_ref[k1].astype(BF16), preferred_element_type=F32)
        kf = kf_ref[k1]
        xr, xi = x[:FFT_N2], x[FFT_N2:]
        kr, ki = kf[:FFT_N2], kf[FFT_N2:]
        y = jnp.concatenate([xr * kr - xi * ki, xr * ki + xi * kr], axis=0).astype(BF16)
        ab_ref[k1] = jnp.dot(ginv_ref[k1], y, preferred_element_type=F32)
        return carry
    lax.fori_loop(0, FFT_K1, per_k1, 0, unroll=K1_UNROLL)

    def outer_inverse(n2h, carry):
        r0 = pl.multiple_of(n2h * SUBLANES, SUBLANES)
        tiles = [ab_ref[k1, pl.ds(ri * FFT_N2 + r0, SUBLANES), :] for k1 in range(FFT_K1) for ri in range(2)
                 if 2 * k1 + ri not in _REAL_ONLY_COLS]
        rhs = jnp.concatenate(tiles, axis=0).astype(BF16)
        out = jnp.dot(i2_ref[...], rhs, preferred_element_type=F32)
        for n1 in range(FFT_N1 // 2):
            rows = pl.ds(n1 * FFT_N2 + r0, SUBLANES)
            w_ref[rows, :] = out[n1 * SUBLANES:(n1 + 1) * SUBLANES, :] + hb_ref[...] * w_ref[rows, :]
        return carry
    lax.fori_loop(0, FFT_N2 // SUBLANES, outer_inverse, 0, unroll=OUTER_UNROLL)

    def finish(c, carry):
        rows = pl.ds(pl.multiple_of(c * SHORT_CHUNK, SHORT_CHUNK), SHORT_CHUNK)
        x0c = _short_conv(x0_ref, cw_ref, cb_ref, 0, c, stage_ref, n_rows)
        o_ref[rows, :] = (x0c * w_ref[rows, :] * zs_ref[rows, :].astype(F32)).astype(o_ref.dtype)
        return carry
    lax.fori_loop(0, n_chunks, finish, 0)


def _long_conv(x0, x1, v, zs, conv_w, conv_b, h_bias, kf, tables, *, cb=MXU_TILE):
    bsz, seq, e = x0.shape
    assert seq == FFT_N // 2
    cb = min(cb, e)
    cw = conv_w.reshape(3, 3, e).transpose(1, 0, 2)
    cbias = conv_b.reshape(3, 1, e)
    tok = pl.BlockSpec((None, seq, cb), lambda c, b: (b, 0, c))
    consts = [tables["s1_half"], tables["g"], tables["ginv"], tables["i2"]]
    return pl.pallas_call(
        _long_conv_kernel,
        grid=(e // cb, bsz),
        in_specs=[tok, tok, tok, tok,
                  pl.BlockSpec((3, 3, cb), lambda c, b: (0, 0, c)),
                  pl.BlockSpec((3, 1, cb), lambda c, b: (0, 0, c)),
                  pl.BlockSpec((1, cb), lambda c, b: (0, c)),
                  pl.BlockSpec((FFT_K1, 2 * FFT_N2, cb), lambda c, b: (0, 0, c), pipeline_mode=pl.Buffered(1))]
                 + [_resident(a.shape, 2) for a in consts],
        out_specs=tok,
        out_shape=jax.ShapeDtypeStruct((bsz, seq, e), BF16),
        scratch_shapes=[pltpu.VMEM((seq, cb), F32),
                        pltpu.VMEM((FFT_K1, 2 * FFT_N2, cb), F32),
                        pltpu.VMEM((SHORT_CHUNK + 2 * STAGE_PAD, cb), F32)],
        compiler_params=_params(("arbitrary", "arbitrary")),
        name="long_conv",
    )(x0, x1, v, zs, cw, cbias, h_bias.reshape(1, e), kf, *consts)


def _final_norm_kernel(h_ref, g_ref, o_ref):
    x = h_ref[...]
    ms = jnp.mean(x * x, axis=-1, keepdims=True)
    o_ref[...] = x * lax.rsqrt(ms + EPS) * g_ref[...]


def _final_norm(h, g, *, tm=1024):
    bn, t, d = h.shape
    m = bn * t
    tm = min(tm, m)
    out = pl.pallas_call(
        _final_norm_kernel,
        grid=(m // tm,),
        in_specs=[pl.BlockSpec((tm, d), lambda i: (i, 0)),
                  pl.BlockSpec((1, d), lambda i: (0, 0))],
        out_specs=pl.BlockSpec((tm, d), lambda i: (i, 0)),
        out_shape=jax.ShapeDtypeStruct((m, d), F32),
        compiler_params=_params(("arbitrary",)),
        name="final_norm",
    )(h.reshape(m, d), g.reshape(1, d))
    return out.reshape(bn, t, d)


def kernel(x, c, ctx, c_ctx, norm_g, ada_w, ada_b, final_g, cv_w_in, cv_dw_w, cv_dw_b, cv_ln_g, cv_ln_b, cv_w_out, ml_w_in, ml_conv_w, ml_conv_b, ml_w_q, ml_w_k, ml_w_v, ml_w_o, ml_b_o, ml_w_gates, ml_b_gates, ml_mh_g, ml_skip, ml_w_out, hy_w_in, hy_conv_w, hy_conv_b, hy_f_w1, hy_f_b1, hy_f_freq1, hy_f_w2, hy_f_b2, hy_f_freq2, hy_f_w3, hy_f_b3, hy_h_bias, hy_w_out):
    depth = norm_g.shape[0]
    bsz, seq, d = x.shape
    n_ctx = ctx.shape[1]
    readers = [i for i in range(depth) if i % N_MIXERS == 1]
    last_reader = readers[-1] if readers else -1

    cond_rows = -(-(bsz + 1) // SUBLANES) * SUBLANES
    cond = jnp.concatenate([c, c_ctx[None], jnp.zeros((cond_rows - bsz - 1, d), F32)], axis=0)
    ada = _ada_params(cond, ada_w, ada_b)

    h_lat, h_ctx = x, ctx
    norm_fused = (depth - 1) % N_MIXERS != 1
    for i in range(depth):
        kind, j = i % N_MIXERS, i // N_MIXERS
        close_g = final_g if (norm_fused and i == depth - 1) else None
        ctx_in = i <= last_reader
        ctx_out = i < last_reader
        sh, sc, gt = jnp.split(ada[i, :bsz], 3, axis=-1)
        sh_c, sc_c, gt_c = (jnp.broadcast_to(a, (bsz, d)) for a in jnp.split(ada[i, bsz:bsz + 1], 3, axis=-1))
        if kind == 0:
            w_in = cv_w_in[j].astype(BF16)
            w_out = cv_w_out[j].astype(BF16)
            streams = [(h_lat, sh, sc, gt, GRID_W, _conv_mid_dft)]
            if ctx_out:
                streams.append((h_ctx, sh_c, sc_c, gt_c, n_ctx, _conv_mid))
            new = []
            for h, s_h, s_c, g_t, seg, conv_mid in streams:
                y, zs = _in_proj(h, norm_g[i], s_h, s_c, w_in, n_groups=3, n_out=2, epilogue=_epilogue_conv,
                                 tm=2 * IN_PROJ_ROWS)
                yb = conv_mid(y, zs, cv_dw_w[j], cv_dw_b[j], cv_ln_g[j], cv_ln_b[j], seg=seg)
                new.append(_out_matmul(yb, w_out, h, g_t, final_g=close_g if h is h_lat else None))
            h_lat = new[0]
            if ctx_out:
                h_ctx = new[1]
        elif kind == 1:
            assert ctx_in and not ctx_out
            w_in = ml_w_in[j].astype(BF16)
            w_out = ml_w_out[j].astype(BF16)
            lead = (-n_ctx) % IN_PROJ_ROWS
            h_all = jnp.concatenate([jnp.zeros((bsz, lead, d), F32), h_ctx, h_lat], axis=1)
            sh_all = jnp.concatenate([sh, sh_c[:1]], axis=0)
            sc_all = jnp.concatenate([sc, sc_c[:1]], axis=0)
            xm, zs = _in_proj(h_all, norm_g[i], sh_all, sc_all, w_in, n_groups=2, n_out=2,
                              epilogue=_epilogue_gate_last, n_ctx=lead + n_ctx, tm=IN_PROJ_ROWS, tn=1024)
            q, k, kt, v, o, xc, gc, gr = _mlstm_pre(
                xm, ml_conv_w[j], ml_conv_b[j], ml_w_q[j], ml_w_k[j], ml_w_v[j], ml_w_o[j], ml_b_o[j],
                ml_w_gates[j], ml_b_gates[j], lead=lead, n_ctx=n_ctx, n_heads=MLSTM_HEADS)
            hs = _mlstm_scan(q, k, kt, v, gc, gr, n_heads=MLSTM_HEADS, n_ctx=n_ctx)
            h_lat = _mlstm_out(hs, o, xc, zs, ml_mh_g[j], ml_skip[j], w_out, h_lat, gt,
                               lead=lead, n_ctx=n_ctx, n_heads=MLSTM_HEADS)
        else:
            assert not ctx_out
            w_in = hy_w_in[j].astype(BF16)
            w_out = hy_w_out[j].astype(BF16)
            x0, x1, v, zs = _in_proj(h_lat, norm_g[i], sh, sc, w_in, n_groups=4, n_out=4,
                                     epilogue=_epilogue_gate_last, tm=2 * IN_PROJ_ROWS, tn=256)
            tables = _dft_tables()
            kfull = _hyena_filter(seq, hy_f_w1[j], hy_f_b1[j], hy_f_freq1[j], hy_f_w2[j], hy_f_b2[j],
                                  hy_f_freq2[j], hy_f_w3[j], hy_f_b3[j])
            kf = _filter_spectrum(kfull, tables)
            yb = _long_conv(x0, x1, v, zs, hy_conv_w[j], hy_conv_b[j], hy_h_bias[j], kf, tables)
            h_lat = _out_matmul(yb, w_out, h_lat, gt, final_g=close_g)
    return h_lat if norm_fused else _final_norm(h_lat, final_g)
```

```python
import functools
import math

import jax
import jax.numpy as jnp
import numpy as np
from jax import lax
from jax.experimental import pallas as pl
from jax.experimental.pallas import tpu as pltpu

GRID_W = 64
N_MIXERS = 3
EPS = 1e-6
MLSTM_HEADS = 8
HYENA_EMB_DIM = 33
HYENA_FAST_DECAY = 0.3
HYENA_SLOW_DECAY = 1.5
HYENA_DECAY_TARGET = 1e-2

V7X_VMEM_LIMIT_BYTES = 56 * 1024 * 1024
SUBLANES = 8
BF16_ROWS = 16
LANES = 128
MXU_TILE = 256

F32 = jnp.float32
BF16 = jnp.bfloat16


def _tiles(m, n, tm, tn):
    tm, tn = min(tm, m), min(tn, n)
    assert m % tm == 0 and n % tn == 0, (m, n, tm, tn)
    return tm, tn


def _sigmoid(x):
    return 1.0 / (1.0 + jnp.exp(-x))


def _silu(x):
    return x * _sigmoid(x)


def _dot_bf16(a, b):
    return jnp.dot(a.astype(BF16), b.astype(BF16), preferred_element_type=F32)


def _params(semantics):
    return pltpu.CompilerParams(dimension_semantics=semantics,
                                vmem_limit_bytes=V7X_VMEM_LIMIT_BYTES)


def _resident(shape, n_grid):
    zeros = (0,) * len(shape)
    return pl.BlockSpec(shape, lambda *_: zeros, pipeline_mode=pl.Buffered(1))


def _ada_kernel(c_ref, w_ref, b_ref, o_ref):
    o_ref[0] = _dot_bf16(_silu(c_ref[...]), w_ref[0]) + b_ref[0]


def _ada_params(cond, ada_w, ada_b, *, tn=512):
    depth, d, n = ada_w.shape
    r = cond.shape[0]
    return pl.pallas_call(
        _ada_kernel,
        grid=(depth, n // tn),
        in_specs=[pl.BlockSpec((r, d), lambda i, j: (0, 0)),
                  pl.BlockSpec((1, d, tn), lambda i, j: (i, 0, j)),
                  pl.BlockSpec((1, 1, tn), lambda i, j: (i, 0, j))],
        out_specs=pl.BlockSpec((1, r, tn), lambda i, j: (i, 0, j)),
        out_shape=jax.ShapeDtypeStruct((depth, r, n), F32),
        compiler_params=_params(("arbitrary", "arbitrary")),
        name="ada_params",
    )(cond, ada_w, ada_b.reshape(depth, 1, n))


def _epilogue_conv(a, g, z):
    return a * _sigmoid(g), _silu(z)


def _epilogue_gate_last(*accs):
    return accs[:-1] + (_silu(accs[-1]),)


def _in_proj_kernel(*refs, n_groups, epilogue):
    h_ref, g_ref, sh_ref, sc_ref = refs[:4]
    w_refs = refs[4:4 + n_groups]
    o_refs = refs[4 + n_groups:-1]
    u_ref = refs[-1]

    @pl.when(pl.program_id(1) == 0)
    def _():
        x = h_ref[...]
        ms = jnp.mean(x * x, axis=-1, keepdims=True)
        y = x * lax.rsqrt(ms + EPS) * g_ref[...]
        u_ref[...] = (y * (1.0 + sc_ref[0]) + sh_ref[0]).astype(u_ref.dtype)

    u = u_ref[...]
    outs = epilogue(*[jnp.dot(u, w[...], preferred_element_type=F32) for w in w_refs])
    for o_ref, val in zip(o_refs, outs, strict=True):
        o_ref[...] = val.astype(o_ref.dtype)


IN_PROJ_ROWS = 512


def _in_proj(h, g, sh, sc, w, *, n_groups, n_out, epilogue, n_ctx=0, tm=IN_PROJ_ROWS, tn=512):
    bn, t, d = h.shape
    e = w.shape[1] // n_groups
    tm, tn = _tiles(t, e, tm, tn)
    assert n_ctx % tm == 0
    tiles_per_seq, ctx_tiles = t // tm, n_ctx // tm
    col_tiles = e // tn
    n_mod = sh.shape[0]

    def mod_row(i, j):
        return (jnp.where(i % tiles_per_seq < ctx_tiles, n_mod - 1, i // tiles_per_seq), 0, 0)

    w_specs = [pl.BlockSpec((d, tn), functools.partial(lambda i, j, k: (0, k * col_tiles + j), k=k))
               for k in range(n_groups)]
    outs = pl.pallas_call(
        functools.partial(_in_proj_kernel, n_groups=n_groups, epilogue=epilogue),
        grid=(bn * tiles_per_seq, col_tiles),
        in_specs=[
            pl.BlockSpec((tm, d), lambda i, j: (i, 0)),
            pl.BlockSpec((1, d), lambda i, j: (0, 0)),
            pl.BlockSpec((1, 1, d), mod_row),
            pl.BlockSpec((1, 1, d), mod_row),
        ] + w_specs,
        out_specs=[pl.BlockSpec((tm, tn), lambda i, j: (i, j))] * n_out,
        out_shape=[jax.ShapeDtypeStruct((bn * t, e), BF16)] * n_out,
        scratch_shapes=[pltpu.VMEM((tm, d), BF16)],
        compiler_params=_params(("arbitrary", "arbitrary")),
        name="in_proj",
    )(h.reshape(bn * t, d), g.reshape(1, d), sh.reshape(n_mod, 1, d),
      sc.reshape(n_mod, 1, d), *([w] * n_groups))
    return [o.reshape(bn, t, e) for o in outs]


def _out_matmul_kernel(y_ref, w_ref, h_ref, gt_ref, o_ref):
    acc = jnp.dot(y_ref[...], w_ref[...], preferred_element_type=F32)
    o_ref[...] = h_ref[...] + gt_ref[0] * acc


def _out_matmul_norm_kernel(y_ref, w_ref, h_ref, gt_ref, g_ref, o_ref):
    acc = jnp.dot(y_ref[...], w_ref[...], preferred_element_type=F32)
    x = h_ref[...] + gt_ref[0] * acc
    ms = jnp.mean(x * x, axis=-1, keepdims=True)
    o_ref[...] = x * lax.rsqrt(ms + EPS) * g_ref[...]


def _out_matmul(y, w, h, gt, *, final_g=None, tm=512):
    bn, t, e = y.shape
    d = w.shape[1]
    tm, tn = _tiles(t, d, tm, d)
    tiles_per_seq = t // tm
    in_specs = [
        pl.BlockSpec((tm, e), lambda i, j: (i, 0)),
        _resident((e, d), 2),
        pl.BlockSpec((tm, tn), lambda i, j: (i, j)),
        pl.BlockSpec((1, 1, tn), lambda i, j: (i // tiles_per_seq, 0, j)),
    ]
    args = [y.reshape(bn * t, e), w, h.reshape(bn * t, d), gt.reshape(bn, 1, d)]
    if final_g is not None:
        in_specs.append(_resident((1, d), 2))
        args.append(final_g.reshape(1, d))
    out = pl.pallas_call(
        _out_matmul_kernel if final_g is None else _out_matmul_norm_kernel,
        grid=(bn * tiles_per_seq, d // tn),
        in_specs=in_specs,
        out_specs=pl.BlockSpec((tm, tn), lambda i, j: (i, j)),
        out_shape=jax.ShapeDtypeStruct((bn * t, d), F32),
        compiler_params=_params(("arbitrary", "arbitrary")),
        name="out_matmul",
    )(*args)
    return out.reshape(bn, t, d)


CONV_PAD = 16
LN_UNROLL = 16


def _conv_mid_kernel(y_ref, zs_ref, w_ref, b_ref, lg_ref, lb_ref, o_ref, xp_ref, xs_ref, cv_ref, *, seg, width):
    tb, e = y_ref.shape
    n_seg = tb // seg
    half = width // 2
    rows_s = xs_ref.shape[2]
    rb = SUBLANES
    zeros = jnp.zeros((CONV_PAD, e), F32)
    for s in range(n_seg):
        xp_ref[s, 0:CONV_PAD, :] = zeros
        xp_ref[s, CONV_PAD:CONV_PAD + seg, :] = y_ref[s * seg:(s + 1) * seg, :].astype(F32)
        xp_ref[s, CONV_PAD + seg:2 * CONV_PAD + seg, :] = zeros

    def lane_block(cb, carry):
        lanes = pl.ds(pl.multiple_of(cb * LANES, LANES), LANES)
        for sh in range(rb):
            for s in range(n_seg):
                xs_ref[sh, s] = xp_ref[s, sh:sh + rows_s, lanes]
        taps = [w_ref[k:k + 1, lanes] for k in range(width)]
        bias = b_ref[:, lanes]
        for s in range(n_seg):
            for r in range(seg // rb):
                acc = bias
                for k in range(width):
                    off = CONV_PAD - half + k
                    base = (off // rb) * rb + r * rb
                    acc = acc + taps[k] * xs_ref[off % rb, s, base:base + rb, :]
                cv_ref[s * seg + r * rb:s * seg + (r + 1) * rb, lanes] = acc
        return carry

    lax.fori_loop(0, e // LANES, lane_block, 0)

    def row_block(i, carry):
        rows = pl.ds(pl.multiple_of(i * rb, rb), rb)
        cv = cv_ref[rows, :]
        mu = jnp.mean(cv, axis=-1, keepdims=True)
        xc = cv - mu
        var = jnp.mean(xc * xc, axis=-1, keepdims=True)
        yn = xc * lax.rsqrt(var + EPS) * lg_ref[...] + lb_ref[...]
        o_ref[rows, :] = (_silu(yn) * zs_ref[rows, :].astype(F32)).astype(o_ref.dtype)
        return carry

    lax.fori_loop(0, tb // rb, row_block, 0, unroll=LN_UNROLL)


def _conv_mid(y, zs, dw_w, dw_b, ln_g, ln_b, *, seg, tb=256):
    bn, t, e = y.shape
    width = dw_w.shape[0]
    m = bn * t
    tb = min(tb, m)
    assert m % tb == 0 and tb % seg == 0 and t % seg == 0 and width // 2 <= CONV_PAD
    row = lambda a: a.reshape(1, e)
    tok = pl.BlockSpec((tb, e), lambda i: (i, 0))
    vec = pl.BlockSpec((1, e), lambda i: (0, 0))
    out = pl.pallas_call(
        functools.partial(_conv_mid_kernel, seg=seg, width=width),
        grid=(m // tb,),
        in_specs=[tok, tok, pl.BlockSpec((width, e), lambda i: (0, 0)), vec, vec, vec],
        out_specs=tok,
        out_shape=jax.ShapeDtypeStruct((m, e), BF16),
        scratch_shapes=[pltpu.VMEM((tb // seg, seg + 2 * CONV_PAD, e), F32),
                        pltpu.VMEM((SUBLANES, tb // seg, seg + 2 * CONV_PAD - SUBLANES, LANES), F32),
                        pltpu.VMEM((tb, e), F32)],
        compiler_params=_params(("arbitrary",)),
        name="conv_mid",
    )(y.reshape(m, e), zs.reshape(m, e), dw_w, row(dw_b), row(ln_g), row(ln_b))
    return out.reshape(bn, t, e)


CONV_DFT = 96
CONV_FREQS = CONV_DFT // 2 + 1
CONV_HALF = -(-CONV_FREQS // SUBLANES) * SUBLANES
CONV_LANE_UNROLL = 8


def _conv_dft_tables(seg, width, n_seg):
    f = np.arange(CONV_FREQS)
    th = 2 * np.pi * np.outer(f, np.arange(seg)) / CONV_DFT
    fwd = np.zeros((2 * CONV_HALF, seg))
    fwd[:CONV_FREQS] = np.cos(th)
    fwd[CONV_HALF:CONV_HALF + CONV_FREQS] = -np.sin(th)
    c = np.where((f == 0) | (f == CONV_DFT // 2), 1.0, 2.0) / CONV_DFT
    inv = np.zeros((seg, 2 * CONV_HALF))
    inv[:, :CONV_FREQS] = (c[:, None] * np.cos(th)).T
    inv[:, CONV_HALF:CONV_HALF + CONV_FREQS] = (-c[:, None] * np.sin(th)).T
    eye = np.eye(n_seg)
    thw = 2 * np.pi * np.outer(f, np.arange(width) - width // 2) / CONV_DFT
    wtab = np.zeros((2 * CONV_HALF, width))
    wtab[:CONV_FREQS] = np.cos(thw)
    wtab[CONV_HALF:CONV_HALF + CONV_FREQS] = np.sin(thw)
    bf = lambda a: jnp.asarray(a, F32).astype(BF16)
    return bf(np.kron(eye, fwd)), bf(np.kron(eye, inv)), jnp.asarray(wtab, F32)


def _conv_dft_kernel(y_ref, zs_ref, w_ref, b_ref, lg_ref, lb_ref, fwd_ref, inv_ref, wtab_ref, o_ref,
                     h_ref, cv_ref, *, n_seg):
    tb, e = y_ref.shape
    rb = SUBLANES

    @pl.when(pl.program_id(0) == 0)
    def _():
        h_ref[...] = jnp.dot(wtab_ref[...], w_ref[...], preferred_element_type=F32,
                             precision=lax.Precision.HIGHEST)

    def lane_block(cb, carry):
        lanes = pl.ds(pl.multiple_of(cb * MXU_TILE, MXU_TILE), MXU_TILE)
        x = jnp.dot(fwd_ref[...], y_ref[:, lanes], preferred_element_type=F32)
        hr = h_ref[0:CONV_HALF, lanes]
        hi = h_ref[CONV_HALF:2 * CONV_HALF, lanes]
        parts = []
        for s in range(n_seg):
            xr = x[s * 2 * CONV_HALF:s * 2 * CONV_HALF + CONV_HALF]
            xi = x[s * 2 * CONV_HALF + CONV_HALF:(s + 1) * 2 * CONV_HALF]
            parts += [xr * hr - xi * hi, xr * hi + xi * hr]
        yf = jnp.concatenate(parts, axis=0).astype(BF16)
        cv_ref[:, lanes] = jnp.dot(inv_ref[...], yf, preferred_element_type=F32) + b_ref[:, lanes]
        return carry

    lax.fori_loop(0, e // MXU_TILE, lane_block, 0, unroll=CONV_LANE_UNROLL)

    def row_block(i, carry):
        rows = pl.ds(pl.multiple_of(i * rb, rb), rb)
        cv = cv_ref[rows, :]
        mu = jnp.mean(cv, axis=-1, keepdims=True)
        xc = cv - mu
        var = jnp.mean(xc * xc, axis=-1, keepdims=True)
        yn = xc * lax.rsqrt(var + EPS) * lg_ref[...] + lb_ref[...]
        o_ref[rows, :] = (_silu(yn) * zs_ref[rows, :].astype(F32)).astype(o_ref.dtype)
        return carry

    lax.fori_loop(0, tb // rb, row_block, 0, unroll=LN_UNROLL)


def _conv_mid_dft(y, zs, dw_w, dw_b, ln_g, ln_b, *, seg, tb=256):
    bn, t, e = y.shape
    width = dw_w.shape[0]
    m = bn * t
    tb = min(tb, m)
    assert m % tb == 0 and tb % seg == 0 and t % seg == 0 and seg + width - 1 <= CONV_DFT
    n_seg = tb // seg
    fwd, inv, wtab = _conv_dft_tables(seg, width, n_seg)
    row = lambda a: a.reshape(1, e)
    tok = pl.BlockSpec((tb, e), lambda i: (i, 0))
    out = pl.pallas_call(
        functools.partial(_conv_dft_kernel, n_seg=n_seg),
        grid=(m // tb,),
        in_specs=[tok, tok, _resident((width, e), 1), _resident((1, e), 1), _resident((1, e), 1),
                  _resident((1, e), 1), _resident(fwd.shape, 1), _resident(inv.shape, 1),
                  _resident(wtab.shape, 1)],
        out_specs=tok,
        out_shape=jax.ShapeDtypeStruct((m, e), BF16),
        scratch_shapes=[pltpu.VMEM((2 * CONV_HALF, e), F32), pltpu.VMEM((tb, e), F32)],
        compiler_params=_params(("arbitrary",)),
        name="conv_mid_dft",
    )(y.reshape(m, e), zs.reshape(m, e), dw_w, row(dw_b), row(ln_g), row(ln_b), fwd, inv, wtab)
    return out.reshape(bn, t, e)


STAGE_PAD = 8


def _expand_block_diag(w):
    g, bi, bo = w.shape
    per = MXU_TILE // bi
    wt = w.reshape(g // per, per, bi, bo)
    eye = jnp.eye(per, dtype=w.dtype)
    return jnp.einsum('tgio,gh->tgiho', wt, eye).reshape(g // per, per * bi, per * bo)


def _mlstm_pre_kernel(xm_ref, prev_ref, next_ref, cw_ref, cb_ref, wq_ref, wk_ref, wkt_ref, wv_ref, wo_ref,
                      bo_ref, wg_ref, bg_ref,
                      q_ref, k_ref, kt_ref, v_ref, o_ref, xc_ref, gc_ref, gr_ref, stage_ref,
                      *, tiles_per_seq, ctx_tiles, n_heads):
    tb, e = xm_ref.shape
    n_g = gc_ref.shape[1]
    q_scale = (e // n_heads) ** -0.5
    t = pl.program_id(0) % tiles_per_seq
    has_prev = jnp.logical_and(t != 0, t != ctx_tiles).astype(F32)
    has_next = jnp.logical_and(t != ctx_tiles - 1, t != tiles_per_seq - 1).astype(F32)
    gc = jnp.zeros(gc_ref.shape, F32) + bg_ref[...]
    nt = (((1,), (1,)), ((), ()))
    for j in range(e // MXU_TILE):
        lanes = slice(j * MXU_TILE, (j + 1) * MXU_TILE)
        xm = xm_ref[:, lanes]
        stage_ref[STAGE_PAD - 1:STAGE_PAD, :] = \
            prev_ref[:, lanes].astype(F32)[BF16_ROWS - 1:BF16_ROWS, :] * has_prev
        stage_ref[STAGE_PAD:STAGE_PAD + tb, :] = xm.astype(F32)
        stage_ref[STAGE_PAD + tb:STAGE_PAD + tb + 1, :] = next_ref[:, lanes].astype(F32)[0:1, :] * has_next
        cw = cw_ref[:, lanes]
        pre = (cb_ref[:, lanes] + cw[0:1] * stage_ref[STAGE_PAD - 1:STAGE_PAD - 1 + tb, :]
               + cw[1:2] * stage_ref[STAGE_PAD:STAGE_PAD + tb, :]
               + cw[2:3] * stage_ref[STAGE_PAD + 1:STAGE_PAD + 1 + tb, :])
        xcb = _silu(pre).astype(BF16)
        xc_ref[:, lanes] = xcb
        q = jnp.dot(xcb, wq_ref[j], preferred_element_type=F32)
        k = jnp.dot(xcb, wk_ref[j], preferred_element_type=F32)
        v = jnp.dot(xm, wv_ref[j], preferred_element_type=F32)
        o = jnp.dot(xcb, wo_ref[j], preferred_element_type=F32) + bo_ref[:, lanes]
        kt_ref[lanes, :] = lax.dot_general(wkt_ref[j], xcb, nt, preferred_element_type=F32).astype(BF16)
        qb, kb, vb = q.astype(BF16), k.astype(BF16), v.astype(BF16)
        q_ref[:, lanes] = (q * q_scale).astype(BF16)
        k_ref[:, lanes] = kb
        v_ref[:, lanes] = vb
        o_ref[:, lanes] = _sigmoid(o).astype(BF16)
        for i, a in enumerate((qb, kb, vb)):
            gc = gc + jnp.dot(a, wg_ref[i, lanes, :], preferred_element_type=F32)
    col = lax.broadcasted_iota(jnp.int32, gc.shape, 1)
    gc = jnp.where((col // n_heads) % 2 == 1, _log_sigmoid(gc), gc)
    gc_ref[...] = gc
    gc_wide = jnp.concatenate([gc, jnp.zeros((tb, LANES - n_g), F32)], axis=1)
    gr_ref[...] = gc_wide.T[:n_g, :]


def _mlstm_pre(xm, conv_w, conv_b, w_q, w_k, w_v, w_o, b_o, w_gates, b_gates, *, lead, n_ctx, n_heads, tb=256):
    bsz, t_all, e = xm.shape
    t = t_all - lead
    assert t % tb == 0 and n_ctx % tb == 0 and lead % tb == 0
    tiles_per_seq, ctx_tiles = t // tb, n_ctx // tb
    tiles_all, lead_tiles = t_all // tb, lead // tb
    m = bsz * t
    n_g = w_gates.shape[1]
    hpt = tb // BF16_ROWS
    n_halo = bsz * t_all // BF16_ROWS

    def src_tile(i):
        return (i // tiles_per_seq) * tiles_all + lead_tiles + i % tiles_per_seq
    bd = lambda w: _expand_block_diag(w).astype(BF16)
    wq, wk, wv, wo = bd(w_q), bd(w_k), bd(w_v), bd(w_o)
    wkt = jnp.swapaxes(wk, 1, 2)
    wg = w_gates.reshape(3, e, n_g).astype(BF16)
    tok = pl.BlockSpec((tb, e), lambda i: (i, 0))
    row = lambda a: a.reshape(1, -1)
    consts = [conv_w, row(conv_b), wq, wk, wkt, wv, wo, row(b_o), wg, row(b_gates)]
    sds = jax.ShapeDtypeStruct
    xm2 = xm.reshape(bsz * t_all, e)
    outs = pl.pallas_call(
        functools.partial(_mlstm_pre_kernel, tiles_per_seq=tiles_per_seq, ctx_tiles=ctx_tiles,
                          n_heads=n_heads),
        grid=(m // tb,),
        in_specs=[pl.BlockSpec((tb, e), lambda i: (src_tile(i), 0)),
                  pl.BlockSpec((BF16_ROWS, e), lambda i: (jnp.maximum(src_tile(i) * hpt - 1, 0), 0)),
                  pl.BlockSpec((BF16_ROWS, e), lambda i: (jnp.minimum((src_tile(i) + 1) * hpt, n_halo - 1), 0))]
                 + [_resident(a.shape, 1) for a in consts],
        out_specs=[tok, tok,
                   pl.BlockSpec((None, e, tb), lambda i: (i // tiles_per_seq, 0, i % tiles_per_seq)),
                   tok, tok, tok,
                   pl.BlockSpec((tb, n_g), lambda i: (i, 0)),
                   pl.BlockSpec((None, n_g, tb), lambda i: (i // tiles_per_seq, 0, i % tiles_per_seq))],
        out_shape=[sds((m, e), BF16), sds((m, e), BF16), sds((bsz, e, t), BF16), sds((m, e), BF16),
                   sds((m, e), BF16), sds((m, e), BF16), sds((m, n_g), F32), sds((bsz, n_g, t), F32)],
        scratch_shapes=[pltpu.VMEM((tb + 2 * STAGE_PAD, MXU_TILE), F32)],
        compiler_params=_params(("arbitrary",)),
        name="mlstm_pre",
    )(xm2, xm2, xm2, *consts)
    q, k, kt, v, o, xc, gc, gr = outs
    r3 = lambda a: a.reshape(bsz, t, -1)
    return r3(q), r3(k), kt, r3(v), r3(o), r3(xc), r3(gc), gr


SCAN_CHUNK = 256


def _log_sigmoid(x):
    return jnp.minimum(x, 0.0) - jnp.log1p(jnp.exp(-jnp.abs(x)))


SCAN_HEADS_PER_STEP = 4


def _split_bf16(x):
    hi = x.astype(BF16)
    return hi, (x - hi.astype(F32)).astype(BF16)


def _mlstm_scan_kernel(q_ref, k_ref, kt_ref, v_ref, gc_ref, gr_ref, o_ref,
                       c_ref, cb_ref, n_ref, m_ref, *, n_heads, heads_per_step):
    direction = pl.program_id(0)

    @pl.when(pl.program_id(3) == 0)
    def _():
        c_ref[...] = jnp.zeros_like(c_ref)
        cb_ref[...] = jnp.zeros_like(cb_ref)
        n_ref[...] = jnp.zeros_like(n_ref)
        m_ref[...] = jnp.full_like(m_ref, -1e30)

    L = q_ref.shape[1]
    dh = q_ref.shape[2] // heads_per_step
    n_g = gc_ref.shape[2]
    gc = gc_ref[0]
    gr = gr_ref[0]
    lane = lax.broadcasted_iota(jnp.int32, (L, n_g), 1)
    sub = lax.broadcasted_iota(jnp.int32, (n_g, L), 0)
    sign = 1 - 2 * direction
    diff = (lax.broadcasted_iota(jnp.int32, (L, L), 0)
            - lax.broadcasted_iota(jnp.int32, (L, L), 1)) * sign
    causal = diff >= 0
    anti = diff <= 0

    for hh in range(heads_per_step):
        lanes = slice(hh * dh, (hh + 1) * dh)
        head = pl.program_id(2) * heads_per_step + hh
        col_i = direction * (2 * n_heads) + head
        col_f = col_i + n_heads
        lf_col = jnp.sum(jnp.where(lane == col_f, gc, 0.0), axis=1, keepdims=True)
        li_row = jnp.sum(jnp.where(sub == col_i, gr, 0.0), axis=0, keepdims=True)
        lf_row = jnp.sum(jnp.where(sub == col_f, gr, 0.0), axis=0, keepdims=True)
        b_col = jnp.sum(jnp.where(causal, lf_row, 0.0), axis=1, keepdims=True)
        b_row = jnp.sum(jnp.where(anti, lf_col, 0.0), axis=0, keepdims=True)
        total = jnp.sum(lf_row, axis=1, keepdims=True)

        m = m_ref[hh]
        log_d = jnp.where(causal, b_col - b_row + li_row, -jnp.inf)
        log_inter = b_col + m
        m_t = jnp.maximum(log_inter, jnp.max(log_d, axis=1, keepdims=True))
        dmat = jnp.exp(log_d - m_t)
        a = jnp.exp(log_inter - m_t)

        q = q_ref[0, :, lanes]
        kt = kt_ref[0, lanes, :]
        v = v_ref[0, :, lanes]
        s = jnp.dot(q, kt, preferred_element_type=F32) * dmat
        num = a * jnp.dot(q, cb_ref[hh], preferred_element_type=F32) \
            + jnp.dot(s.astype(BF16), v, preferred_element_type=F32)
        n_hi, n_lo = _split_bf16(n_ref[hh])
        qn2 = lax.dot_general(q, jnp.concatenate([n_hi, n_lo], axis=0), (((1,), (1,)), ((), ())),
                              preferred_element_type=F32)
        qn = qn2[:, 0:1] + qn2[:, SUBLANES:SUBLANES + 1]
        den = a * qn + jnp.sum(s, axis=1, keepdims=True)
        o_ref[0, 0, :, lanes] = (num / jnp.maximum(jnp.abs(den), jnp.exp(-m_t))).astype(o_ref.dtype)

        log_w_row = total - b_row + li_row
        m_new = jnp.maximum(total + m, jnp.max(log_w_row, axis=1, keepdims=True))
        decay = jnp.exp(total + m - m_new)
        w_hi, w_lo = _split_bf16(jnp.exp(log_w_row - m_new))
        kwt = kt * w_hi
        c_new = decay * c_ref[hh] + jnp.dot(kwt, v, preferred_element_type=F32)
        c_ref[hh] = c_new
        cb_ref[hh] = c_new.astype(BF16)
        w2 = jnp.concatenate([jnp.broadcast_to(w_hi, (SUBLANES, L)), jnp.broadcast_to(w_lo, (SUBLANES, L))], axis=0)
        n_add = jnp.dot(w2, k_ref[0, :, lanes], preferred_element_type=F32)
        n_ref[hh] = decay * n_ref[hh] + n_add[0:SUBLANES] + n_add[SUBLANES:2 * SUBLANES]
        m_ref[hh] = m_new


def _mlstm_scan(q, k, kt, v, gc, gr, *, n_heads, n_ctx, chunk=SCAN_CHUNK, heads_per_step=SCAN_HEADS_PER_STEP):
    bsz, t, e = q.shape
    dh = e // n_heads
    hp = heads_per_step
    L = chunk
    assert t % L == 0 and n_ctx % L == 0 and n_heads % hp == 0
    nc, nc_ctx = t // L, n_ctx // L

    def cidx(d, c):
        rev = jnp.where(c < nc_ctx, nc_ctx - 1 - c, nc - 1 - c + nc_ctx)
        return jnp.where(d == 0, c, rev)

    tok = pl.BlockSpec((1, L, hp * dh), lambda d, b, h, c: (b, cidx(d, c), h))
    return pl.pallas_call(
        functools.partial(_mlstm_scan_kernel, n_heads=n_heads, heads_per_step=hp),
        grid=(2, bsz, n_heads // hp, nc),
        in_specs=[
            tok, tok,
            pl.BlockSpec((1, hp * dh, L), lambda d, b, h, c: (b, h, cidx(d, c))),
            tok,
            pl.BlockSpec((1, L, 4 * n_heads), lambda d, b, h, c: (b, cidx(d, c), 0)),
            pl.BlockSpec((1, 4 * n_heads, L), lambda d, b, h, c: (b, 0, cidx(d, c))),
        ],
        out_specs=pl.BlockSpec((1, 1, L, hp * dh), lambda d, b, h, c: (d, b, cidx(d, c), h)),
        out_shape=jax.ShapeDtypeStruct((2, bsz, t, e), BF16),
        scratch_shapes=[pltpu.VMEM((hp, dh, dh), F32), pltpu.VMEM((hp, dh, dh), BF16),
                        pltpu.VMEM((hp, SUBLANES, dh), F32), pltpu.VMEM((hp, 1, 1), F32)],
        compiler_params=_params(("arbitrary",) * 4),
        name="mlstm_scan",
    )(q, k, kt, v, gc, gr)


def _mlstm_out_kernel(hf_ref, hb_ref, o_ref, xc_ref, zs_ref, mh_ref, skip_ref, w_ref, res_ref, gt_ref,
                      out_ref, y_ref, *, n_heads):
    tm, e = o_ref.shape
    dh = e // n_heads
    for hd in range(n_heads):
        lanes = slice(hd * dh, (hd + 1) * dh)
        h = hf_ref[:, lanes].astype(F32) + hb_ref[:, lanes].astype(F32)
        mu = jnp.mean(h, axis=-1, keepdims=True)
        hc = h - mu
        var = jnp.mean(hc * hc, axis=-1, keepdims=True)
        hn = hc * lax.rsqrt(var + EPS) * mh_ref[:, lanes]
        y = (o_ref[:, lanes].astype(F32) * hn + skip_ref[:, lanes] * xc_ref[:, lanes].astype(F32)) \
            * zs_ref[:, lanes].astype(F32)
        y_ref[:, lanes] = y.astype(BF16)
    acc = jnp.dot(y_ref[...], w_ref[...], preferred_element_type=F32)
    out_ref[...] = res_ref[...] + gt_ref[0] * acc


def _mlstm_out(hs, o, xc, zs, mh_g, skip, w_out, h_lat, gt, *, lead, n_ctx, n_heads, tm=256):
    _, bsz, t, e = hs.shape
    n = t - n_ctx
    d = w_out.shape[1]
    assert n % tm == 0 and n_ctx % tm == 0 and lead % tm == 0
    tiles, off = n // tm, n_ctx // tm
    zs_off = off + lead // tm
    tok = pl.BlockSpec((None, tm, e), lambda i: (i // tiles, off + i % tiles, 0))
    zs_tok = pl.BlockSpec((None, tm, e), lambda i: (i // tiles, zs_off + i % tiles, 0))
    row = lambda a: a.reshape(1, -1)
    out = pl.pallas_call(
        functools.partial(_mlstm_out_kernel, n_heads=n_heads),
        grid=(bsz * tiles,),
        in_specs=[pl.BlockSpec((None, None, tm, e), lambda i: (0, i // tiles, off + i % tiles, 0)),
                  pl.BlockSpec((None, None, tm, e), lambda i: (1, i // tiles, off + i % tiles, 0)),
                  tok, tok, zs_tok, _resident((1, e), 1), _resident((1, e), 1), _resident((e, d), 1),
                  pl.BlockSpec((tm, d), lambda i: (i, 0)),
                  pl.BlockSpec((1, 1, d), lambda i: (i // tiles, 0, 0))],
        out_specs=pl.BlockSpec((tm, d), lambda i: (i, 0)),
        out_shape=jax.ShapeDtypeStruct((bsz * n, d), F32),
        scratch_shapes=[pltpu.VMEM((tm, e), BF16)],
        compiler_params=_params(("arbitrary",)),
        name="mlstm_out",
    )(hs, hs, o, xc, zs, row(mh_g), row(skip), w_out, h_lat.reshape(bsz * n, d), gt.reshape(bsz, 1, d))
    return out.reshape(bsz, n, d)


def _filter_features(seq):
    r = np.arange(2 * seq)
    lag = np.where(r < seq, r, 2 * seq - r).clip(0, seq - 1).astype(np.float64)
    t = lag / (seq - 1)
    bands = (HYENA_EMB_DIM - 1) // 2
    ang = 2.0 * math.pi * lag / seq
    fr = np.linspace(1e-4, bands - 1, bands)
    feat = np.concatenate([t[:, None], np.cos(fr[None] * ang[:, None]), -np.sin(fr[None] * ang[:, None])], axis=-1)
    return jnp.asarray(feat, F32), jnp.asarray(t[:, None], F32)


def _filter_kernel(feat_ref, t_ref, w1_ref, b1_ref, f1_ref, w2_ref, b2_ref, f2_ref, w3_ref, b3_ref, dl_ref,
                   o_ref, *, seq):
    tr = o_ref.shape[0]
    h = jnp.sin(f1_ref[...] * (_dot_bf16(feat_ref[...], w1_ref[...]) + b1_ref[...]))
    h = jnp.sin(f2_ref[...] * (_dot_bf16(h, w2_ref[...]) + b2_ref[...]))
    h = _dot_bf16(h, w3_ref[...]) + b3_ref[...]
    out = h * jnp.exp(-t_ref[...] * dl_ref[...])
    row = pl.program_id(0) * tr + lax.broadcasted_iota(jnp.int32, (tr, 1), 0)
    o_ref[...] = jnp.where(row == seq, 0.0, out)


def _hyena_filter(seq, f_w1, f_b1, f_freq1, f_w2, f_b2, f_freq2, f_w3, f_b3, *, tr=512, tn=1024):
    e = f_w3.shape[1] // 2
    feat, t = _filter_features(seq)
    lo = math.log(HYENA_DECAY_TARGET) / HYENA_FAST_DECAY
    hi = math.log(HYENA_DECAY_TARGET) / HYENA_SLOW_DECAY
    deltas = jnp.asarray(np.abs(np.linspace(lo, hi, e)), F32).reshape(1, e)
    fdim = f_w1.shape[1]
    row_tiles, col_tiles = 2 * seq // tr, e // tn
    half = lambda i: i // (row_tiles // 2)
    row = lambda a: a.reshape(1, -1)
    small = lambda shape: pl.BlockSpec(shape, lambda i, j: (0, 0))
    return pl.pallas_call(
        functools.partial(_filter_kernel, seq=seq),
        grid=(row_tiles, col_tiles),
        in_specs=[pl.BlockSpec((tr, HYENA_EMB_DIM), lambda i, j: (i, 0)),
                  pl.BlockSpec((tr, 1), lambda i, j: (i, 0)),
                  small((HYENA_EMB_DIM, fdim)), small((1, fdim)), small((1, fdim)),
                  small((fdim, fdim)), small((1, fdim)), small((1, fdim)),
                  pl.BlockSpec((fdim, tn), lambda i, j: (0, half(i) * col_tiles + j)),
                  pl.BlockSpec((1, tn), lambda i, j: (0, half(i) * col_tiles + j)),
                  pl.BlockSpec((1, tn), lambda i, j: (0, j))],
        out_specs=pl.BlockSpec((tr, tn), lambda i, j: (i, j)),
        out_shape=jax.ShapeDtypeStruct((2 * seq, e), F32),
        compiler_params=_params(("arbitrary", "arbitrary")),
        name="hyena_filter",
    )(feat, t, f_w1, row(f_b1), row(f_freq1), f_w2, row(f_b2), row(f_freq2), f_w3, row(f_b3), deltas)


FFT_N1 = 64
FFT_N2 = 128
FFT_N = FFT_N1 * FFT_N2
FFT_K1 = FFT_N1 // 2 + 1
SHORT_CHUNK = 256
K1_UNROLL = 11
OUTER_UNROLL = 8
_REAL_ONLY_COLS = (1, 2 * (FFT_K1 - 1) + 1)


def _dft_tables():
    n1h = FFT_N1 // 2
    k1 = np.arange(FFT_K1)
    eye = np.eye(SUBLANES)

    def outer_fwd(n_rows):
        th = 2 * np.pi * np.outer(k1, np.arange(n_rows)) / FFT_N1
        f = np.stack([np.cos(th), -np.sin(th)], axis=1).reshape(2 * FFT_K1, n_rows)
        return np.kron(f, eye)

    n2 = np.arange(FFT_N2)
    k2 = np.arange(FFT_N2)
    ph = 2 * np.pi * n2[None, None, :] * (k1[:, None, None] + FFT_N1 * k2[None, :, None]) / FFT_N
    gr, gi = np.cos(ph), -np.sin(ph)
    g = np.concatenate([np.concatenate([gr, -gi], axis=2), np.concatenate([gi, gr], axis=2)], axis=1)
    ginv = np.swapaxes(g, 1, 2)
    th2 = 2 * np.pi * np.outer(np.arange(n1h), k1) / FFT_N1
    c = np.where((k1 == 0) | (k1 == n1h), 1.0, 2.0)[None, :] / FFT_N
    finv = np.stack([c * np.cos(th2), -c * np.sin(th2)], axis=2).reshape(n1h, 2 * FFT_K1)
    finv = finv[:, [col for col in range(2 * FFT_K1) if col not in _REAL_ONLY_COLS]]
    bf = lambda a: jnp.asarray(a, F32).astype(BF16)
    return dict(s1_half=bf(outer_fwd(n1h)), s1_full=bf(outer_fwd(FFT_N1)), g=bf(g), ginv=bf(ginv),
                i2=bf(np.kron(finv, eye)))


def _outer_forward(src_ref, s1_ref, dst_ref, n1_rows):
    def body(n2h, carry):
        r0 = pl.multiple_of(n2h * SUBLANES, SUBLANES)
        tiles = [src_ref[pl.ds(n1 * FFT_N2 + r0, SUBLANES), :] for n1 in range(n1_rows)]
        rhs = jnp.concatenate(tiles, axis=0).astype(BF16)
        out = jnp.dot(s1_ref[...], rhs, preferred_element_type=F32)
        for k1 in range(FFT_K1):
            for ri in range(2):
                row = (2 * k1 + ri) * SUBLANES
                dst_ref[k1, pl.ds(ri * FFT_N2 + r0, SUBLANES), :] = out[row:row + SUBLANES, :]
        return carry
    lax.fori_loop(0, FFT_N2 // SUBLANES, body, 0, unroll=OUTER_UNROLL)


def _filter_spectrum_kernel(k_ref, s1_ref, g_ref, kf_ref, a_ref):
    _outer_forward(k_ref, s1_ref, a_ref, FFT_N1)

    def body(k1, carry):
        kf_ref[k1] = jnp.dot(g_ref[k1], a_ref[k1].astype(BF16), preferred_element_type=F32)
        return carry
    lax.fori_loop(0, FFT_K1, body, 0)


def _filter_spectrum(kfull, tables, *, cb=MXU_TILE):
    e = kfull.shape[1]
    cb = min(cb, e)
    return pl.pallas_call(
        _filter_spectrum_kernel,
        grid=(e // cb,),
        in_specs=[pl.BlockSpec((FFT_N, cb), lambda c: (0, c)),
                  _resident(tables["s1_full"].shape, 1), _resident(tables["g"].shape, 1)],
        out_specs=pl.BlockSpec((FFT_K1, 2 * FFT_N2, cb), lambda c: (0, 0, c)),
        out_shape=jax.ShapeDtypeStruct((FFT_K1, 2 * FFT_N2, e), F32),
        scratch_shapes=[pltpu.VMEM((FFT_K1, 2 * FFT_N2, cb), F32)],
        compiler_params=_params(("arbitrary",)),
        name="filter_spectrum",
    )(kfull, tables["s1_full"], tables["g"])


def _short_conv(x_ref, cw_ref, cb_ref, group, c, stage_ref, n_rows):
    r0 = pl.multiple_of(c * SHORT_CHUNK, SHORT_CHUNK)
    prev0 = pl.multiple_of(jnp.maximum(r0 - BF16_ROWS, 0), BF16_ROWS)
    next0 = pl.multiple_of(jnp.minimum(r0 + SHORT_CHUNK, n_rows - BF16_ROWS), BF16_ROWS)
    prev = x_ref[pl.ds(prev0, BF16_ROWS), :].astype(F32)[BF16_ROWS - 1:BF16_ROWS, :] * jnp.where(c > 0, 1.0, 0.0)
    nxt = x_ref[pl.ds(next0, BF16_ROWS), :].astype(F32)[0:1, :] * jnp.where(r0 + SHORT_CHUNK < n_rows, 1.0, 0.0)
    stage_ref[STAGE_PAD - 1:STAGE_PAD, :] = prev
    stage_ref[STAGE_PAD:STAGE_PAD + SHORT_CHUNK, :] = x_ref[pl.ds(r0, SHORT_CHUNK), :].astype(F32)
    stage_ref[STAGE_PAD + SHORT_CHUNK:STAGE_PAD + SHORT_CHUNK + 1, :] = nxt
    w = cw_ref[group]
    return (cb_ref[group]
            + w[0:1] * stage_ref[STAGE_PAD - 1:STAGE_PAD - 1 + SHORT_CHUNK, :]
            + w[1:2] * stage_ref[STAGE_PAD:STAGE_PAD + SHORT_CHUNK, :]
            + w[2:3] * stage_ref[STAGE_PAD + 1:STAGE_PAD + 1 + SHORT_CHUNK, :])


def _long_conv_kernel(x0_ref, x1_ref, v_ref, zs_ref, cw_ref, cb_ref, hb_ref, kf_ref,
                      s1_ref, g_ref, ginv_ref, i2_ref, o_ref, w_ref, ab_ref, stage_ref):
    n_rows = x1_ref.shape[0]
    n_chunks = n_rows // SHORT_CHUNK

    def make_w(c, carry):
        rows = pl.ds(pl.multiple_of(c * SHORT_CHUNK, SHORT_CHUNK), SHORT_CHUNK)
        x1c = _short_conv(x1_ref, cw_ref, cb_ref, 1, c, stage_ref, n_rows)
        vc = _short_conv(v_ref, cw_ref, cb_ref, 2, c, stage_ref, n_rows)
        w_ref[rows, :] = x1c * vc
        return carry
    lax.fori_loop(0, n_chunks, make_w, 0)

    _outer_forward(w_ref, s1_ref, ab_ref, FFT_N1 // 2)

    def per_k1(k1, carry):
        x = jnp.dot(g_ref[k1], ab_ref[k1].astype(BF16), preferred_element_type=F32)
        kf = kf_ref[k1]
        xr, xi = x[:FFT_N2], x[FFT_N2:]
        kr, ki = kf[:FFT_N2], kf[FFT_N2:]
        y = jnp.concatenate([xr * kr - xi * ki, xr * ki + xi * kr], axis=0).astype(BF16)
        ab_ref[k1] = jnp.dot(ginv_ref[k1], y, preferred_element_type=F32)
        return carry
    lax.fori_loop(0, FFT_K1, per_k1, 0, unroll=K1_UNROLL)

    def outer_inverse(n2h, carry):
        r0 = pl.multiple_of(n2h * SUBLANES, SUBLANES)
        tiles = [ab_ref[k1, pl.ds(ri * FFT_N2 + r0, SUBLANES), :] for k1 in range(FFT_K1) for ri in range(2)
                 if 2 * k1 + ri not in _REAL_ONLY_COLS]
        rhs = jnp.concatenate(tiles, axis=0).astype(BF16)
        out = jnp.dot(i2_ref[...], rhs, preferred_element_type=F32)
        for n1 in range(FFT_N1 // 2):
            rows = pl.ds(n1 * FFT_N2 + r0, SUBLANES)
            w_ref[rows, :] = out[n1 * SUBLANES:(n1 + 1) * SUBLANES, :] + hb_ref[...] * w_ref[rows, :]
        return carry
    lax.fori_loop(0, FFT_N2 // SUBLANES, outer_inverse, 0, unroll=OUTER_UNROLL)

    def finish(c, carry):
        rows = pl.ds(pl.multiple_of(c * SHORT_CHUNK, SHORT_CHUNK), SHORT_CHUNK)
        x0c = _short_conv(x0_ref, cw_ref, cb_ref, 0, c, stage_ref, n_rows)
        o_ref[rows, :] = (x0c * w_ref[rows, :] * zs_ref[rows, :].astype(F32)).astype(o_ref.dtype)
        return carry
    lax.fori_loop(0, n_chunks, finish, 0)


def _long_conv(x0, x1, v, zs, conv_w, conv_b, h_bias, kf, tables, *, cb=MXU_TILE):
    bsz, seq, e = x0.shape
    assert seq == FFT_N // 2
    cb = min(cb, e)
    cw = conv_w.reshape(3, 3, e).transpose(1, 0, 2)
    cbias = conv_b.reshape(3, 1, e)
    tok = pl.BlockSpec((None, seq, cb), lambda c, b: (b, 0, c))
    consts = [tables["s1_half"], tables["g"], tables["ginv"], tables["i2"]]
    return pl.pallas_call(
        _long_conv_kernel,
        grid=(e // cb, bsz),
        in_specs=[tok, tok, tok, tok,
                  pl.BlockSpec((3, 3, cb), lambda c, b: (0, 0, c)),
                  pl.BlockSpec((3, 1, cb), lambda c, b: (0, 0, c)),
                  pl.BlockSpec((1, cb), lambda c, b: (0, c)),
                  pl.BlockSpec((FFT_K1, 2 * FFT_N2, cb), lambda c, b: (0, 0, c), pipeline_mode=pl.Buffered(1))]
                 + [_resident(a.shape, 2) for a in consts],
        out_specs=tok,
        out_shape=jax.ShapeDtypeStruct((bsz, seq, e), BF16),
        scratch_shapes=[pltpu.VMEM((seq, cb), F32),
                        pltpu.VMEM((FFT_K1, 2 * FFT_N2, cb), F32),
                        pltpu.VMEM((SHORT_CHUNK + 2 * STAGE_PAD, cb), F32)],
        compiler_params=_params(("arbitrary", "arbitrary")),
        name="long_conv",
    )(x0, x1, v, zs, cw, cbias, h_bias.reshape(1, e), kf, *consts)


def _final_norm_kernel(h_ref, g_ref, o_ref):
    x = h_ref[...]
    ms = jnp.mean(x * x, axis=-1, keepdims=True)
    o_ref[...] = x * lax.rsqrt(ms + EPS) * g_ref[...]


def _final_norm(h, g, *, tm=1024):
    bn, t, d = h.shape
    m = bn * t
    tm = min(tm, m)
    out = pl.pallas_call(
        _final_norm_kernel,
        grid=(m // tm,),
        in_specs=[pl.BlockSpec((tm, d), lambda i: (i, 0)),
                  pl.BlockSpec((1, d), lambda i: (0, 0))],
        out_specs=pl.BlockSpec((tm, d), lambda i: (i, 0)),
        out_shape=jax.ShapeDtypeStruct((m, d), F32),
        compiler_params=_params(("arbitrary",)),
        name="final_norm",
    )(h.reshape(m, d), g.reshape(1, d))
    return out.reshape(bn, t, d)


def kernel(x, c, ctx, c_ctx, norm_g, ada_w, ada_b, final_g, cv_w_in, cv_dw_w, cv_dw_b, cv_ln_g, cv_ln_b, cv_w_out, ml_w_in, ml_conv_w, ml_conv_b, ml_w_q, ml_w_k, ml_w_v, ml_w_o, ml_b_o, ml_w_gates, ml_b_gates, ml_mh_g, ml_skip, ml_w_out, hy_w_in, hy_conv_w, hy_conv_b, hy_f_w1, hy_f_b1, hy_f_freq1, hy_f_w2, hy_f_b2, hy_f_freq2, hy_f_w3, hy_f_b3, hy_h_bias, hy_w_out):
    depth = norm_g.shape[0]
    bsz, seq, d = x.shape
    n_ctx = ctx.shape[1]
    readers = [i for i in range(depth) if i % N_MIXERS == 1]
    last_reader = readers[-1] if readers else -1

    cond_rows = -(-(bsz + 1) // SUBLANES) * SUBLANES
    cond = jnp.concatenate([c, c_ctx[None], jnp.zeros((cond_rows - bsz - 1, d), F32)], axis=0)
    ada = _ada_params(cond, ada_w, ada_b)

    h_lat, h_ctx = x, ctx
    norm_fused = (depth - 1) % N_MIXERS != 1
    for i in range(depth):
        kind, j = i % N_MIXERS, i // N_MIXERS
        close_g = final_g if (norm_fused and i == depth - 1) else None
        ctx_in = i <= last_reader
        ctx_out = i < last_reader
        sh, sc, gt = jnp.split(ada[i, :bsz], 3, axis=-1)
        sh_c, sc_c, gt_c = (jnp.broadcast_to(a, (bsz, d)) for a in jnp.split(ada[i, bsz:bsz + 1], 3, axis=-1))
        if kind == 0:
            w_in = cv_w_in[j].astype(BF16)
            w_out = cv_w_out[j].astype(BF16)
            streams = [(h_lat, sh, sc, gt, GRID_W, _conv_mid_dft)]
            if ctx_out:
                streams.append((h_ctx, sh_c, sc_c, gt_c, n_ctx, _conv_mid))
            new = []
            for h, s_h, s_c, g_t, seg, conv_mid in streams:
                y, zs = _in_proj(h, norm_g[i], s_h, s_c, w_in, n_groups=3, n_out=2, epilogue=_epilogue_conv,
                                 tm=2 * IN_PROJ_ROWS)
                yb = conv_mid(y, zs, cv_dw_w[j], cv_dw_b[j], cv_ln_g[j], cv_ln_b[j], seg=seg)
                new.append(_out_matmul(yb, w_out, h, g_t, final_g=close_g if h is h_lat else None))
            h_lat = new[0]
            if ctx_out:
                h_ctx = new[1]
        elif kind == 1:
            assert ctx_in and not ctx_out
            w_in = ml_w_in[j].astype(BF16)
            w_out = ml_w_out[j].astype(BF16)
            lead = (-n_ctx) % IN_PROJ_ROWS
            h_all = jnp.concatenate([jnp.zeros((bsz, lead, d), F32), h_ctx, h_lat], axis=1)
            sh_all = jnp.concatenate([sh, sh_c[:1]], axis=0)
            sc_all = jnp.concatenate([sc, sc_c[:1]], axis=0)
            xm, zs = _in_proj(h_all, norm_g[i], sh_all, sc_all, w_in, n_groups=2, n_out=2,
                              epilogue=_epilogue_gate_last, n_ctx=lead + n_ctx, tm=IN_PROJ_ROWS, tn=1024)
            q, k, kt, v, o, xc, gc, gr = _mlstm_pre(
                xm, ml_conv_w[j], ml_conv_b[j], ml_w_q[j], ml_w_k[j], ml_w_v[j], ml_w_o[j], ml_b_o[j],
                ml_w_gates[j], ml_b_gates[j], lead=lead, n_ctx=n_ctx, n_heads=MLSTM_HEADS)
            hs = _mlstm_scan(q, k, kt, v, gc, gr, n_heads=MLSTM_HEADS, n_ctx=n_ctx)
            h_lat = _mlstm_out(hs, o, xc, zs, ml_mh_g[j], ml_skip[j], w_out, h_lat, gt,
                               lead=lead, n_ctx=n_ctx, n_heads=MLSTM_HEADS)
        else:
            assert not ctx_out
            w_in = hy_w_in[j].astype(BF16)
            w_out = hy_w_out[j].astype(BF16)
            x0, x1, v, zs = _in_proj(h_lat, norm_g[i], sh, sc, w_in, n_groups=4, n_out=4,
                                     epilogue=_epilogue_gate_last, tm=2 * IN_PROJ_ROWS, tn=256)
            tables = _dft_tables()
            kfull = _hyena_filter(seq, hy_f_w1[j], hy_f_b1[j], hy_f_freq1[j], hy_f_w2[j], hy_f_b2[j],
                                  hy_f_freq2[j], hy_f_w3[j], hy_f_b3[j])
            kf = _filter_spectrum(kfull, tables)
            yb = _long_conv(x0, x1, v, zs, hy_conv_w[j], hy_conv_b[j], hy_h_bias[j], kf, tables)
            h_lat = _out_matmul(yb, w_out, h_lat, gt, final_g=close_g)
    return h_lat if norm_fused else _final_norm(h_lat, final_g)
```

```python
import functools
import math

import jax
import jax.numpy as jnp
import numpy as np
from jax import lax
from jax.experimental import pallas as pl
from jax.experimental.pallas import tpu as pltpu

GRID_W = 64
N_MIXERS = 3
EPS = 1e-6
MLSTM_HEADS = 8
HYENA_EMB_DIM = 33
HYENA_FAST_DECAY = 0.3
HYENA_SLOW_DECAY = 1.5
HYENA_DECAY_TARGET = 1e-2

V7X_VMEM_LIMIT_BYTES = 56 * 1024 * 1024
SUBLANES = 8
BF16_ROWS = 16
LANES = 128
MXU_TILE = 256

F32 = jnp.float32
BF16 = jnp.bfloat16


def _tiles(m, n, tm, tn):
    tm, tn = min(tm, m), min(tn, n)
    assert m % tm == 0 and n % tn == 0, (m, n, tm, tn)
    return tm, tn


def _sigmoid(x):
    return 1.0 / (1.0 + jnp.exp(-x))


def _silu(x):
    return x * _sigmoid(x)


def _dot_bf16(a, b):
    return jnp.dot(a.astype(BF16), b.astype(BF16), preferred_element_type=F32)


def _params(semantics):
    return pltpu.CompilerParams(dimension_semantics=semantics,
                                vmem_limit_bytes=V7X_VMEM_LIMIT_BYTES)


def _resident(shape, n_grid):
    zeros = (0,) * len(shape)
    return pl.BlockSpec(shape, lambda *_: zeros, pipeline_mode=pl.Buffered(1))


def _ada_kernel(c_ref, w_ref, b_ref, o_ref):
    o_ref[0] = _dot_bf16(_silu(c_ref[...]), w_ref[0]) + b_ref[0]


def _ada_params(cond, ada_w, ada_b, *, tn=512):
    depth, d, n = ada_w.shape
    r = cond.shape[0]
    return pl.pallas_call(
        _ada_kernel,
        grid=(depth, n // tn),
        in_specs=[pl.BlockSpec((r, d), lambda i, j: (0, 0)),
                  pl.BlockSpec((1, d, tn), lambda i, j: (i, 0, j)),
                  pl.BlockSpec((1, 1, tn), lambda i, j: (i, 0, j))],
        out_specs=pl.BlockSpec((1, r, tn), lambda i, j: (i, 0, j)),
        out_shape=jax.ShapeDtypeStruct((depth, r, n), F32),
        compiler_params=_params(("arbitrary", "arbitrary")),
        name="ada_params",
    )(cond, ada_w, ada_b.reshape(depth, 1, n))


def _epilogue_conv(a, g, z):
    return a * _sigmoid(g), _silu(z)


def _epilogue_gate_last(*accs):
    return accs[:-1] + (_silu(accs[-1]),)


def _in_proj_kernel(*refs, n_groups, epilogue):
    h_ref, g_ref, sh_ref, sc_ref = refs[:4]
    w_refs = refs[4:4 + n_groups]
    o_refs = refs[4 + n_groups:-1]
    u_ref = refs[-1]

    @pl.when(pl.program_id(1) == 0)
    def _():
        x = h_ref[...]
        ms = jnp.mean(x * x, axis=-1, keepdims=True)
        y = x * lax.rsqrt(ms + EPS) * g_ref[...]
        u_ref[...] = (y * (1.0 + sc_ref[0]) + sh_ref[0]).astype(u_ref.dtype)

    u = u_ref[...]
    outs = epilogue(*[jnp.dot(u, w[...], preferred_element_type=F32) for w in w_refs])
    for o_ref, val in zip(o_refs, outs, strict=True):
        o_ref[...] = val.astype(o_ref.dtype)


IN_PROJ_ROWS = 512


def _in_proj(h, g, sh, sc, w, *, n_groups, n_out, epilogue, n_ctx=0, tm=IN_PROJ_ROWS, tn=512):
    bn, t, d = h.shape
    e = w.shape[1] // n_groups
    tm, tn = _tiles(t, e, tm, tn)
    assert n_ctx % tm == 0
    tiles_per_seq, ctx_tiles = t // tm, n_ctx // tm
    col_tiles = e // tn
    n_mod = sh.shape[0]

    def mod_row(i, j):
        return (jnp.where(i % tiles_per_seq < ctx_tiles, n_mod - 1, i // tiles_per_seq), 0, 0)

    w_specs = [pl.BlockSpec((d, tn), functools.partial(lambda i, j, k: (0, k * col_tiles + j), k=k))
               for k in range(n_groups)]
    outs = pl.pallas_call(
        functools.partial(_in_proj_kernel, n_groups=n_groups, epilogue=epilogue),
        grid=(bn * tiles_per_seq, col_tiles),
        in_specs=[
            pl.BlockSpec((tm, d), lambda i, j: (i, 0)),
            pl.BlockSpec((1, d), lambda i, j: (0, 0)),
            pl.BlockSpec((1, 1, d), mod_row),
            pl.BlockSpec((1, 1, d), mod_row),
        ] + w_specs,
        out_specs=[pl.BlockSpec((tm, tn), lambda i, j: (i, j))] * n_out,
        out_shape=[jax.ShapeDtypeStruct((bn * t, e), BF16)] * n_out,
        scratch_shapes=[pltpu.VMEM((tm, d), BF16)],
        compiler_params=_params(("arbitrary", "arbitrary")),
        name="in_proj",
    )(h.reshape(bn * t, d), g.reshape(1, d), sh.reshape(n_mod, 1, d),
      sc.reshape(n_mod, 1, d), *([w] * n_groups))
    return [o.reshape(bn, t, e) for o in outs]


def _out_matmul_kernel(y_ref, w_ref, h_ref, gt_ref, o_ref):
    acc = jnp.dot(y_ref[...], w_ref[...], preferred_element_type=F32)
    o_ref[...] = h_ref[...] + gt_ref[0] * acc


def _out_matmul_norm_kernel(y_ref, w_ref, h_ref, gt_ref, g_ref, o_ref):
    acc = jnp.dot(y_ref[...], w_ref[...], preferred_element_type=F32)
    x = h_ref[...] + gt_ref[0] * acc
    ms = jnp.mean(x * x, axis=-1, keepdims=True)
    o_ref[...] = x * lax.rsqrt(ms + EPS) * g_ref[...]


def _out_matmul(y, w, h, gt, *, final_g=None, tm=512):
    bn, t, e = y.shape
    d = w.shape[1]
    tm, tn = _tiles(t, d, tm, d)
    tiles_per_seq = t // tm
    in_specs = [
        pl.BlockSpec((tm, e), lambda i, j: (i, 0)),
        _resident((e, d), 2),
        pl.BlockSpec((tm, tn), lambda i, j: (i, j)),
        pl.BlockSpec((1, 1, tn), lambda i, j: (i // tiles_per_seq, 0, j)),
    ]
    args = [y.reshape(bn * t, e), w, h.reshape(bn * t, d), gt.reshape(bn, 1, d)]
    if final_g is not None:
        in_specs.append(_resident((1, d), 2))
        args.append(final_g.reshape(1, d))
    out = pl.pallas_call(
        _out_matmul_kernel if final_g is None else _out_matmul_norm_kernel,
        grid=(bn * tiles_per_seq, d // tn),
        in_specs=in_specs,
        out_specs=pl.BlockSpec((tm, tn), lambda i, j: (i, j)),
        out_shape=jax.ShapeDtypeStruct((bn * t, d), F32),
        compiler_params=_params(("arbitrary", "arbitrary")),
        name="out_matmul",
    )(*args)
    return out.reshape(bn, t, d)


CONV_PAD = 16
LN_UNROLL = 16


def _conv_mid_kernel(y_ref, zs_ref, w_ref, b_ref, lg_ref, lb_ref, o_ref, xp_ref, xs_ref, cv_ref, *, seg, width):
    tb, e = y_ref.shape
    n_seg = tb // seg
    half = width // 2
    rows_s = xs_ref.shape[2]
    rb = SUBLANES
    zeros = jnp.zeros((CONV_PAD, e), F32)
    for s in range(n_seg):
        xp_ref[s, 0:CONV_PAD, :] = zeros
        xp_ref[s, CONV_PAD:CONV_PAD + seg, :] = y_ref[s * seg:(s + 1) * seg, :].astype(F32)
        xp_ref[s, CONV_PAD + seg:2 * CONV_PAD + seg, :] = zeros

    def lane_block(cb, carry):
        lanes = pl.ds(pl.multiple_of(cb * LANES, LANES), LANES)
        for sh in range(rb):
            for s in range(n_seg):
                xs_ref[sh, s] = xp_ref[s, sh:sh + rows_s, lanes]
        taps = [w_ref[k:k + 1, lanes] for k in range(width)]
        bias = b_ref[:, lanes]
        for s in range(n_seg):
            for r in range(seg // rb):
                acc = bias
                for k in range(width):
                    off = CONV_PAD - half + k
                    base = (off // rb) * rb + r * rb
                    acc = acc + taps[k] * xs_ref[off % rb, s, base:base + rb, :]
                cv_ref[s * seg + r * rb:s * seg + (r + 1) * rb, lanes] = acc
        return carry

    lax.fori_loop(0, e // LANES, lane_block, 0)

    def row_block(i, carry):
        rows = pl.ds(pl.multiple_of(i * rb, rb), rb)
        cv = cv_ref[rows, :]
        mu = jnp.mean(cv, axis=-1, keepdims=True)
        xc = cv - mu
        var = jnp.mean(xc * xc, axis=-1, keepdims=True)
        yn = xc * lax.rsqrt(var + EPS) * lg_ref[...] + lb_ref[...]
        o_ref[rows, :] = (_silu(yn) * zs_ref[rows, :].astype(F32)).astype(o_ref.dtype)
        return carry

    lax.fori_loop(0, tb // rb, row_block, 0, unroll=LN_UNROLL)


def _conv_mid(y, zs, dw_w, dw_b, ln_g, ln_b, *, seg, tb=256):
    bn, t, e = y.shape
    width = dw_w.shape[0]
    m = bn * t
    tb = min(tb, m)
    assert m % tb == 0 and tb % seg == 0 and t % seg == 0 and width // 2 <= CONV_PAD
    row = lambda a: a.reshape(1, e)
    tok = pl.BlockSpec((tb, e), lambda i: (i, 0))
    vec = pl.BlockSpec((1, e), lambda i: (0, 0))
    out = pl.pallas_call(
        functools.partial(_conv_mid_kernel, seg=seg, width=width),
        grid=(m // tb,),
        in_specs=[tok, tok, pl.BlockSpec((width, e), lambda i: (0, 0)), vec, vec, vec],
        out_specs=tok,
        out_shape=jax.ShapeDtypeStruct((m, e), BF16),
        scratch_shapes=[pltpu.VMEM((tb // seg, seg + 2 * CONV_PAD, e), F32),
                        pltpu.VMEM((SUBLANES, tb // seg, seg + 2 * CONV_PAD - SUBLANES, LANES), F32),
                        pltpu.VMEM((tb, e), F32)],
        compiler_params=_params(("arbitrary",)),
        name="conv_mid",
    )(y.reshape(m, e), zs.reshape(m, e), dw_w, row(dw_b), row(ln_g), row(ln_b))
    return out.reshape(bn, t, e)


CONV_DFT = 96
CONV_FREQS = CONV_DFT // 2 + 1
CONV_HALF = -(-CONV_FREQS // SUBLANES) * SUBLANES
CONV_LANE_UNROLL = 8


def _conv_dft_tables(seg, width, n_seg):
    f = np.arange(CONV_FREQS)
    th = 2 * np.pi * np.outer(f, np.arange(seg)) / CONV_DFT
    fwd = np.zeros((2 * CONV_HALF, seg))
    fwd[:CONV_FREQS] = np.cos(th)
    fwd[CONV_HALF:CONV_HALF + CONV_FREQS] = -np.sin(th)
    c = np.where((f == 0) | (f == CONV_DFT // 2), 1.0, 2.0) / CONV_DFT
    inv = np.zeros((seg, 2 * CONV_HALF))
    inv[:, :CONV_FREQS] = (c[:, None] * np.cos(th)).T
    inv[:, CONV_HALF:CONV_HALF + CONV_FREQS] = (-c[:, None] * np.sin(th)).T
    eye = np.eye(n_seg)
    thw = 2 * np.pi * np.outer(f, np.arange(width) - width // 2) / CONV_DFT
    wtab = np.zeros((2 * CONV_HALF, width))
    wtab[:CONV_FREQS] = np.cos(thw)
    wtab[CONV_HALF:CONV_HALF + CONV_FREQS] = np.sin(thw)
    bf = lambda a: jnp.asarray(a, F32).astype(BF16)
    return bf(np.kron(eye, fwd)), bf(np.kron(eye, inv)), jnp.asarray(wtab, F32)


def _conv_dft_kernel(y_ref, zs_ref, w_ref, b_ref, lg_ref, lb_ref, fwd_ref, inv_ref, wtab_ref, o_ref,
                     h_ref, cv_ref, *, n_seg):
    tb, e = y_ref.shape
    rb = SUBLANES

    @pl.when(pl.program_id(0) == 0)
    def _():
        h_ref[...] = jnp.dot(wtab_ref[...], w_ref[...], preferred_element_type=F32,
                             precision=lax.Precision.HIGHEST)

    def lane_block(cb, carry):
        lanes = pl.ds(pl.multiple_of(cb * MXU_TILE, MXU_TILE), MXU_TILE)
        x = jnp.dot(fwd_ref[...], y_ref[:, lanes], preferred_element_type=F32)
        hr = h_ref[0:CONV_HALF, lanes]
        hi = h_ref[CONV_HALF:2 * CONV_HALF, lanes]
        parts = []
        for s in range(n_seg):
            xr = x[s * 2 * CONV_HALF:s * 2 * CONV_HALF + CONV_HALF]
            xi = x[s * 2 * CONV_HALF + CONV_HALF:(s + 1) * 2 * CONV_HALF]
            parts += [xr * hr - xi * hi, xr * hi + xi * hr]
        yf = jnp.concatenate(parts, axis=0).astype(BF16)
        cv_ref[:, lanes] = jnp.dot(inv_ref[...], yf, preferred_element_type=F32) + b_ref[:, lanes]
        return carry

    lax.fori_loop(0, e // MXU_TILE, lane_block, 0, unroll=CONV_LANE_UNROLL)

    def row_block(i, carry):
        rows = pl.ds(pl.multiple_of(i * rb, rb), rb)
        cv = cv_ref[rows, :]
        mu = jnp.mean(cv, axis=-1, keepdims=True)
        xc = cv - mu
        var = jnp.mean(xc * xc, axis=-1, keepdims=True)
        yn = xc * lax.rsqrt(var + EPS) * lg_ref[...] + lb_ref[...]
        o_ref[rows, :] = (_silu(yn) * zs_ref[rows, :].astype(F32)).astype(o_ref.dtype)
        return carry

    lax.fori_loop(0, tb // rb, row_block, 0, unroll=LN_UNROLL)


def _conv_mid_dft(y, zs, dw_w, dw_b, ln_g, ln_b, *, seg, tb=256):
    bn, t, e = y.shape
    width = dw_w.shape[0]
    m = bn * t
    tb = min(tb, m)
    assert m % tb == 0 and tb % seg == 0 and t % seg == 0 and seg + width - 1 <= CONV_DFT
    n_seg = tb // seg
    fwd, inv, wtab = _conv_dft_tables(seg, width, n_seg)
    row = lambda a: a.reshape(1, e)
    tok = pl.BlockSpec((tb, e), lambda i: (i, 0))
    out = pl.pallas_call(
        functools.partial(_conv_dft_kernel, n_seg=n_seg),
        grid=(m // tb,),
        in_specs=[tok, tok, _resident((width, e), 1), _resident((1, e), 1), _resident((1, e), 1),
                  _resident((1, e), 1), _resident(fwd.shape, 1), _resident(inv.shape, 1),
                  _resident(wtab.shape, 1)],
        out_specs=tok,
        out_shape=jax.ShapeDtypeStruct((m, e), BF16),
        scratch_shapes=[pltpu.VMEM((2 * CONV_HALF, e), F32), pltpu.VMEM((tb, e), F32)],
        compiler_params=_params(("arbitrary",)),
        name="conv_mid_dft",
    )(y.reshape(m, e), zs.reshape(m, e), dw_w, row(dw_b), row(ln_g), row(ln_b), fwd, inv, wtab)
    return out.reshape(bn, t, e)


STAGE_PAD = 8


def _expand_block_diag(w):
    g, bi, bo = w.shape
    per = MXU_TILE // bi
    wt = w.reshape(g // per, per, bi, bo)
    eye = jnp.eye(per, dtype=w.dtype)
    return jnp.einsum('tgio,gh->tgiho', wt, eye).reshape(g // per, per * bi, per * bo)


def _mlstm_pre_kernel(xm_ref, prev_ref, next_ref, cw_ref, cb_ref, wq_ref, wk_ref, wkt_ref, wv_ref, wo_ref,
                      bo_ref, wg_ref, bg_ref,
                      q_ref, k_ref, kt_ref, v_ref, o_ref, xc_ref, gc_ref, gr_ref, stage_ref,
                      *, tiles_per_seq, ctx_tiles, n_heads):
    tb, e = xm_ref.shape
    n_g = gc_ref.shape[1]
    q_scale = (e // n_heads) ** -0.5
    t = pl.program_id(0) % tiles_per_seq
    has_prev = jnp.logical_and(t != 0, t != ctx_tiles).astype(F32)
    has_next = jnp.logical_and(t != ctx_tiles - 1, t != tiles_per_seq - 1).astype(F32)
    gc = jnp.zeros(gc_ref.shape, F32) + bg_ref[...]
    nt = (((1,), (1,)), ((), ()))
    for j in range(e // MXU_TILE):
        lanes = slice(j * MXU_TILE, (j + 1) * MXU_TILE)
        xm = xm_ref[:, lanes]
        stage_ref[STAGE_PAD - 1:STAGE_PAD, :] = \
            prev_ref[:, lanes].astype(F32)[BF16_ROWS - 1:BF16_ROWS, :] * has_prev
        stage_ref[STAGE_PAD:STAGE_PAD + tb, :] = xm.astype(F32)
        stage_ref[STAGE_PAD + tb:STAGE_PAD + tb + 1, :] = next_ref[:, lanes].astype(F32)[0:1, :] * has_next
        cw = cw_ref[:, lanes]
        pre = (cb_ref[:, lanes] + cw[0:1] * stage_ref[STAGE_PAD - 1:STAGE_PAD - 1 + tb, :]
               + cw[1:2] * stage_ref[STAGE_PAD:STAGE_PAD + tb, :]
               + cw[2:3] * stage_ref[STAGE_PAD + 1:STAGE_PAD + 1 + tb, :])
        xcb = _silu(pre).astype(BF16)
        xc_ref[:, lanes] = xcb
        q = jnp.dot(xcb, wq_ref[j], preferred_element_type=F32)
        k = jnp.dot(xcb, wk_ref[j], preferred_element_type=F32)
        v = jnp.dot(xm, wv_ref[j], preferred_element_type=F32)
        o = jnp.dot(xcb, wo_ref[j], preferred_element_type=F32) + bo_ref[:, lanes]
        kt_ref[lanes, :] = lax.dot_general(wkt_ref[j], xcb, nt, preferred_element_type=F32).astype(BF16)
        qb, kb, vb = q.astype(BF16), k.astype(BF16), v.astype(BF16)
        q_ref[:, lanes] = (q * q_scale).astype(BF16)
        k_ref[:, lanes] = kb
        v_ref[:, lanes] = vb
        o_ref[:, lanes] = _sigmoid(o).astype(BF16)
        for i, a in enumerate((qb, kb, vb)):
            gc = gc + jnp.dot(a, wg_ref[i, lanes, :], preferred_element_type=F32)
    col = lax.broadcasted_iota(jnp.int32, gc.shape, 1)
    gc = jnp.where((col // n_heads) % 2 == 1, _log_sigmoid(gc), gc)
    gc_ref[...] = gc
    gc_wide = jnp.concatenate([gc, jnp.zeros((tb, LANES - n_g), F32)], axis=1)
    gr_ref[...] = gc_wide.T[:n_g, :]


def _mlstm_pre(xm, conv_w, conv_b, w_q, w_k, w_v, w_o, b_o, w_gates, b_gates, *, lead, n_ctx, n_heads, tb=256):
    bsz, t_all, e = xm.shape
    t = t_all - lead
    assert t % tb == 0 and n_ctx % tb == 0 and lead % tb == 0
    tiles_per_seq, ctx_tiles = t // tb, n_ctx // tb
    tiles_all, lead_tiles = t_all // tb, lead // tb
    m = bsz * t
    n_g = w_gates.shape[1]
    hpt = tb // BF16_ROWS
    n_halo = bsz * t_all // BF16_ROWS

    def src_tile(i):
        return (i // tiles_per_seq) * tiles_all + lead_tiles + i % tiles_per_seq
    bd = lambda w: _expand_block_diag(w).astype(BF16)
    wq, wk, wv, wo = bd(w_q), bd(w_k), bd(w_v), bd(w_o)
    wkt = jnp.swapaxes(wk, 1, 2)
    wg = w_gates.reshape(3, e, n_g).astype(BF16)
    tok = pl.BlockSpec((tb, e), lambda i: (i, 0))
    row = lambda a: a.reshape(1, -1)
    consts = [conv_w, row(conv_b), wq, wk, wkt, wv, wo, row(b_o), wg, row(b_gates)]
    sds = jax.ShapeDtypeStruct
    xm2 = xm.reshape(bsz * t_all, e)
    outs = pl.pallas_call(
        functools.partial(_mlstm_pre_kernel, tiles_per_seq=tiles_per_seq, ctx_tiles=ctx_tiles,
                          n_heads=n_heads),
        grid=(m // tb,),
        in_specs=[pl.BlockSpec((tb, e), lambda i: (src_tile(i), 0)),
                  pl.BlockSpec((BF16_ROWS, e), lambda i: (jnp.maximum(src_tile(i) * hpt - 1, 0), 0)),
                  pl.BlockSpec((BF16_ROWS, e), lambda i: (jnp.minimum((src_tile(i) + 1) * hpt, n_halo - 1), 0))]
                 + [_resident(a.shape, 1) for a in consts],
        out_specs=[tok, tok,
                   pl.BlockSpec((None, e, tb), lambda i: (i // tiles_per_seq, 0, i % tiles_per_seq)),
                   tok, tok, tok,
                   pl.BlockSpec((tb, n_g), lambda i: (i, 0)),
                   pl.BlockSpec((None, n_g, tb), lambda i: (i // tiles_per_seq, 0, i % tiles_per_seq))],
        out_shape=[sds((m, e), BF16), sds((m, e), BF16), sds((bsz, e, t), BF16), sds((m, e), BF16),
                   sds((m, e), BF16), sds((m, e), BF16), sds((m, n_g), F32), sds((bsz, n_g, t), F32)],
        scratch_shapes=[pltpu.VMEM((tb + 2 * STAGE_PAD, MXU_TILE), F32)],
        compiler_params=_params(("arbitrary",)),
        name="mlstm_pre",
    )(xm2, xm2, xm2, *consts)
    q, k, kt, v, o, xc, gc, gr = outs
    r3 = lambda a: a.reshape(bsz, t, -1)
    return r3(q), r3(k), kt, r3(v), r3(o), r3(xc), r3(gc), gr


SCAN_CHUNK = 256


def _log_sigmoid(x):
    return jnp.minimum(x, 0.0) - jnp.log1p(jnp.exp(-jnp.abs(x)))


SCAN_HEADS_PER_STEP = 4


def _split_bf16(x):
    hi = x.astype(BF16)
    return hi, (x - hi.astype(F32)).astype(BF16)


def _mlstm_scan_kernel(q_ref, k_ref, kt_ref, v_ref, gc_ref, gr_ref, o_ref,
                       c_ref, cb_ref, n_ref, m_ref, *, n_heads, heads_per_step):
    direction = pl.program_id(0)

    @pl.when(pl.program_id(3) == 0)
    def _():
        c_ref[...] = jnp.zeros_like(c_ref)
        cb_ref[...] = jnp.zeros_like(cb_ref)
        n_ref[...] = jnp.zeros_like(n_ref)
        m_ref[...] = jnp.full_like(m_ref, -1e30)

    L = q_ref.shape[1]
    dh = q_ref.shape[2] // heads_per_step
    n_g = gc_ref.shape[2]
    gc = gc_ref[0]
    gr = gr_ref[0]
    lane = lax.broadcasted_iota(jnp.int32, (L, n_g), 1)
    sub = lax.broadcasted_iota(jnp.int32, (n_g, L), 0)
    sign = 1 - 2 * direction
    diff = (lax.broadcasted_iota(jnp.int32, (L, L), 0)
            - lax.broadcasted_iota(jnp.int32, (L, L), 1)) * sign
    causal = diff >= 0
    anti = diff <= 0

    for hh in range(heads_per_step):
        lanes = slice(hh * dh, (hh + 1) * dh)
        head = pl.program_id(2) * heads_per_step + hh
        col_i = direction * (2 * n_heads) + head
        col_f = col_i + n_heads
        lf_col = jnp.sum(jnp.where(lane == col_f, gc, 0.0), axis=1, keepdims=True)
        li_row = jnp.sum(jnp.where(sub == col_i, gr, 0.0), axis=0, keepdims=True)
        lf_row = jnp.sum(jnp.where(sub == col_f, gr, 0.0), axis=0, keepdims=True)
        b_col = jnp.sum(jnp.where(causal, lf_row, 0.0), axis=1, keepdims=True)
        b_row = jnp.sum(jnp.where(anti, lf_col, 0.0), axis=0, keepdims=True)
        total = jnp.sum(lf_row, axis=1, keepdims=True)

        m = m_ref[hh]
        log_d = jnp.where(causal, b_col - b_row + li_row, -jnp.inf)
        log_inter = b_col + m
        m_t = jnp.maximum(log_inter, jnp.max(log_d, axis=1, keepdims=True))
        dmat = jnp.exp(log_d - m_t)
        a = jnp.exp(log_inter - m_t)

        q = q_ref[0, :, lanes]
        kt = kt_ref[0, lanes, :]
        v = v_ref[0, :, lanes]
        s = jnp.dot(q, kt, preferred_element_type=F32) * dmat
        num = a * jnp.dot(q, cb_ref[hh], preferred_element_type=F32) \
            + jnp.dot(s.astype(BF16), v, preferred_element_type=F32)
        n_hi, n_lo = _split_bf16(n_ref[hh])
        qn2 = lax.dot_general(q, jnp.concatenate([n_hi, n_lo], axis=0), (((1,), (1,)), ((), ())),
                              preferred_element_type=F32)
        qn = qn2[:, 0:1] + qn2[:, SUBLANES:SUBLANES + 1]
        den = a * qn + jnp.sum(s, axis=1, keepdims=True)
        o_ref[0, 0, :, lanes] = (num / jnp.maximum(jnp.abs(den), jnp.exp(-m_t))).astype(o_ref.dtype)

        log_w_row = total - b_row + li_row
        m_new = jnp.maximum(total + m, jnp.max(log_w_row, axis=1, keepdims=True))
        decay = jnp.exp(total + m - m_new)
        w_hi, w_lo = _split_bf16(jnp.exp(log_w_row - m_new))
        kwt = kt * w_hi
        c_new = decay * c_ref[hh] + jnp.dot(kwt, v, preferred_element_type=F32)
        c_ref[hh] = c_new
        cb_ref[hh] = c_new.astype(BF16)
        w2 = jnp.concatenate([jnp.broadcast_to(w_hi, (SUBLANES, L)), jnp.broadcast_to(w_lo, (SUBLANES, L))], axis=0)
        n_add = jnp.dot(w2, k_ref[0, :, lanes], preferred_element_type=F32)
        n_ref[hh] = decay * n_ref[hh] + n_add[0:SUBLANES] + n_add[SUBLANES:2 * SUBLANES]
        m_ref[hh] = m_new


def _mlstm_scan(q, k, kt, v, gc, gr, *, n_heads, n_ctx, chunk=SCAN_CHUNK, heads_per_step=SCAN_HEADS_PER_STEP):
    bsz, t, e = q.shape
    dh = e // n_heads
    hp = heads_per_step
    L = chunk
    assert t % L == 0 and n_ctx % L == 0 and n_heads % hp == 0
    nc, nc_ctx = t // L, n_ctx // L

    def cidx(d, c):
        rev = jnp.where(c < nc_ctx, nc_ctx - 1 - c, nc - 1 - c + nc_ctx)
        return jnp.where(d == 0, c, rev)

    tok = pl.BlockSpec((1, L, hp * dh), lambda d, b, h, c: (b, cidx(d, c), h))
    return pl.pallas_call(
        functools.partial(_mlstm_scan_kernel, n_heads=n_heads, heads_per_step=hp),
        grid=(2, bsz, n_heads // hp, nc),
        in_specs=[
            tok, tok,
            pl.BlockSpec((1, hp * dh, L), lambda d, b, h, c: (b, h, cidx(d, c))),
            tok,
            pl.BlockSpec((1, L, 4 * n_heads), lambda d, b, h, c: (b, cidx(d, c), 0)),
            pl.BlockSpec((1, 4 * n_heads, L), lambda d, b, h, c: (b, 0, cidx(d, c))),
        ],
        out_specs=pl.BlockSpec((1, 1, L, hp * dh), lambda d, b, h, c: (d, b, cidx(d, c), h)),
        out_shape=jax.ShapeDtypeStruct((2, bsz, t, e), BF16),
        scratch_shapes=[pltpu.VMEM((hp, dh, dh), F32), pltpu.VMEM((hp, dh, dh), BF16),
                        pltpu.VMEM((hp, SUBLANES, dh), F32), pltpu.VMEM((hp, 1, 1), F32)],
        compiler_params=_params(("arbitrary",) * 4),
        name="mlstm_scan",
    )(q, k, kt, v, gc, gr)


def _mlstm_out_kernel(hf_ref, hb_ref, o_ref, xc_ref, zs_ref, mh_ref, skip_ref, w_ref, res_ref, gt_ref,
                      out_ref, y_ref, *, n_heads):
    tm, e = o_ref.shape
    dh = e // n_heads
    for hd in range(n_heads):
        lanes = slice(hd * dh, (hd + 1) * dh)
        h = hf_ref[:, lanes].astype(F32) + hb_ref[:, lanes].astype(F32)
        mu = jnp.mean(h, axis=-1, keepdims=True)
        hc = h - mu
        var = jnp.mean(hc * hc, axis=-1, keepdims=True)
        hn = hc * lax.rsqrt(var + EPS) * mh_ref[:, lanes]
        y = (o_ref[:, lanes].astype(F32) * hn + skip_ref[:, lanes] * xc_ref[:, lanes].astype(F32)) \
            * zs_ref[:, lanes].astype(F32)
        y_ref[:, lanes] = y.astype(BF16)
    acc = jnp.dot(y_ref[...], w_ref[...], preferred_element_type=F32)
    out_ref[...] = res_ref[...] + gt_ref[0] * acc


def _mlstm_out(hs, o, xc, zs, mh_g, skip, w_out, h_lat, gt, *, lead, n_ctx, n_heads, tm=256):
    _, bsz, t, e = hs.shape
    n = t - n_ctx
    d = w_out.shape[1]
    assert n % tm == 0 and n_ctx % tm == 0 and lead % tm == 0
    tiles, off = n // tm, n_ctx // tm
    zs_off = off + lead // tm
    tok = pl.BlockSpec((None, tm, e), lambda i: (i // tiles, off + i % tiles, 0))
    zs_tok = pl.BlockSpec((None, tm, e), lambda i: (i // tiles, zs_off + i % tiles, 0))
    row = lambda a: a.reshape(1, -1)
    out = pl.pallas_call(
        functools.partial(_mlstm_out_kernel, n_heads=n_heads),
        grid=(bsz * tiles,),
        in_specs=[pl.BlockSpec((None, None, tm, e), lambda i: (0, i // tiles, off + i % tiles, 0)),
                  pl.BlockSpec((None, None, tm, e), lambda i: (1, i // tiles, off + i % tiles, 0)),
                  tok, tok, zs_tok, _resident((1, e), 1), _resident((1, e), 1), _resident((e, d), 1),
                  pl.BlockSpec((tm, d), lambda i: (i, 0)),
                  pl.BlockSpec((1, 1, d), lambda i: (i // tiles, 0, 0))],
        out_specs=pl.BlockSpec((tm, d), lambda i: (i, 0)),
        out_shape=jax.ShapeDtypeStruct((bsz * n, d), F32),
        scratch_shapes=[pltpu.VMEM((tm, e), BF16)],
        compiler_params=_params(("arbitrary",)),
        name="mlstm_out",
    )(hs, hs, o, xc, zs, row(mh_g), row(skip), w_out, h_lat.reshape(bsz * n, d), gt.reshape(bsz, 1, d))
    return out.reshape(bsz, n, d)


def _filter_features(seq):
    r = np.arange(2 * seq)
    lag = np.where(r < seq, r, 2 * seq - r).clip(0, seq - 1).astype(np.float64)
    t = lag / (seq - 1)
    bands = (HYENA_EMB_DIM - 1) // 2
    ang = 2.0 * math.pi * lag / seq
    fr = np.linspace(1e-4, bands - 1, bands)
    feat = np.concatenate([t[:, None], np.cos(fr[None] * ang[:, None]), -np.sin(fr[None] * ang[:, None])], axis=-1)
    return jnp.asarray(feat, F32), jnp.asarray(t[:, None], F32)


def _filter_kernel(feat_ref, t_ref, w1_ref, b1_ref, f1_ref, w2_ref, b2_ref, f2_ref, w3_ref, b3_ref, dl_ref,
                   o_ref, *, seq):
    tr = o_ref.shape[0]
    h = jnp.sin(f1_ref[...] * (_dot_bf16(feat_ref[...], w1_ref[...]) + b1_ref[...]))
    h = jnp.sin(f2_ref[...] * (_dot_bf16(h, w2_ref[...]) + b2_ref[...]))
    h = _dot_bf16(h, w3_ref[...]) + b3_ref[...]
    out = h * jnp.exp(-t_ref[...] * dl_ref[...])
    row = pl.program_id(0) * tr + lax.broadcasted_iota(jnp.int32, (tr, 1), 0)
    o_ref[...] = jnp.where(row == seq, 0.0, out)


def _hyena_filter(seq, f_w1, f_b1, f_freq1, f_w2, f_b2, f_freq2, f_w3, f_b3, *, tr=512, tn=1024):
    e = f_w3.shape[1] // 2
    feat, t = _filter_features(seq)
    lo = math.log(HYENA_DECAY_TARGET) / HYENA_FAST_DECAY
    hi = math.log(HYENA_DECAY_TARGET) / HYENA_SLOW_DECAY
    deltas = jnp.asarray(np.abs(np.linspace(lo, hi, e)), F32).reshape(1, e)
    fdim = f_w1.shape[1]
    row_tiles, col_tiles = 2 * seq // tr, e // tn
    half = lambda i: i // (row_tiles // 2)
    row = lambda a: a.reshape(1, -1)
    small = lambda shape: pl.BlockSpec(shape, lambda i, j: (0, 0))
    return pl.pallas_call(
        functools.partial(_filter_kernel, seq=seq),
        grid=(row_tiles, col_tiles),
        in_specs=[pl.BlockSpec((tr, HYENA_EMB_DIM), lambda i, j: (i, 0)),
                  pl.BlockSpec((tr, 1), lambda i, j: (i, 0)),
                  small((HYENA_EMB_DIM, fdim)), small((1, fdim)), small((1, fdim)),
                  small((fdim, fdim)), small((1, fdim)), small((1, fdim)),
                  pl.BlockSpec((fdim, tn), lambda i, j: (0, half(i) * col_tiles + j)),
                  pl.BlockSpec((1, tn), lambda i, j: (0, half(i) * col_tiles + j)),
                  pl.BlockSpec((1, tn), lambda i, j: (0, j))],
        out_specs=pl.BlockSpec((tr, tn), lambda i, j: (i, j)),
        out_shape=jax.ShapeDtypeStruct((2 * seq, e), F32),
        compiler_params=_params(("arbitrary", "arbitrary")),
        name="hyena_filter",
    )(feat, t, f_w1, row(f_b1), row(f_freq1), f_w2, row(f_b2), row(f_freq2), f_w3, row(f_b3), deltas)


FFT_N1 = 64
FFT_N2 = 128
FFT_N = FFT_N1 * FFT_N2
FFT_K1 = FFT_N1 // 2 + 1
SHORT_CHUNK = 256
K1_UNROLL = 11
OUTER_UNROLL = 8
_REAL_ONLY_COLS = (1, 2 * (FFT_K1 - 1) + 1)


def _dft_tables():
    n1h = FFT_N1 // 2
    k1 = np.arange(FFT_K1)
    eye = np.eye(SUBLANES)

    def outer_fwd(n_rows):
        th = 2 * np.pi * np.outer(k1, np.arange(n_rows)) / FFT_N1
        f = np.stack([np.cos(th), -np.sin(th)], axis=1).reshape(2 * FFT_K1, n_rows)
        return np.kron(f, eye)

    n2 = np.arange(FFT_N2)
    k2 = np.arange(FFT_N2)
    ph = 2 * np.pi * n2[None, None, :] * (k1[:, None, None] + FFT_N1 * k2[None, :, None]) / FFT_N
    gr, gi = np.cos(ph), -np.sin(ph)
    g = np.concatenate([np.concatenate([gr, -gi], axis=2), np.concatenate([gi, gr], axis=2)], axis=1)
    ginv = np.swapaxes(g, 1, 2)
    th2 = 2 * np.pi * np.outer(np.arange(n1h), k1) / FFT_N1
    c = np.where((k1 == 0) | (k1 == n1h), 1.0, 2.0)[None, :] / FFT_N
    finv = np.stack([c * np.cos(th2), -c * np.sin(th2)], axis=2).reshape(n1h, 2 * FFT_K1)
    finv = finv[:, [col for col in range(2 * FFT_K1) if col not in _REAL_ONLY_COLS]]
    bf = lambda a: jnp.asarray(a, F32).astype(BF16)
    return dict(s1_half=bf(outer_fwd(n1h)), s1_full=bf(outer_fwd(FFT_N1)), g=bf(g), ginv=bf(ginv),
                i2=bf(np.kron(finv, eye)))


def _outer_forward(src_ref, s1_ref, dst_ref, n1_rows):
    def body(n2h, carry):
        r0 = pl.multiple_of(n2h * SUBLANES, SUBLANES)
        tiles = [src_ref[pl.ds(n1 * FFT_N2 + r0, SUBLANES), :] for n1 in range(n1_rows)]
        rhs = jnp.concatenate(tiles, axis=0).astype(BF16)
        out = jnp.dot(s1_ref[...], rhs, preferred_element_type=F32)
        for k1 in range(FFT_K1):
            for ri in range(2):
                row = (2 * k1 + ri) * SUBLANES
                dst_ref[k1, pl.ds(ri * FFT_N2 + r0, SUBLANES), :] = out[row:row + SUBLANES, :]
        return carry
    lax.fori_loop(0, FFT_N2 // SUBLANES, body, 0, unroll=OUTER_UNROLL)


def _filter_spectrum_kernel(k_ref, s1_ref, g_ref, kf_ref, a_ref):
    _outer_forward(k_ref, s1_ref, a_ref, FFT_N1)

    def body(k1, carry):
        kf_ref[k1] = jnp.dot(g_ref[k1], a_ref[k1].astype(BF16), preferred_element_type=F32)
        return carry
    lax.fori_loop(0, FFT_K1, body, 0)


def _filter_spectrum(kfull, tables, *, cb=MXU_TILE):
    e = kfull.shape[1]
    cb = min(cb, e)
    return pl.pallas_call(
        _filter_spectrum_kernel,
        grid=(e // cb,),
        in_specs=[pl.BlockSpec((FFT_N, cb), lambda c: (0, c)),
                  _resident(tables["s1_full"].shape, 1), _resident(tables["g"].shape, 1)],
        out_specs=pl.BlockSpec((FFT_K1, 2 * FFT_N2, cb), lambda c: (0, 0, c)),
        out_shape=jax.ShapeDtypeStruct((FFT_K1, 2 * FFT_N2, e), F32),
        scratch_shapes=[pltpu.VMEM((FFT_K1, 2 * FFT_N2, cb), F32)],
        compiler_params=_params(("arbitrary",)),
        name="filter_spectrum",
    )(kfull, tables["s1_full"], tables["g"])


def _short_conv(x_ref, cw_ref, cb_ref, group, c, stage_ref, n_rows):
    r0 = pl.multiple_of(c * SHORT_CHUNK, SHORT_CHUNK)
    prev0 = pl.multiple_of(jnp.maximum(r0 - BF16_ROWS, 0), BF16_ROWS)
    next0 = pl.multiple_of(jnp.minimum(r0 + SHORT_CHUNK, n_rows - BF16_ROWS), BF16_ROWS)
    prev = x_ref[pl.ds(prev0, BF16_ROWS), :].astype(F32)[BF16_ROWS - 1:BF16_ROWS, :] * jnp.where(c > 0, 1.0, 0.0)
    nxt = x_ref[pl.ds(next0, BF16_ROWS), :].astype(F32)[0:1, :] * jnp.where(r0 + SHORT_CHUNK < n_rows, 1.0, 0.0)
    stage_ref[STAGE_PAD - 1:STAGE_PAD, :] = prev
    stage_ref[STAGE_PAD:STAGE_PAD + SHORT_CHUNK, :] = x_ref[pl.ds(r0, SHORT_CHUNK), :].astype(F32)
    stage_ref[STAGE_PAD + SHORT_CHUNK:STAGE_PAD + SHORT_CHUNK + 1, :] = nxt
    w = cw_ref[group]
    return (cb_ref[group]
            + w[0:1] * stage_ref[STAGE_PAD - 1:STAGE_PAD - 1 + SHORT_CHUNK, :]
            + w[1:2] * stage_ref[STAGE_PAD:STAGE_PAD + SHORT_CHUNK, :]
            + w[2:3] * stage_ref[STAGE_PAD + 1:STAGE_PAD + 1 + SHORT_CHUNK, :])


def _long_conv_kernel(x0_ref, x1_ref, v_ref, zs_ref, cw_ref, cb_ref, hb_ref, kf_ref,
                      s1_ref, g_ref, ginv_ref, i2_ref, o_ref, w_ref, ab_ref, stage_ref):
    n_rows = x1_ref.shape[0]
    n_chunks = n_rows // SHORT_CHUNK

    def make_w(c, carry):
        rows = pl.ds(pl.multiple_of(c * SHORT_CHUNK, SHORT_CHUNK), SHORT_CHUNK)
        x1c = _short_conv(x1_ref, cw_ref, cb_ref, 1, c, stage_ref, n_rows)
        vc = _short_conv(v_ref, cw_ref, cb_ref, 2, c, stage_ref, n_rows)
        w_ref[rows, :] = x1c * vc
        return carry
    lax.fori_loop(0, n_chunks, make_w, 0)

    _outer_forward(w_ref, s1_ref, ab_ref, FFT_N1 // 2)

    def per_k1(k1, carry):
        x = jnp.dot(g_ref[k1], ab_ref[k1].astype(BF16), preferred_element_type=F32)
        kf = kf_ref[k1]
        xr, xi = x[:FFT_N2], x[FFT_N2:]
        kr, ki = kf[:FFT_N2], kf[FFT_N2:]
        y = jnp.concatenate([xr * kr - xi * ki, xr * ki + xi * kr], axis=0).astype(BF16)
        ab_ref[k1] = jnp.dot(ginv_ref[k1], y, preferred_element_type=F32)
        return carry
    lax.fori_loop(0, FFT_K1, per_k1, 0, unroll=K1_UNROLL)

    def outer_inverse(n2h, carry):
        r0 = pl.multiple_of(n2h * SUBLANES, SUBLANES)
        tiles = [ab_ref[k1, pl.ds(ri * FFT_N2 + r0, SUBLANES), :] for k1 in range(FFT_K1) for ri in range(2)
                 if 2 * k1 + ri not in _REAL_ONLY_COLS]
        rhs = jnp.concatenate(tiles, axis=0).astype(BF16)
        out = jnp.dot(i2_ref[...], rhs, preferred_element_type=F32)
        for n1 in range(FFT_N1 // 2):
            rows = pl.ds(n1 * FFT_N2 + r0, SUBLANES)
            w_ref[rows, :] = out[n1 * SUBLANES:(n1 + 1) * SUBLANES, :] + hb_ref[...] * w_ref[rows, :]
        return carry
    lax.fori_loop(0, FFT_N2 // SUBLANES, outer_inverse, 0, unroll=OUTER_UNROLL)

    def finish(c, carry):
        rows = pl.ds(pl.multiple_of(c * SHORT_CHUNK, SHORT_CHUNK), SHORT_CHUNK)
        x0c = _short_conv(x0_ref, cw_ref, cb_ref, 0, c, stage_ref, n_rows)
        o_ref[rows, :] = (x0c * w_ref[rows, :] * zs_ref[rows, :].astype(F32)).astype(o_ref.dtype)
        return carry
    lax.fori_loop(0, n_chunks, finish, 0)


def _long_conv(x0, x1, v, zs, conv_w, conv_b, h_bias, kf, tables, *, cb=MXU_TILE):
    bsz, seq, e = x0.shape
    assert seq == FFT_N // 2
    cb = min(cb, e)
    cw = conv_w.reshape(3, 3, e).transpose(1, 0, 2)
    cbias = conv_b.reshape(3, 1, e)
    tok = pl.BlockSpec((None, seq, cb), lambda c, b: (b, 0, c))
    consts = [tables["s1_half"], tables["g"], tables["ginv"], tables["i2"]]
    return pl.pallas_call(
        _long_conv_kernel,
        grid=(e // cb, bsz),
        in_specs=[tok, tok, tok, tok,
                  pl.BlockSpec((3, 3, cb), lambda c, b: (0, 0, c)),
                  pl.BlockSpec((3, 1, cb), lambda c, b: (0, 0, c)),
                  pl.BlockSpec((1, cb), lambda c, b: (0, c)),
                  pl.BlockSpec((FFT_K1, 2 * FFT_N2, cb), lambda c, b: (0, 0, c), pipeline_mode=pl.Buffered(1))]
                 + [_resident(a.shape, 2) for a in consts],
        out_specs=tok,
        out_shape=jax.ShapeDtypeStruct((bsz, seq, e), BF16),
        scratch_shapes=[pltpu.VMEM((seq, cb), F32),
                        pltpu.VMEM((FFT_K1, 2 * FFT_N2, cb), F32),
                        pltpu.VMEM((SHORT_CHUNK + 2 * STAGE_PAD, cb), F32)],
        compiler_params=_params(("arbitrary", "arbitrary")),
        name="long_conv",
    )(x0, x1, v, zs, cw, cbias, h_bias.reshape(1, e), kf, *consts)


def _final_norm_kernel(h_ref, g_ref, o_ref):
    x = h_ref[...]
    ms = jnp.mean(x * x, axis=-1, keepdims=True)
    o_ref[...] = x * lax.rsqrt(ms + EPS) * g_ref[...]


def _final_norm(h, g, *, tm=1024):
    bn, t, d = h.shape
    m = bn * t
    tm = min(tm, m)
    out = pl.pallas_call(
        _final_norm_kernel,
        grid=(m // tm,),
        in_specs=[pl.BlockSpec((tm, d), lambda i: (i, 0)),
                  pl.BlockSpec((1, d), lambda i: (0, 0))],
        out_specs=pl.BlockSpec((tm, d), lambda i: (i, 0)),
        out_shape=jax.ShapeDtypeStruct((m, d), F32),
        compiler_params=_params(("arbitrary",)),
        name="final_norm",
    )(h.reshape(m, d), g.reshape(1, d))
    return out.reshape(bn, t, d)


def kernel(x, c, ctx, c_ctx, norm_g, ada_w, ada_b, final_g, cv_w_in, cv_dw_w, cv_dw_b, cv_ln_g, cv_ln_b, cv_w_out, ml_w_in, ml_conv_w, ml_conv_b, ml_w_q, ml_w_k, ml_w_v, ml_w_o, ml_b_o, ml_w_gates, ml_b_gates, ml_mh_g, ml_skip, ml_w_out, hy_w_in, hy_conv_w, hy_conv_b, hy_f_w1, hy_f_b1, hy_f_freq1, hy_f_w2, hy_f_b2, hy_f_freq2, hy_f_w3, hy_f_b3, hy_h_bias, hy_w_out):
    depth = norm_g.shape[0]
    bsz, seq, d = x.shape
    n_ctx = ctx.shape[1]
    readers = [i for i in range(depth) if i % N_MIXERS == 1]
    last_reader = readers[-1] if readers else -1

    cond_rows = -(-(bsz + 1) // SUBLANES) * SUBLANES
    cond = jnp.concatenate([c, c_ctx[None], jnp.zeros((cond_rows - bsz - 1, d), F32)], axis=0)
    ada = _ada_params(cond, ada_w, ada_b)

    h_lat, h_ctx = x, ctx
    norm_fused = (depth - 1) % N_MIXERS != 1
    for i in range(depth):
        kind, j = i % N_MIXERS, i // N_MIXERS
        close_g = final_g if (norm_fused and i == depth - 1) else None
        ctx_in = i <= last_reader
        ctx_out = i < last_reader
        sh, sc, gt = jnp.split(ada[i, :bsz], 3, axis=-1)
        sh_c, sc_c, gt_c = (jnp.broadcast_to(a, (bsz, d)) for a in jnp.split(ada[i, bsz:bsz + 1], 3, axis=-1))
        if kind == 0:
            w_in = cv_w_in[j].astype(BF16)
            w_out = cv_w_out[j].astype(BF16)
            streams = [(h_lat, sh, sc, gt, GRID_W, _conv_mid_dft)]
            if ctx_out:
                streams.append((h_ctx.reshape(1, bsz * n_ctx, d), sh_c[:1], sc_c[:1], gt_c[:1], n_ctx, _conv_mid))
            new = []
            for h, s_h, s_c, g_t, seg, conv_mid in streams:
                y, zs = _in_proj(h, norm_g[i], s_h, s_c, w_in, n_groups=3, n_out=2, epilogue=_epilogue_conv,
                                 tm=2 * IN_PROJ_ROWS)
                yb = conv_mid(y, zs, cv_dw_w[j], cv_dw_b[j], cv_ln_g[j], cv_ln_b[j], seg=seg)
                new.append(_out_matmul(yb, w_out, h, g_t, final_g=close_g if h is h_lat else None))
            h_lat = new[0]
            if ctx_out:
                h_ctx = new[1].reshape(bsz, n_ctx, d)
        elif kind == 1:
            assert ctx_in and not ctx_out
            w_in = ml_w_in[j].astype(BF16)
            w_out = ml_w_out[j].astype(BF16)
            lead = (-n_ctx) % IN_PROJ_ROWS
            h_all = jnp.concatenate([jnp.zeros((bsz, lead, d), F32), h_ctx, h_lat], axis=1)
            sh_all = jnp.concatenate([sh, sh_c[:1]], axis=0)
            sc_all = jnp.concatenate([sc, sc_c[:1]], axis=0)
            xm, zs = _in_proj(h_all, norm_g[i], sh_all, sc_all, w_in, n_groups=2, n_out=2,
                              epilogue=_epilogue_gate_last, n_ctx=lead + n_ctx, tm=IN_PROJ_ROWS, tn=1024)
            q, k, kt, v, o, xc, gc, gr = _mlstm_pre(
                xm, ml_conv_w[j], ml_conv_b[j], ml_w_q[j], ml_w_k[j], ml_w_v[j], ml_w_o[j], ml_b_o[j],
                ml_w_gates[j], ml_b_gates[j], lead=lead, n_ctx=n_ctx, n_heads=MLSTM_HEADS)
            hs = _mlstm_scan(q, k, kt, v, gc, gr, n_heads=MLSTM_HEADS, n_ctx=n_ctx)
            h_lat = _mlstm_out(hs, o, xc, zs, ml_mh_g[j], ml_skip[j], w_out, h_lat, gt,
                               lead=lead, n_ctx=n_ctx, n_heads=MLSTM_HEADS)
        else:
            assert not ctx_out
            w_in = hy_w_in[j].astype(BF16)
            w_out = hy_w_out[j].astype(BF16)
            x0, x1, v, zs = _in_proj(h_lat, norm_g[i], sh, sc, w_in, n_groups=4, n_out=4,
                                     epilogue=_epilogue_gate_last, tm=2 * IN_PROJ_ROWS, tn=256)
            tables = _dft_tables()
            kfull = _hyena_filter(seq, hy_f_w1[j], hy_f_b1[j], hy_f_freq1[j], hy_f_w2[j], hy_f_b2[j],
                                  hy_f_freq2[j], hy_f_w3[j], hy_f_b3[j])
            kf = _filter_spectrum(kfull, tables)
            yb = _long_conv(x0, x1, v, zs, hy_conv_w[j], hy_conv_b[j], hy_h_bias[j], kf, tables)
            h_lat = _out_matmul(yb, w_out, h_lat, gt, final_g=close_g)
    return h_lat if norm_fused else _final_norm(h_lat, final_g)
```

```python
import functools
import math

import jax
import jax.numpy as jnp
import numpy as np
from jax import lax
from jax.experimental import pallas as pl
from jax.experimental.pallas import tpu as pltpu

GRID_W = 64
N_MIXERS = 3
EPS = 1e-6
MLSTM_HEADS = 8
HYENA_EMB_DIM = 33
HYENA_FAST_DECAY = 0.3
HYENA_SLOW_DECAY = 1.5
HYENA_DECAY_TARGET = 1e-2

V7X_VMEM_LIMIT_BYTES = 56 * 1024 * 1024
SUBLANES = 8
BF16_ROWS = 16
LANES = 128
MXU_TILE = 256

F32 = jnp.float32
BF16 = jnp.bfloat16


def _tiles(m, n, tm, tn):
    tm, tn = min(tm, m), min(tn, n)
    assert m % tm == 0 and n % tn == 0, (m, n, tm, tn)
    return tm, tn


def _sigmoid(x):
    return 1.0 / (1.0 + jnp.exp(-x))


def _silu(x):
    return x * _sigmoid(x)


def _dot_bf16(a, b):
    return jnp.dot(a.astype(BF16), b.astype(BF16), preferred_element_type=F32)


def _params(semantics):
    return pltpu.CompilerParams(dimension_semantics=semantics,
                                vmem_limit_bytes=V7X_VMEM_LIMIT_BYTES)


def _resident(shape, n_grid):
    zeros = (0,) * len(shape)
    return pl.BlockSpec(shape, lambda *_: zeros, pipeline_mode=pl.Buffered(1))


def _ada_kernel(c_ref, w_ref, b_ref, o_ref):
    o_ref[0] = _dot_bf16(_silu(c_ref[...]), w_ref[0]) + b_ref[0]


def _ada_params(cond, ada_w, ada_b, *, tn=512):
    depth, d, n = ada_w.shape
    r = cond.shape[0]
    return pl.pallas_call(
        _ada_kernel,
        grid=(depth, n // tn),
        in_specs=[pl.BlockSpec((r, d), lambda i, j: (0, 0)),
                  pl.BlockSpec((1, d, tn), lambda i, j: (i, 0, j)),
                  pl.BlockSpec((1, 1, tn), lambda i, j: (i, 0, j))],
        out_specs=pl.BlockSpec((1, r, tn), lambda i, j: (i, 0, j)),
        out_shape=jax.ShapeDtypeStruct((depth, r, n), F32),
        compiler_params=_params(("arbitrary", "arbitrary")),
        name="ada_params",
    )(cond, ada_w, ada_b.reshape(depth, 1, n))


def _epilogue_conv(a, g, z):
    return a * _sigmoid(g), _silu(z)


def _epilogue_gate_last(*accs):
    return accs[:-1] + (_silu(accs[-1]),)


def _in_proj_kernel(*refs, n_groups, epilogue):
    h_ref, g_ref, sh_ref, sc_ref = refs[:4]
    w_refs = refs[4:4 + n_groups]
    o_refs = refs[4 + n_groups:-1]
    u_ref = refs[-1]

    @pl.when(pl.program_id(1) == 0)
    def _():
        x = h_ref[...]
        ms = jnp.mean(x * x, axis=-1, keepdims=True)
        y = x * lax.rsqrt(ms + EPS) * g_ref[...]
        u_ref[...] = (y * (1.0 + sc_ref[0]) + sh_ref[0]).astype(u_ref.dtype)

    u = u_ref[...]
    outs = epilogue(*[jnp.dot(u, w[...], preferred_element_type=F32) for w in w_refs])
    for o_ref, val in zip(o_refs, outs, strict=True):
        o_ref[...] = val.astype(o_ref.dtype)


IN_PROJ_ROWS = 512


def _in_proj(h, g, sh, sc, w, *, n_groups, n_out, epilogue, n_ctx=0, tm=IN_PROJ_ROWS, tn=512):
    bn, t, d = h.shape
    e = w.shape[1] // n_groups
    tm, tn = _tiles(t, e, tm, tn)
    assert n_ctx % tm == 0
    tiles_per_seq, ctx_tiles = t // tm, n_ctx // tm
    col_tiles = e // tn
    n_mod = sh.shape[0]

    def mod_row(i, j):
        return (jnp.where(i % tiles_per_seq < ctx_tiles, n_mod - 1, i // tiles_per_seq), 0, 0)

    w_specs = [pl.BlockSpec((d, tn), functools.partial(lambda i, j, k: (0, k * col_tiles + j), k=k))
               for k in range(n_groups)]
    outs = pl.pallas_call(
        functools.partial(_in_proj_kernel, n_groups=n_groups, epilogue=epilogue),
        grid=(bn * tiles_per_seq, col_tiles),
        in_specs=[
            pl.BlockSpec((tm, d), lambda i, j: (i, 0)),
            pl.BlockSpec((1, d), lambda i, j: (0, 0)),
            pl.BlockSpec((1, 1, d), mod_row),
            pl.BlockSpec((1, 1, d), mod_row),
        ] + w_specs,
        out_specs=[pl.BlockSpec((tm, tn), lambda i, j: (i, j))] * n_out,
        out_shape=[jax.ShapeDtypeStruct((bn * t, e), BF16)] * n_out,
        scratch_shapes=[pltpu.VMEM((tm, d), BF16)],
        compiler_params=_params(("arbitrary", "arbitrary")),
        name="in_proj",
    )(h.reshape(bn * t, d), g.reshape(1, d), sh.reshape(n_mod, 1, d),
      sc.reshape(n_mod, 1, d), *([w] * n_groups))
    return [o.reshape(bn, t, e) for o in outs]


def _out_matmul_kernel(y_ref, w_ref, h_ref, gt_ref, o_ref):
    acc = jnp.dot(y_ref[...], w_ref[...], preferred_element_type=F32)
    o_ref[...] = h_ref[...] + gt_ref[0] * acc


def _out_matmul_norm_kernel(y_ref, w_ref, h_ref, gt_ref, g_ref, o_ref):
    acc = jnp.dot(y_ref[...], w_ref[...], preferred_element_type=F32)
    x = h_ref[...] + gt_ref[0] * acc
    ms = jnp.mean(x * x, axis=-1, keepdims=True)
    o_ref[...] = x * lax.rsqrt(ms + EPS) * g_ref[...]


def _out_matmul(y, w, h, gt, *, final_g=None, tm=512):
    bn, t, e = y.shape
    d = w.shape[1]
    tm, tn = _tiles(t, d, tm, d)
    tiles_per_seq = t // tm
    in_specs = [
        pl.BlockSpec((tm, e), lambda i, j: (i, 0)),
        _resident((e, d), 2),
        pl.BlockSpec((tm, tn), lambda i, j: (i, j)),
        pl.BlockSpec((1, 1, tn), lambda i, j: (i // tiles_per_seq, 0, j)),
    ]
    args = [y.reshape(bn * t, e), w, h.reshape(bn * t, d), gt.reshape(bn, 1, d)]
    if final_g is not None:
        in_specs.append(_resident((1, d), 2))
        args.append(final_g.reshape(1, d))
    out = pl.pallas_call(
        _out_matmul_kernel if final_g is None else _out_matmul_norm_kernel,
        grid=(bn * tiles_per_seq, d // tn),
        in_specs=in_specs,
        out_specs=pl.BlockSpec((tm, tn), lambda i, j: (i, j)),
        out_shape=jax.ShapeDtypeStruct((bn * t, d), F32),
        compiler_params=_params(("arbitrary", "arbitrary")),
        name="out_matmul",
    )(*args)
    return out.reshape(bn, t, d)


CONV_PAD = 16
LN_UNROLL = 16


def _conv_mid_kernel(y_ref, zs_ref, w_ref, b_ref, lg_ref, lb_ref, o_ref, xp_ref, xs_ref, cv_ref, *, seg, width):
    tb, e = y_ref.shape
    n_seg = tb // seg
    half = width // 2
    rows_s = xs_ref.shape[2]
    rb = SUBLANES
    zeros = jnp.zeros((CONV_PAD, e), F32)
    for s in range(n_seg):
        xp_ref[s, 0:CONV_PAD, :] = zeros
        xp_ref[s, CONV_PAD:CONV_PAD + seg, :] = y_ref[s * seg:(s + 1) * seg, :].astype(F32)
        xp_ref[s, CONV_PAD + seg:2 * CONV_PAD + seg, :] = zeros

    def lane_block(cb, carry):
        lanes = pl.ds(pl.multiple_of(cb * LANES, LANES), LANES)
        for sh in range(rb):
            for s in range(n_seg):
                xs_ref[sh, s] = xp_ref[s, sh:sh + rows_s, lanes]
        taps = [w_ref[k:k + 1, lanes] for k in range(width)]
        bias = b_ref[:, lanes]
        for s in range(n_seg):
            for r in range(seg // rb):
                acc = bias
                for k in range(width):
                    off = CONV_PAD - half + k
                    base = (off // rb) * rb + r * rb
                    acc = acc + taps[k] * xs_ref[off % rb, s, base:base + rb, :]
                cv_ref[s * seg + r * rb:s * seg + (r + 1) * rb, lanes] = acc
        return carry

    lax.fori_loop(0, e // LANES, lane_block, 0)

    def row_block(i, carry):
        rows = pl.ds(pl.multiple_of(i * rb, rb), rb)
        cv = cv_ref[rows, :]
        mu = jnp.mean(cv, axis=-1, keepdims=True)
        xc = cv - mu
        var = jnp.mean(xc * xc, axis=-1, keepdims=True)
        yn = xc * lax.rsqrt(var + EPS) * lg_ref[...] + lb_ref[...]
        o_ref[rows, :] = (_silu(yn) * zs_ref[rows, :].astype(F32)).astype(o_ref.dtype)
        return carry

    lax.fori_loop(0, tb // rb, row_block, 0, unroll=LN_UNROLL)


def _conv_mid(y, zs, dw_w, dw_b, ln_g, ln_b, *, seg, tb=256):
    bn, t, e = y.shape
    width = dw_w.shape[0]
    m = bn * t
    tb = min(tb, m)
    assert m % tb == 0 and tb % seg == 0 and t % seg == 0 and width // 2 <= CONV_PAD
    row = lambda a: a.reshape(1, e)
    tok = pl.BlockSpec((tb, e), lambda i: (i, 0))
    vec = pl.BlockSpec((1, e), lambda i: (0, 0))
    out = pl.pallas_call(
        functools.partial(_conv_mid_kernel, seg=seg, width=width),
        grid=(m // tb,),
        in_specs=[tok, tok, pl.BlockSpec((width, e), lambda i: (0, 0)), vec, vec, vec],
        out_specs=tok,
        out_shape=jax.ShapeDtypeStruct((m, e), BF16),
        scratch_shapes=[pltpu.VMEM((tb // seg, seg + 2 * CONV_PAD, e), F32),
                        pltpu.VMEM((SUBLANES, tb // seg, seg + 2 * CONV_PAD - SUBLANES, LANES), F32),
                        pltpu.VMEM((tb, e), F32)],
        compiler_params=_params(("arbitrary",)),
        name="conv_mid",
    )(y.reshape(m, e), zs.reshape(m, e), dw_w, row(dw_b), row(ln_g), row(ln_b))
    return out.reshape(bn, t, e)


CONV_DFT = 96
CONV_FREQS = CONV_DFT // 2 + 1
CONV_HALF = -(-CONV_FREQS // SUBLANES) * SUBLANES
CONV_LANE_UNROLL = 8


def _conv_dft_tables(seg, width, n_seg):
    f = np.arange(CONV_FREQS)
    th = 2 * np.pi * np.outer(f, np.arange(seg)) / CONV_DFT
    fwd = np.zeros((2 * CONV_HALF, seg))
    fwd[:CONV_FREQS] = np.cos(th)
    fwd[CONV_HALF:CONV_HALF + CONV_FREQS] = -np.sin(th)
    c = np.where((f == 0) | (f == CONV_DFT // 2), 1.0, 2.0) / CONV_DFT
    inv = np.zeros((seg, 2 * CONV_HALF))
    inv[:, :CONV_FREQS] = (c[:, None] * np.cos(th)).T
    inv[:, CONV_HALF:CONV_HALF + CONV_FREQS] = (-c[:, None] * np.sin(th)).T
    eye = np.eye(n_seg)
    thw = 2 * np.pi * np.outer(f, np.arange(width) - width // 2) / CONV_DFT
    wtab = np.zeros((2 * CONV_HALF, width))
    wtab[:CONV_FREQS] = np.cos(thw)
    wtab[CONV_HALF:CONV_HALF + CONV_FREQS] = np.sin(thw)
    bf = lambda a: jnp.asarray(a, F32).astype(BF16)
    return bf(np.kron(eye, fwd)), bf(np.kron(eye, inv)), jnp.asarray(wtab, F32)


def _conv_dft_kernel(y_ref, zs_ref, w_ref, b_ref, lg_ref, lb_ref, fwd_ref, inv_ref, wtab_ref, o_ref,
                     h_ref, cv_ref, *, n_seg):
    tb, e = y_ref.shape
    rb = SUBLANES

    @pl.when(pl.program_id(0) == 0)
    def _():
        h_ref[...] = jnp.dot(wtab_ref[...], w_ref[...], preferred_element_type=F32,
                             precision=lax.Precision.HIGHEST)

    def lane_block(cb, carry):
        lanes = pl.ds(pl.multiple_of(cb * MXU_TILE, MXU_TILE), MXU_TILE)
        x = jnp.dot(fwd_ref[...], y_ref[:, lanes], preferred_element_type=F32)
        hr = h_ref[0:CONV_HALF, lanes]
        hi = h_ref[CONV_HALF:2 * CONV_HALF, lanes]
        parts = []
        for s in range(n_seg):
            xr = x[s * 2 * CONV_HALF:s * 2 * CONV_HALF + CONV_HALF]
            xi = x[s * 2 * CONV_HALF + CONV_HALF:(s + 1) * 2 * CONV_HALF]
            parts += [xr * hr - xi * hi, xr * hi + xi * hr]
        yf = jnp.concatenate(parts, axis=0).astype(BF16)
        cv_ref[:, lanes] = jnp.dot(inv_ref[...], yf, preferred_element_type=F32) + b_ref[:, lanes]
        return carry

    lax.fori_loop(0, e // MXU_TILE, lane_block, 0, unroll=CONV_LANE_UNROLL)

    def row_block(i, carry):
        rows = pl.ds(pl.multiple_of(i * rb, rb), rb)
        cv = cv_ref[rows, :]
        mu = jnp.mean(cv, axis=-1, keepdims=True)
        xc = cv - mu
        var = jnp.mean(xc * xc, axis=-1, keepdims=True)
        yn = xc * lax.rsqrt(var + EPS) * lg_ref[...] + lb_ref[...]
        o_ref[rows, :] = (_silu(yn) * zs_ref[rows, :].astype(F32)).astype(o_ref.dtype)
        return carry

    lax.fori_loop(0, tb // rb, row_block, 0, unroll=LN_UNROLL)


def _conv_mid_dft(y, zs, dw_w, dw_b, ln_g, ln_b, *, seg, tb=256):
    bn, t, e = y.shape
    width = dw_w.shape[0]
    m = bn * t
    tb = min(tb, m)
    assert m % tb == 0 and tb % seg == 0 and t % seg == 0 and seg + width - 1 <= CONV_DFT
    n_seg = tb // seg
    fwd, inv, wtab = _conv_dft_tables(seg, width, n_seg)
    row = lambda a: a.reshape(1, e)
    tok = pl.BlockSpec((tb, e), lambda i: (i, 0))
    out = pl.pallas_call(
        functools.partial(_conv_dft_kernel, n_seg=n_seg),
        grid=(m // tb,),
        in_specs=[tok, tok, _resident((width, e), 1), _resident((1, e), 1), _resident((1, e), 1),
                  _resident((1, e), 1), _resident(fwd.shape, 1), _resident(inv.shape, 1),
                  _resident(wtab.shape, 1)],
        out_specs=tok,
        out_shape=jax.ShapeDtypeStruct((m, e), BF16),
        scratch_shapes=[pltpu.VMEM((2 * CONV_HALF, e), F32), pltpu.VMEM((tb, e), F32)],
        compiler_params=_params(("arbitrary",)),
        name="conv_mid_dft",
    )(y.reshape(m, e), zs.reshape(m, e), dw_w, row(dw_b), row(ln_g), row(ln_b), fwd, inv, wtab)
    return out.reshape(bn, t, e)


STAGE_PAD = 8


def _expand_block_diag(w):
    g, bi, bo = w.shape
    per = MXU_TILE // bi
    wt = w.reshape(g // per, per, bi, bo)
    eye = jnp.eye(per, dtype=w.dtype)
    return jnp.einsum('tgio,gh->tgiho', wt, eye).reshape(g // per, per * bi, per * bo)


def _mlstm_pre_kernel(xm_ref, prev_ref, next_ref, cw_ref, cb_ref, wq_ref, wk_ref, wkt_ref, wv_ref, wo_ref,
                      bo_ref, wg_ref, bg_ref,
                      q_ref, k_ref, kt_ref, v_ref, o_ref, xc_ref, gc_ref, gr_ref, stage_ref,
                      *, tiles_per_seq, ctx_tiles, n_heads):
    tb, e = xm_ref.shape
    n_g = gc_ref.shape[1]
    q_scale = (e // n_heads) ** -0.5
    t = pl.program_id(0) % tiles_per_seq
    has_prev = jnp.logical_and(t != 0, t != ctx_tiles).astype(F32)
    has_next = jnp.logical_and(t != ctx_tiles - 1, t != tiles_per_seq - 1).astype(F32)
    gc = jnp.zeros(gc_ref.shape, F32) + bg_ref[...]
    nt = (((1,), (1,)), ((), ()))
    for j in range(e // MXU_TILE):
        lanes = slice(j * MXU_TILE, (j + 1) * MXU_TILE)
        xm = xm_ref[:, lanes]
        stage_ref[STAGE_PAD - 1:STAGE_PAD, :] = \
            prev_ref[:, lanes].astype(F32)[BF16_ROWS - 1:BF16_ROWS, :] * has_prev
        stage_ref[STAGE_PAD:STAGE_PAD + tb, :] = xm.astype(F32)
        stage_ref[STAGE_PAD + tb:STAGE_PAD + tb + 1, :] = next_ref[:, lanes].astype(F32)[0:1, :] * has_next
        cw = cw_ref[:, lanes]
        pre = (cb_ref[:, lanes] + cw[0:1] * stage_ref[STAGE_PAD - 1:STAGE_PAD - 1 + tb, :]
               + cw[1:2] * stage_ref[STAGE_PAD:STAGE_PAD + tb, :]
               + cw[2:3] * stage_ref[STAGE_PAD + 1:STAGE_PAD + 1 + tb, :])
        xcb = _silu(pre).astype(BF16)
        xc_ref[:, lanes] = xcb
        q = jnp.dot(xcb, wq_ref[j], preferred_element_type=F32)
        k = jnp.dot(xcb, wk_ref[j], preferred_element_type=F32)
        v = jnp.dot(xm, wv_ref[j], preferred_element_type=F32)
        o = jnp.dot(xcb, wo_ref[j], preferred_element_type=F32) + bo_ref[:, lanes]
        kt_ref[lanes, :] = lax.dot_general(wkt_ref[j], xcb, nt, preferred_element_type=F32).astype(BF16)
        qb, kb, vb = q.astype(BF16), k.astype(BF16), v.astype(BF16)
        q_ref[:, lanes] = (q * q_scale).astype(BF16)
        k_ref[:, lanes] = kb
        v_ref[:, lanes] = vb
        o_ref[:, lanes] = _sigmoid(o).astype(BF16)
        for i, a in enumerate((qb, kb, vb)):
            gc = gc + jnp.dot(a, wg_ref[i, lanes, :], preferred_element_type=F32)
    col = lax.broadcasted_iota(jnp.int32, gc.shape, 1)
    gc = jnp.where((col // n_heads) % 2 == 1, _log_sigmoid(gc), gc)
    gc_ref[...] = gc
    gc_wide = jnp.concatenate([gc, jnp.zeros((tb, LANES - n_g), F32)], axis=1)
    gr_ref[...] = gc_wide.T[:n_g, :]


def _mlstm_pre(xm, conv_w, conv_b, w_q, w_k, w_v, w_o, b_o, w_gates, b_gates, *, lead, n_ctx, n_heads, tb=256):
    bsz, t_all, e = xm.shape
    t = t_all - lead
    assert t % tb == 0 and n_ctx % tb == 0 and lead % tb == 0
    tiles_per_seq, ctx_tiles = t // tb, n_ctx // tb
    tiles_all, lead_tiles = t_all // tb, lead // tb
    m = bsz * t
    n_g = w_gates.shape[1]
    hpt = tb // BF16_ROWS
    n_halo = bsz * t_all // BF16_ROWS

    def src_tile(i):
        return (i // tiles_per_seq) * tiles_all + lead_tiles + i % tiles_per_seq
    bd = lambda w: _expand_block_diag(w).astype(BF16)
    wq, wk, wv, wo = bd(w_q), bd(w_k), bd(w_v), bd(w_o)
    wkt = jnp.swapaxes(wk, 1, 2)
    wg = w_gates.reshape(3, e, n_g).astype(BF16)
    tok = pl.BlockSpec((tb, e), lambda i: (i, 0))
    row = lambda a: a.reshape(1, -1)
    consts = [conv_w, row(conv_b), wq, wk, wkt, wv, wo, row(b_o), wg, row(b_gates)]
    sds = jax.ShapeDtypeStruct
    xm2 = xm.reshape(bsz * t_all, e)
    outs = pl.pallas_call(
        functools.partial(_mlstm_pre_kernel, tiles_per_seq=tiles_per_seq, ctx_tiles=ctx_tiles,
                          n_heads=n_heads),
        grid=(m // tb,),
        in_specs=[pl.BlockSpec((tb, e), lambda i: (src_tile(i), 0)),
                  pl.BlockSpec((BF16_ROWS, e), lambda i: (jnp.maximum(src_tile(i) * hpt - 1, 0), 0)),
                  pl.BlockSpec((BF16_ROWS, e), lambda i: (jnp.minimum((src_tile(i) + 1) * hpt, n_halo - 1), 0))]
                 + [_resident(a.shape, 1) for a in consts],
        out_specs=[tok, tok,
                   pl.BlockSpec((None, e, tb), lambda i: (i // tiles_per_seq, 0, i % tiles_per_seq)),
                   tok, tok, tok,
                   pl.BlockSpec((tb, n_g), lambda i: (i, 0)),
                   pl.BlockSpec((None, n_g, tb), lambda i: (i // tiles_per_seq, 0, i % tiles_per_seq))],
        out_shape=[sds((m, e), BF16), sds((m, e), BF16), sds((bsz, e, t), BF16), sds((m, e), BF16),
                   sds((m, e), BF16), sds((m, e), BF16), sds((m, n_g), F32), sds((bsz, n_g, t), F32)],
        scratch_shapes=[pltpu.VMEM((tb + 2 * STAGE_PAD, MXU_TILE), F32)],
        compiler_params=_params(("arbitrary",)),
        name="mlstm_pre",
    )(xm2, xm2, xm2, *consts)
    q, k, kt, v, o, xc, gc, gr = outs
    r3 = lambda a: a.reshape(bsz, t, -1)
    return r3(q), r3(k), kt, r3(v), r3(o), r3(xc), r3(gc), gr


SCAN_CHUNK = 256


def _log_sigmoid(x):
    return jnp.minimum(x, 0.0) - jnp.log1p(jnp.exp(-jnp.abs(x)))


SCAN_HEADS_PER_STEP = 4


def _split_bf16(x):
    hi = x.astype(BF16)
    return hi, (x - hi.astype(F32)).astype(BF16)


def _mlstm_scan_kernel(q_ref, k_ref, kt_ref, v_ref, gc_ref, gr_ref, o_ref,
                       c_ref, cb_ref, n_ref, m_ref, *, n_heads, heads_per_step):
    direction = pl.program_id(0)

    @pl.when(pl.program_id(3) == 0)
    def _():
        c_ref[...] = jnp.zeros_like(c_ref)
        cb_ref[...] = jnp.zeros_like(cb_ref)
        n_ref[...] = jnp.zeros_like(n_ref)
        m_ref[...] = jnp.full_like(m_ref, -1e30)

    L = q_ref.shape[1]
    dh = q_ref.shape[2] // heads_per_step
    n_g = gc_ref.shape[2]
    gc = gc_ref[0]
    gr = gr_ref[0]
    lane = lax.broadcasted_iota(jnp.int32, (L, n_g), 1)
    sub = lax.broadcasted_iota(jnp.int32, (n_g, L), 0)
    sign = 1 - 2 * direction
    diff = (lax.broadcasted_iota(jnp.int32, (L, L), 0)
            - lax.broadcasted_iota(jnp.int32, (L, L), 1)) * sign
    causal = diff >= 0
    anti = diff <= 0

    for hh in range(heads_per_step):
        lanes = slice(hh * dh, (hh + 1) * dh)
        head = pl.program_id(2) * heads_per_step + hh
        col_i = direction * (2 * n_heads) + head
        col_f = col_i + n_heads
        lf_col = jnp.sum(jnp.where(lane == col_f, gc, 0.0), axis=1, keepdims=True)
        li_row = jnp.sum(jnp.where(sub == col_i, gr, 0.0), axis=0, keepdims=True)
        lf_row = jnp.sum(jnp.where(sub == col_f, gr, 0.0), axis=0, keepdims=True)
        b_col = jnp.sum(jnp.where(causal, lf_row, 0.0), axis=1, keepdims=True)
        b_row = jnp.sum(jnp.where(anti, lf_col, 0.0), axis=0, keepdims=True)
        total = jnp.sum(lf_row, axis=1, keepdims=True)

        m = m_ref[hh]
        log_d = jnp.where(causal, b_col - b_row + li_row, -jnp.inf)
        log_inter = b_col + m
        m_t = jnp.maximum(log_inter, jnp.max(log_d, axis=1, keepdims=True))
        dmat = jnp.exp(log_d - m_t)
        a = jnp.exp(log_inter - m_t)

        q = q_ref[0, :, lanes]
        kt = kt_ref[0, lanes, :]
        v = v_ref[0, :, lanes]
        s = jnp.dot(q, kt, preferred_element_type=F32) * dmat
        num = a * jnp.dot(q, cb_ref[hh], preferred_element_type=F32) \
            + jnp.dot(s.astype(BF16), v, preferred_element_type=F32)
        n_hi, n_lo = _split_bf16(n_ref[hh])
        qn2 = lax.dot_general(q, jnp.concatenate([n_hi, n_lo], axis=0), (((1,), (1,)), ((), ())),
                              preferred_element_type=F32)
        qn = qn2[:, 0:1] + qn2[:, SUBLANES:SUBLANES + 1]
        den = a * qn + jnp.sum(s, axis=1, keepdims=True)
        o_ref[0, 0, :, lanes] = (num / jnp.maximum(jnp.abs(den), jnp.exp(-m_t))).astype(o_ref.dtype)

        log_w_row = total - b_row + li_row
        m_new = jnp.maximum(total + m, jnp.max(log_w_row, axis=1, keepdims=True))
        decay = jnp.exp(total + m - m_new)
        w_hi, w_lo = _split_bf16(jnp.exp(log_w_row - m_new))
        kwt = kt * w_hi
        c_new = decay * c_ref[hh] + jnp.dot(kwt, v, preferred_element_type=F32)
        c_ref[hh] = c_new
        cb_ref[hh] = c_new.astype(BF16)
        w2 = jnp.concatenate([jnp.broadcast_to(w_hi, (SUBLANES, L)), jnp.broadcast_to(w_lo, (SUBLANES, L))], axis=0)
        n_add = jnp.dot(w2, k_ref[0, :, lanes], preferred_element_type=F32)
        n_ref[hh] = decay * n_ref[hh] + n_add[0:SUBLANES] + n_add[SUBLANES:2 * SUBLANES]
        m_ref[hh] = m_new


def _mlstm_scan(q, k, kt, v, gc, gr, *, n_heads, n_ctx, chunk=SCAN_CHUNK, heads_per_step=SCAN_HEADS_PER_STEP):
    bsz, t, e = q.shape
    dh = e // n_heads
    hp = heads_per_step
    L = chunk
    assert t % L == 0 and n_ctx % L == 0 and n_heads % hp == 0
    nc, nc_ctx = t // L, n_ctx // L

    def cidx(d, c):
        rev = jnp.where(c < nc_ctx, nc_ctx - 1 - c, nc - 1 - c + nc_ctx)
        return jnp.where(d == 0, c, rev)

    tok = pl.BlockSpec((1, L, hp * dh), lambda d, b, h, c: (b, cidx(d, c), h))
    return pl.pallas_call(
        functools.partial(_mlstm_scan_kernel, n_heads=n_heads, heads_per_step=hp),
        grid=(2, bsz, n_heads // hp, nc),
        in_specs=[
            tok, tok,
            pl.BlockSpec((1, hp * dh, L), lambda d, b, h, c: (b, h, cidx(d, c))),
            tok,
            pl.BlockSpec((1, L, 4 * n_heads), lambda d, b, h, c: (b, cidx(d, c), 0)),
            pl.BlockSpec((1, 4 * n_heads, L), lambda d, b, h, c: (b, 0, cidx(d, c))),
        ],
        out_specs=pl.BlockSpec((1, 1, L, hp * dh), lambda d, b, h, c: (d, b, cidx(d, c), h)),
        out_shape=jax.ShapeDtypeStruct((2, bsz, t, e), BF16),
        scratch_shapes=[pltpu.VMEM((hp, dh, dh), F32), pltpu.VMEM((hp, dh, dh), BF16),
                        pltpu.VMEM((hp, SUBLANES, dh), F32), pltpu.VMEM((hp, 1, 1), F32)],
        compiler_params=_params(("arbitrary",) * 4),
        name="mlstm_scan",
    )(q, k, kt, v, gc, gr)


def _mlstm_out_kernel(hf_ref, hb_ref, o_ref, xc_ref, zs_ref, mh_ref, skip_ref, w_ref, res_ref, gt_ref,
                      out_ref, y_ref, *, n_heads):
    tm, e = o_ref.shape
    dh = e // n_heads
    for hd in range(n_heads):
        lanes = slice(hd * dh, (hd + 1) * dh)
        h = hf_ref[:, lanes].astype(F32) + hb_ref[:, lanes].astype(F32)
        mu = jnp.mean(h, axis=-1, keepdims=True)
        hc = h - mu
        var = jnp.mean(hc * hc, axis=-1, keepdims=True)
        hn = hc * lax.rsqrt(var + EPS) * mh_ref[:, lanes]
        y = (o_ref[:, lanes].astype(F32) * hn + skip_ref[:, lanes] * xc_ref[:, lanes].astype(F32)) \
            * zs_ref[:, lanes].astype(F32)
        y_ref[:, lanes] = y.astype(BF16)
    acc = jnp.dot(y_ref[...], w_ref[...], preferred_element_type=F32)
    out_ref[...] = res_ref[...] + gt_ref[0] * acc


def _mlstm_out(hs, o, xc, zs, mh_g, skip, w_out, h_lat, gt, *, lead, n_ctx, n_heads, tm=256):
    _, bsz, t, e = hs.shape
    n = t - n_ctx
    d = w_out.shape[1]
    assert n % tm == 0 and n_ctx % tm == 0 and lead % tm == 0
    tiles, off = n // tm, n_ctx // tm
    zs_off = off + lead // tm
    tok = pl.BlockSpec((None, tm, e), lambda i: (i // tiles, off + i % tiles, 0))
    zs_tok = pl.BlockSpec((None, tm, e), lambda i: (i // tiles, zs_off + i % tiles, 0))
    row = lambda a: a.reshape(1, -1)
    out = pl.pallas_call(
        functools.partial(_mlstm_out_kernel, n_heads=n_heads),
        grid=(bsz * tiles,),
        in_specs=[pl.BlockSpec((None, None, tm, e), lambda i: (0, i // tiles, off + i % tiles, 0)),
                  pl.BlockSpec((None, None, tm, e), lambda i: (1, i // tiles, off + i % tiles, 0)),
                  tok, tok, zs_tok, _resident((1, e), 1), _resident((1, e), 1), _resident((e, d), 1),
                  pl.BlockSpec((tm, d), lambda i: (i, 0)),
                  pl.BlockSpec((1, 1, d), lambda i: (i // tiles, 0, 0))],
        out_specs=pl.BlockSpec((tm, d), lambda i: (i, 0)),
        out_shape=jax.ShapeDtypeStruct((bsz * n, d), F32),
        scratch_shapes=[pltpu.VMEM((tm, e), BF16)],
        compiler_params=_params(("arbitrary",)),
        name="mlstm_out",
    )(hs, hs, o, xc, zs, row(mh_g), row(skip), w_out, h_lat.reshape(bsz * n, d), gt.reshape(bsz, 1, d))
    return out.reshape(bsz, n, d)


def _filter_features(seq):
    r = np.arange(2 * seq)
    lag = np.where(r < seq, r, 2 * seq - r).clip(0, seq - 1).astype(np.float64)
    t = lag / (seq - 1)
    bands = (HYENA_EMB_DIM - 1) // 2
    ang = 2.0 * math.pi * lag / seq
    fr = np.linspace(1e-4, bands - 1, bands)
    feat = np.concatenate([t[:, None], np.cos(fr[None] * ang[:, None]), -np.sin(fr[None] * ang[:, None])], axis=-1)
    return jnp.asarray(feat, F32), jnp.asarray(t[:, None], F32)


def _filter_kernel(feat_ref, t_ref, w1_ref, b1_ref, f1_ref, w2_ref, b2_ref, f2_ref, w3_ref, b3_ref, dl_ref,
                   o_ref, *, seq):
    tr = o_ref.shape[0]
    h = jnp.sin(f1_ref[...] * (_dot_bf16(feat_ref[...], w1_ref[...]) + b1_ref[...]))
    h = jnp.sin(f2_ref[...] * (_dot_bf16(h, w2_ref[...]) + b2_ref[...]))
    h = _dot_bf16(h, w3_ref[...]) + b3_ref[...]
    out = h * jnp.exp(-t_ref[...] * dl_ref[...])
    row = pl.program_id(0) * tr + lax.broadcasted_iota(jnp.int32, (tr, 1), 0)
    o_ref[...] = jnp.where(row == seq, 0.0, out)


def _hyena_filter(seq, f_w1, f_b1, f_freq1, f_w2, f_b2, f_freq2, f_w3, f_b3, *, tr=512, tn=1024):
    e = f_w3.shape[1] // 2
    feat, t = _filter_features(seq)
    lo = math.log(HYENA_DECAY_TARGET) / HYENA_FAST_DECAY
    hi = math.log(HYENA_DECAY_TARGET) / HYENA_SLOW_DECAY
    deltas = jnp.asarray(np.abs(np.linspace(lo, hi, e)), F32).reshape(1, e)
    fdim = f_w1.shape[1]
    row_tiles, col_tiles = 2 * seq // tr, e // tn
    half = lambda i: i // (row_tiles // 2)
    row = lambda a: a.reshape(1, -1)
    small = lambda shape: pl.BlockSpec(shape, lambda i, j: (0, 0))
    return pl.pallas_call(
        functools.partial(_filter_kernel, seq=seq),
        grid=(row_tiles, col_tiles),
        in_specs=[pl.BlockSpec((tr, HYENA_EMB_DIM), lambda i, j: (i, 0)),
                  pl.BlockSpec((tr, 1), lambda i, j: (i, 0)),
                  small((HYENA_EMB_DIM, fdim)), small((1, fdim)), small((1, fdim)),
                  small((fdim, fdim)), small((1, fdim)), small((1, fdim)),
                  pl.BlockSpec((fdim, tn), lambda i, j: (0, half(i) * col_tiles + j)),
                  pl.BlockSpec((1, tn), lambda i, j: (0, half(i) * col_tiles + j)),
                  pl.BlockSpec((1, tn), lambda i, j: (0, j))],
        out_specs=pl.BlockSpec((tr, tn), lambda i, j: (i, j)),
        out_shape=jax.ShapeDtypeStruct((2 * seq, e), F32),
        compiler_params=_params(("arbitrary", "arbitrary")),
        name="hyena_filter",
    )(feat, t, f_w1, row(f_b1), row(f_freq1), f_w2, row(f_b2), row(f_freq2), f_w3, row(f_b3), deltas)


FFT_N1 = 64
FFT_N2 = 128
FFT_N = FFT_N1 * FFT_N2
FFT_K1 = FFT_N1 // 2 + 1
SHORT_CHUNK = 256
K1_UNROLL = 33
OUTER_UNROLL = 8
_REAL_ONLY_COLS = (1, 2 * (FFT_K1 - 1) + 1)


def _dft_tables():
    n1h = FFT_N1 // 2
    k1 = np.arange(FFT_K1)
    eye = np.eye(SUBLANES)

    def outer_fwd(n_rows):
        th = 2 * np.pi * np.outer(k1, np.arange(n_rows)) / FFT_N1
        f = np.stack([np.cos(th), -np.sin(th)], axis=1).reshape(2 * FFT_K1, n_rows)
        return np.kron(f, eye)

    n2 = np.arange(FFT_N2)
    k2 = np.arange(FFT_N2)
    ph = 2 * np.pi * n2[None, None, :] * (k1[:, None, None] + FFT_N1 * k2[None, :, None]) / FFT_N
    gr, gi = np.cos(ph), -np.sin(ph)
    g = np.concatenate([np.concatenate([gr, -gi], axis=2), np.concatenate([gi, gr], axis=2)], axis=1)
    ginv = np.swapaxes(g, 1, 2)
    th2 = 2 * np.pi * np.outer(np.arange(n1h), k1) / FFT_N1
    c = np.where((k1 == 0) | (k1 == n1h), 1.0, 2.0)[None, :] / FFT_N
    finv = np.stack([c * np.cos(th2), -c * np.sin(th2)], axis=2).reshape(n1h, 2 * FFT_K1)
    finv = finv[:, [col for col in range(2 * FFT_K1) if col not in _REAL_ONLY_COLS]]
    bf = lambda a: jnp.asarray(a, F32).astype(BF16)
    return dict(s1_half=bf(outer_fwd(n1h)), s1_full=bf(outer_fwd(FFT_N1)), g=bf(g), ginv=bf(ginv),
                i2=bf(np.kron(finv, eye)))


def _outer_forward(src_ref, s1_ref, dst_ref, n1_rows):
    def body(n2h, carry):
        r0 = pl.multiple_of(n2h * SUBLANES, SUBLANES)
        tiles = [src_ref[pl.ds(n1 * FFT_N2 + r0, SUBLANES), :] for n1 in range(n1_rows)]
        rhs = jnp.concatenate(tiles, axis=0).astype(BF16)
        out = jnp.dot(s1_ref[...], rhs, preferred_element_type=F32)
        for k1 in range(FFT_K1):
            for ri in range(2):
                row = (2 * k1 + ri) * SUBLANES
                dst_ref[k1, pl.ds(ri * FFT_N2 + r0, SUBLANES), :] = out[row:row + SUBLANES, :]
        return carry
    lax.fori_loop(0, FFT_N2 // SUBLANES, body, 0, unroll=OUTER_UNROLL)


def _filter_spectrum_kernel(k_ref, s1_ref, g_ref, kf_ref, a_ref):
    _outer_forward(k_ref, s1_ref, a_ref, FFT_N1)

    def body(k1, carry):
        kf_ref[k1] = jnp.dot(g_ref[k1], a_ref[k1].astype(BF16), preferred_element_type=F32)
        return carry
    lax.fori_loop(0, FFT_K1, body, 0)


def _filter_spectrum(kfull, tables, *, cb=MXU_TILE):
    e = kfull.shape[1]
    cb = min(cb, e)
    return pl.pallas_call(
        _filter_spectrum_kernel,
        grid=(e // cb,),
        in_specs=[pl.BlockSpec((FFT_N, cb), lambda c: (0, c)),
                  _resident(tables["s1_full"].shape, 1), _resident(tables["g"].shape, 1)],
        out_specs=pl.BlockSpec((FFT_K1, 2 * FFT_N2, cb), lambda c: (0, 0, c)),
        out_shape=jax.ShapeDtypeStruct((FFT_K1, 2 * FFT_N2, e), F32),
        scratch_shapes=[pltpu.VMEM((FFT_K1, 2 * FFT_N2, cb), F32)],
        compiler_params=_params(("arbitrary",)),
        name="filter_spectrum",
    )(kfull, tables["s1_full"], tables["g"])


def _short_conv(x_ref, cw_ref, cb_ref, group, c, stage_ref, n_rows):
    r0 = pl.multiple_of(c * SHORT_CHUNK, SHORT_CHUNK)
    prev0 = pl.multiple_of(jnp.maximum(r0 - BF16_ROWS, 0), BF16_ROWS)
    next0 = pl.multiple_of(jnp.minimum(r0 + SHORT_CHUNK, n_rows - BF16_ROWS), BF16_ROWS)
    prev = x_ref[pl.ds(prev0, BF16_ROWS), :].astype(F32)[BF16_ROWS - 1:BF16_ROWS, :] * jnp.where(c > 0, 1.0, 0.0)
    nxt = x_ref[pl.ds(next0, BF16_ROWS), :].astype(F32)[0:1, :] * jnp.where(r0 + SHORT_CHUNK < n_rows, 1.0, 0.0)
    stage_ref[STAGE_PAD - 1:STAGE_PAD, :] = prev
    stage_ref[STAGE_PAD:STAGE_PAD + SHORT_CHUNK, :] = x_ref[pl.ds(r0, SHORT_CHUNK), :].astype(F32)
    stage_ref[STAGE_PAD + SHORT_CHUNK:STAGE_PAD + SHORT_CHUNK + 1, :] = nxt
    w = cw_ref[group]
    return (cb_ref[group]
            + w[0:1] * stage_ref[STAGE_PAD - 1:STAGE_PAD - 1 + SHORT_CHUNK, :]
            + w[1:2] * stage_ref[STAGE_PAD:STAGE_PAD + SHORT_CHUNK, :]
            + w[2:3] * stage_ref[STAGE_PAD + 1:STAGE_PAD + 1 + SHORT_CHUNK, :])


def _long_conv_kernel(x0_ref, x1_ref, v_ref, zs_ref, cw_ref, cb_ref, hb_ref, kf_ref,
                      s1_ref, g_ref, ginv_ref, i2_ref, o_ref, w_ref, ab_ref, stage_ref):
    n_rows = x1_ref.shape[0]
    n_chunks = n_rows // SHORT_CHUNK

    def make_w(c, carry):
        rows = pl.ds(pl.multiple_of(c * SHORT_CHUNK, SHORT_CHUNK), SHORT_CHUNK)
        x1c = _short_conv(x1_ref, cw_ref, cb_ref, 1, c, stage_ref, n_rows)
        vc = _short_conv(v_ref, cw_ref, cb_ref, 2, c, stage_ref, n_rows)
        w_ref[rows, :] = x1c * vc
        return carry
    lax.fori_loop(0, n_chunks, make_w, 0)

    _outer_forward(w_ref, s1_ref, ab_ref, FFT_N1 // 2)

    def per_k1(k1, carry):
        x = jnp.dot(g_ref[k1], ab_ref[k1].astype(BF16), preferred_element_type=F32)
        kf = kf_ref[k1]
        xr, xi = x[:FFT_N2], x[FFT_N2:]
        kr, ki = kf[:FFT_N2], kf[FFT_N2:]
        y = jnp.concatenate([xr * kr - xi * ki, xr * ki + xi * kr], axis=0).astype(BF16)
        ab_ref[k1] = jnp.dot(ginv_ref[k1], y, preferred_element_type=F32)
        return carry
    lax.fori_loop(0, FFT_K1, per_k1, 0, unroll=K1_UNROLL)

    def outer_inverse(n2h, carry):
        r0 = pl.multiple_of(n2h * SUBLANES, SUBLANES)
        tiles = [ab_ref[k1, pl.ds(ri * FFT_N2 + r0, SUBLANES), :] for k1 in range(FFT_K1) for ri in range(2)
                 if 2 * k1 + ri not in _REAL_ONLY_COLS]
        rhs = jnp.concatenate(tiles, axis=0).astype(BF16)
        out = jnp.dot(i2_ref[...], rhs, preferred_element_type=F32)
        for n1 in range(FFT_N1 // 2):
            rows = pl.ds(n1 * FFT_N2 + r0, SUBLANES)
            w_ref[rows, :] = out[n1 * SUBLANES:(n1 + 1) * SUBLANES, :] + hb_ref[...] * w_ref[rows, :]
        return carry
    lax.fori_loop(0, FFT_N2 // SUBLANES, outer_inverse, 0, unroll=OUTER_UNROLL)

    def finish(c, carry):
        rows = pl.ds(pl.multiple_of(c * SHORT_CHUNK, SHORT_CHUNK), SHORT_CHUNK)
        x0c = _short_conv(x0_ref, cw_ref, cb_ref, 0, c, stage_ref, n_rows)
        o_ref[rows, :] = (x0c * w_ref[rows, :] * zs_ref[rows, :].astype(F32)).astype(o_ref.dtype)
        return carry
    lax.fori_loop(0, n_chunks, finish, 0)


def _long_conv(x0, x1, v, zs, conv_w, conv_b, h_bias, kf, tables, *, cb=MXU_TILE):
    bsz, seq, e = x0.shape
    assert seq == FFT_N // 2
    cb = min(cb, e)
    cw = conv_w.reshape(3, 3, e).transpose(1, 0, 2)
    cbias = conv_b.reshape(3, 1, e)
    tok = pl.BlockSpec((None, seq, cb), lambda c, b: (b, 0, c))
    consts = [tables["s1_half"], tables["g"], tables["ginv"], tables["i2"]]
    return pl.pallas_call(
        _long_conv_kernel,
        grid=(e // cb, bsz),
        in_specs=[tok, tok, tok, tok,
                  pl.BlockSpec((3, 3, cb), lambda c, b: (0, 0, c)),
                  pl.BlockSpec((3, 1, cb), lambda c, b: (0, 0, c)),
                  pl.BlockSpec((1, cb), lambda c, b: (0, c)),
                  pl.BlockSpec((FFT_K1, 2 * FFT_N2, cb), lambda c, b: (0, 0, c), pipeline_mode=pl.Buffered(1))]
                 + [_resident(a.shape, 2) for a in consts],
        out_specs=tok,
        out_shape=jax.ShapeDtypeStruct((bsz, seq, e), BF16),
        scratch_shapes=[pltpu.VMEM((seq, cb), F32),
                        pltpu.VMEM((FFT_K1, 2 * FFT_N2, cb), F32),
                        pltpu.VMEM((SHORT_CHUNK + 2 * STAGE_PAD, cb), F32)],
        compiler_params=_params(("arbitrary", "arbitrary")),
        name="long_conv",
    )(x0, x1, v, zs, cw, cbias, h_bias.reshape(1, e), kf, *consts)


def _final_norm_kernel(h_ref, g_ref, o_ref):
    x = h_ref[...]
    ms = jnp.mean(x * x, axis=-1, keepdims=True)
    o_ref[...] = x * lax.rsqrt(ms + EPS) * g_ref[...]


def _final_norm(h, g, *, tm=1024):
    bn, t, d = h.shape
    m = bn * t
    tm = min(tm, m)
    out = pl.pallas_call(
        _final_norm_kernel,
        grid=(m // tm,),
        in_specs=[pl.BlockSpec((tm, d), lambda i: (i, 0)),
                  pl.BlockSpec((1, d), lambda i: (0, 0))],
        out_specs=pl.BlockSpec((tm, d), lambda i: (i, 0)),
        out_shape=jax.ShapeDtypeStruct((m, d), F32),
        compiler_params=_params(("arbitrary",)),
        name="final_norm",
    )(h.reshape(m, d), g.reshape(1, d))
    return out.reshape(bn, t, d)


def kernel(x, c, ctx, c_ctx, norm_g, ada_w, ada_b, final_g, cv_w_in, cv_dw_w, cv_dw_b, cv_ln_g, cv_ln_b, cv_w_out, ml_w_in, ml_conv_w, ml_conv_b, ml_w_q, ml_w_k, ml_w_v, ml_w_o, ml_b_o, ml_w_gates, ml_b_gates, ml_mh_g, ml_skip, ml_w_out, hy_w_in, hy_conv_w, hy_conv_b, hy_f_w1, hy_f_b1, hy_f_freq1, hy_f_w2, hy_f_b2, hy_f_freq2, hy_f_w3, hy_f_b3, hy_h_bias, hy_w_out):
    depth = norm_g.shape[0]
    bsz, seq, d = x.shape
    n_ctx = ctx.shape[1]
    readers = [i for i in range(depth) if i % N_MIXERS == 1]
    last_reader = readers[-1] if readers else -1

    cond_rows = -(-(bsz + 1) // SUBLANES) * SUBLANES
    cond = jnp.concatenate([c, c_ctx[None], jnp.zeros((cond_rows - bsz - 1, d), F32)], axis=0)
    ada = _ada_params(cond, ada_w, ada_b)

    h_lat, h_ctx = x, ctx
    norm_fused = (depth - 1) % N_MIXERS != 1
    for i in range(depth):
        kind, j = i % N_MIXERS, i // N_MIXERS
        close_g = final_g if (norm_fused and i == depth - 1) else None
        ctx_in = i <= last_reader
        ctx_out = i < last_reader
        sh, sc, gt = jnp.split(ada[i, :bsz], 3, axis=-1)
        sh_c, sc_c, gt_c = (jnp.broadcast_to(a, (bsz, d)) for a in jnp.split(ada[i, bsz:bsz + 1], 3, axis=-1))
        if kind == 0:
            w_in = cv_w_in[j].astype(BF16)
            w_out = cv_w_out[j].astype(BF16)
            streams = [(h_lat, sh, sc, gt, GRID_W, _conv_mid_dft)]
            if ctx_out:
                streams.append((h_ctx.reshape(1, bsz * n_ctx, d), sh_c[:1], sc_c[:1], gt_c[:1], n_ctx, _conv_mid))
            new = []
            for h, s_h, s_c, g_t, seg, conv_mid in streams:
                y, zs = _in_proj(h, norm_g[i], s_h, s_c, w_in, n_groups=3, n_out=2, epilogue=_epilogue_conv,
                                 tm=2 * IN_PROJ_ROWS)
                yb = conv_mid(y, zs, cv_dw_w[j], cv_dw_b[j], cv_ln_g[j], cv_ln_b[j], seg=seg)
                new.append(_out_matmul(yb, w_out, h, g_t, final_g=close_g if h is h_lat else None))
            h_lat = new[0]
            if ctx_out:
                h_ctx = new[1].reshape(bsz, n_ctx, d)
        elif kind == 1:
            assert ctx_in and not ctx_out
            w_in = ml_w_in[j].astype(BF16)
            w_out = ml_w_out[j].astype(BF16)
            lead = (-n_ctx) % IN_PROJ_ROWS
            h_all = jnp.concatenate([jnp.zeros((bsz, lead, d), F32), h_ctx, h_lat], axis=1)
            sh_all = jnp.concatenate([sh, sh_c[:1]], axis=0)
            sc_all = jnp.concatenate([sc, sc_c[:1]], axis=0)
            xm, zs = _in_proj(h_all, norm_g[i], sh_all, sc_all, w_in, n_groups=2, n_out=2,
                              epilogue=_epilogue_gate_last, n_ctx=lead + n_ctx, tm=IN_PROJ_ROWS, tn=1024)
            q, k, kt, v, o, xc, gc, gr = _mlstm_pre(
                xm, ml_conv_w[j], ml_conv_b[j], ml_w_q[j], ml_w_k[j], ml_w_v[j], ml_w_o[j], ml_b_o[j],
                ml_w_gates[j], ml_b_gates[j], lead=lead, n_ctx=n_ctx, n_heads=MLSTM_HEADS)
            hs = _mlstm_scan(q, k, kt, v, gc, gr, n_heads=MLSTM_HEADS, n_ctx=n_ctx)
            h_lat = _mlstm_out(hs, o, xc, zs, ml_mh_g[j], ml_skip[j], w_out, h_lat, gt,
                               lead=lead, n_ctx=n_ctx, n_heads=MLSTM_HEADS)
        else:
            assert not ctx_out
            w_in = hy_w_in[j].astype(BF16)
            w_out = hy_w_out[j].astype(BF16)
            x0, x1, v, zs = _in_proj(h_lat, norm_g[i], sh, sc, w_in, n_groups=4, n_out=4,
                                     epilogue=_epilogue_gate_last, tm=2 * IN_PROJ_ROWS, tn=256)
            tables = _dft_tables()
            kfull = _hyena_filter(seq, hy_f_w1[j], hy_f_b1[j], hy_f_freq1[j], hy_f_w2[j], hy_f_b2[j],
                                  hy_f_freq2[j], hy_f_w3[j], hy_f_b3[j])
            kf = _filter_spectrum(kfull, tables)
            yb = _long_conv(x0, x1, v, zs, hy_conv_w[j], hy_conv_b[j], hy_h_bias[j], kf, tables)
            h_lat = _out_matmul(yb, w_out, h_lat, gt, final_g=close_g)
    return h_lat if norm_fused else _final_norm(h_lat, final_g)
```
